```python
import jax
import jax.numpy as jnp
from jax import lax
import numpy as np

D_MODEL = 2048
BATCH = 1
SEQ = 8192
DEPTH = 2

POOL_WIDTH = D_MODEL // 2
POOL_WINDOWS = (2, 4, 8, 16)
POOL_GROUPS = len(POOL_WINDOWS)
POOL_GROUP_DIM = POOL_WIDTH // POOL_GROUPS
RWKV_WIDTH = D_MODEL // 2
RWKV_HEAD_DIM = 64
RWKV_HEADS = RWKV_WIDTH // RWKV_HEAD_DIM
DECAY_RANK = 64
ICLR_RANK = 64
VRES_RANK = 32
GATE_RANK = 160
IN_WIDTH = POOL_WIDTH + 3 * RWKV_WIDTH + 2 * D_MODEL
D_FF_DENSE = 256 * ((8 * D_MODEL // 3 + 255) // 256)
N_EXPERTS = 8
TOP_K = 2
D_FF_EXPERT = 7 * D_MODEL // 2
MOE_BLOCK = 256
N_DENSE_LAYERS = (DEPTH + 1) // 2
N_MOE_LAYERS = DEPTH // 2
N_VRES_LAYERS = DEPTH - 1
N_MOD = 6
NORM_EPS = 1e-6
GROUPNORM_EPS = 64e-5
L2_EPS = 1e-12
F32 = jnp.float32

kernel_name = 'hybrid_pool_rwkv7_moe_block'


def rms_norm(z):
    z32 = z.astype(F32)
    y = z32 * lax.rsqrt(jnp.mean(z32 * z32, axis=-1, keepdims=True) + NORM_EPS)
    return y.astype(z.dtype)


def token_shift(u):
    return jnp.pad(u, ((0, 0), (1, 0), (0, 0)))[:, :-1]


def swiglu(h, w1, w3, w2):
    return (jax.nn.silu(h @ w1) * (h @ w3)) @ w2


def pool_mixer(u, w_group, pool_scale):
    B, T, _ = u.shape
    ug = u.astype(F32).reshape(B, T, POOL_GROUPS, POOL_GROUP_DIM)
    cs = jnp.pad(jnp.cumsum(ug, axis=1), ((0, 0), (1, 0), (0, 0), (0, 0)))
    pos = jnp.arange(T)
    outs = []
    for g, win in enumerate(POOL_WINDOWS):
        lo = jnp.maximum(pos + 1 - win, 0)
        count = jnp.minimum(pos + 1, win).astype(F32)
        win_mean = (cs[:, 1:, g] - cs[:, lo, g]) / count[None, :, None]
        outs.append(win_mean - ug[:, :, g])
    pooled = jnp.stack(outs, axis=2)
    mixed = jnp.einsum('btgc,gcd->btgd', pooled, w_group.astype(F32))
    return (mixed.reshape(B, T, POOL_WIDTH) * pool_scale).astype(u.dtype)


def rwkv7_time_mix(h, p_r, p_k, p_v, shift_mu, lerp_mu, w0, w1, w2, a0, a1, a2,
                   g1, g2, k_k, k_a, r_k, gn_w, gn_b, v_first, vres):
    B, T, _ = h.shape
    H, N = RWKV_HEADS, RWKV_HEAD_DIM
    h32 = h.astype(F32)
    dh = token_shift(h32) - h32
    xw = h32 + dh * lerp_mu[0]
    xa = h32 + dh * lerp_mu[1]
    xg = h32 + dh * lerp_mu[2]

    def shift_mix(p, mu):
        p = p.astype(F32)
        return p + (token_shift(p) - p) * mu

    r = shift_mix(p_r, shift_mu[0])
    k = shift_mix(p_k, shift_mu[1])
    v = shift_mix(p_v, shift_mu[2])
    w = -jax.nn.softplus(-(w0 + jnp.tanh(xw @ w1) @ w2)) - 0.5
    decay = jnp.exp(-jnp.exp(w))
    if vres is None:
        v_first = v
    else:
        mu_v, v0, v1, v2 = vres
        xv = h32 + dh * mu_v
        v = v + (v_first - v) * jax.nn.sigmoid(v0 + (xv @ v1) @ v2)
    a = jax.nn.sigmoid(a0 + (xa @ a1) @ a2)
    g = jax.nn.sigmoid(xg @ g1) @ g2
    kk = (k * k_k).reshape(B, T, H, N)
    kk = kk / jnp.maximum(jnp.linalg.norm(kk, axis=-1, keepdims=True), L2_EPS)
    kk = kk.reshape(B, T, RWKV_WIDTH)
    k = k * (1.0 + (a - 1.0) * k_a)

    def heads_t(z):
        return z.reshape(B, T, H, N).transpose(1, 0, 2, 3)

    def step(state, inp):
        r_t, w_t, k_t, v_t, a_t, b_t = inp
        sa = jnp.einsum('bhvk,bhk->bhv', state, a_t)
        state = (state * w_t[:, :, None, :] + sa[..., None] * b_t[:, :, None, :]
                 + v_t[..., None] * k_t[:, :, None, :])
        return state, jnp.einsum('bhvk,bhk->bhv', state, r_t)

    s0 = jnp.zeros((B, H, N, N), F32)
    _, y = lax.scan(step, s0, (heads_t(r), heads_t(decay), heads_t(k), heads_t(v),
                               heads_t(-kk), heads_t(kk * a)))
    y = y.transpose(1, 0, 2, 3)
    mean = jnp.mean(y, axis=-1, keepdims=True)
    var = jnp.mean(jnp.square(y - mean), axis=-1, keepdims=True)
    y = ((y - mean) * lax.rsqrt(var + GROUPNORM_EPS)).reshape(B, T, RWKV_WIDTH) * gn_w + gn_b
    rh = r.reshape(B, T, H, N)
    kh = k.reshape(B, T, H, N)
    vh = v.reshape(B, T, H, N)
    bonus = (jnp.sum(rh * kh * r_k, axis=-1, keepdims=True) * vh).reshape(B, T, RWKV_WIDTH)
    return (y + bonus) * g, v_first


def moe_swiglu(h, router_w, w1, w3, w2):
    B, T, D = h.shape
    n_tok = B * T
    n_assign = n_tok * TOP_K
    xf = h.reshape(n_tok, D)
    logits = (xf @ router_w).astype(F32)
    top_logit, top_idx = lax.top_k(logits, TOP_K)
    gate = jax.nn.softmax(top_logit, axis=-1)
    flat_e = top_idx.reshape(-1)
    flat_tok = jnp.repeat(jnp.arange(n_tok, dtype=jnp.int32), TOP_K)
    order = jnp.argsort(flat_e)
    sorted_e = flat_e[order]
    sorted_tok = flat_tok[order]
    sorted_gate = gate.reshape(-1)[order]
    counts = jnp.bincount(flat_e, length=N_EXPERTS)
    padded = (counts + MOE_BLOCK - 1) // MOE_BLOCK * MOE_BLOCK
    pad_end = jnp.cumsum(padded)
    pad_start = pad_end - padded
    start = jnp.cumsum(counts) - counts
    dest = pad_start[sorted_e] + jnp.arange(n_assign) - start[sorted_e]
    n_blocks = -(-n_assign // MOE_BLOCK) + N_EXPERTS
    cap = n_blocks * MOE_BLOCK
    slot_tok = jnp.zeros((cap,), jnp.int32).at[dest].set(sorted_tok)
    block_start = jnp.arange(n_blocks) * MOE_BLOCK
    block_e = jnp.minimum(jnp.searchsorted(pad_end, block_start, side='right'), N_EXPERTS - 1)
    xb = xf[slot_tok].reshape(n_blocks, MOE_BLOCK, D)

    def expert_block(args):
        xblk, e = args
        return swiglu(xblk, w1[e], w3[e], w2[e])

    yb = lax.map(expert_block, (xb, block_e)).reshape(cap, D)
    contrib = yb[dest].astype(F32) * sorted_gate[:, None]
    out = jnp.zeros((n_tok, D), F32).at[sorted_tok].add(contrib)
    return out.reshape(B, T, D).astype(h.dtype)


def setup_inputs(seed: int = 0) -> dict:
    key = jax.random.key(seed)
    keys = iter(jax.random.split(key, 64))
    D, C, H, N = D_MODEL, RWKV_WIDTH, RWKV_HEADS, RWKV_HEAD_DIM

    def nrm(shape, scale):
        return jax.random.normal(next(keys), shape, F32) * scale

    def uni(shape, lo, hi):
        return jax.random.uniform(next(keys), shape, F32, lo, hi)

    return {
        'x': nrm((BATCH, SEQ, D), 1.0),
        'c': nrm((BATCH, D), 1.0),
        'ada_w': nrm((DEPTH, D, N_MOD * D), D ** -0.5),
        'ada_b': nrm((DEPTH, N_MOD * D), 0.02),
        'w_in': nrm((DEPTH, D, IN_WIDTH), D ** -0.5),
        'pool_w': nrm((DEPTH, POOL_GROUPS, POOL_GROUP_DIM, POOL_GROUP_DIM), POOL_GROUP_DIM ** -0.5),
        'pool_scale': 1.0 + nrm((DEPTH, POOL_WIDTH), 0.1),
        'pool_out': nrm((DEPTH, POOL_WIDTH, D), POOL_WIDTH ** -0.5),
        'rk_shift': uni((DEPTH, 3, C), 0.0, 1.0),
        'rk_lerp': uni((DEPTH, 3, D), 0.0, 1.0),
        'rk_w0': uni((DEPTH, C), -5.0, -1.0),
        'rk_w1': nrm((DEPTH, D, DECAY_RANK), D ** -0.5),
        'rk_w2': nrm((DEPTH, DECAY_RANK, C), 0.1 * DECAY_RANK ** -0.5),
        'rk_a0': nrm((DEPTH, C), 0.5),
        'rk_a1': nrm((DEPTH, D, ICLR_RANK), D ** -0.5),
        'rk_a2': nrm((DEPTH, ICLR_RANK, C), 0.1 * ICLR_RANK ** -0.5),
        'rk_g1': nrm((DEPTH, D, GATE_RANK), D ** -0.5),
        'rk_g2': nrm((DEPTH, GATE_RANK, C), GATE_RANK ** -0.5),
        'rk_kk': 0.85 + nrm((DEPTH, C), 0.05),
        'rk_ka': 1.0 + nrm((DEPTH, C), 0.05),
        'rk_rk': nrm((DEPTH, H, N), 0.1),
        'rk_gn_w': 1.0 + nrm((DEPTH, C), 0.05),
        'rk_gn_b': nrm((DEPTH, C), 0.02),
        'rk_vmu': uni((N_VRES_LAYERS, D), 0.0, 1.0),
        'rk_v0': nrm((N_VRES_LAYERS, C), 0.5),
        'rk_v1': nrm((N_VRES_LAYERS, D, VRES_RANK), D ** -0.5),
        'rk_v2': nrm((N_VRES_LAYERS, VRES_RANK, C), 0.1 * VRES_RANK ** -0.5),
        'rk_out': nrm((DEPTH, C, D), C ** -0.5),
        'mix_out': nrm((DEPTH, D, D), D ** -0.5),
        'ffn_w1': nrm((N_DENSE_LAYERS, D, D_FF_DENSE), D ** -0.5),
        'ffn_w3': nrm((N_DENSE_LAYERS, D, D_FF_DENSE), D ** -0.5),
        'ffn_w2': nrm((N_DENSE_LAYERS, D_FF_DENSE, D), D_FF_DENSE ** -0.5),
        'router': nrm((N_MOE_LAYERS, D, N_EXPERTS), D ** -0.5),
        'moe_w1': nrm((N_MOE_LAYERS, N_EXPERTS, D, D_FF_EXPERT), D ** -0.5),
        'moe_w3': nrm((N_MOE_LAYERS, N_EXPERTS, D, D_FF_EXPERT), D ** -0.5),
        'moe_w2': nrm((N_MOE_LAYERS, N_EXPERTS, D_FF_EXPERT, D), D_FF_EXPERT ** -0.5),
        'final_gain': 1.0 + nrm((D,), 0.05),
    }


def reference(x, c, ada_w, ada_b, w_in, pool_w, pool_scale, pool_out, rk_shift, rk_lerp,
              rk_w0, rk_w1, rk_w2, rk_a0, rk_a1, rk_a2, rk_g1, rk_g2, rk_kk, rk_ka, rk_rk,
              rk_gn_w, rk_gn_b, rk_vmu, rk_v0, rk_v1, rk_v2, rk_out, mix_out,
              ffn_w1, ffn_w3, ffn_w2, router, moe_w1, moe_w3, moe_w2, final_gain):
    B, T, D = x.shape
    c_act = jax.nn.silu(c.astype(F32))
    split_at = [POOL_WIDTH,
                POOL_WIDTH + RWKV_WIDTH,
                POOL_WIDTH + 2 * RWKV_WIDTH,
                POOL_WIDTH + 3 * RWKV_WIDTH,
                POOL_WIDTH + 3 * RWKV_WIDTH + D_MODEL]
    v_first = None
    for l in range(DEPTH):
        mod = (c_act @ ada_w[l] + ada_b[l]).reshape(B, N_MOD, 1, D).astype(x.dtype)
        shift1, scale1, gate1, shift2, scale2, gate2 = (mod[:, i] for i in range(N_MOD))

        h = rms_norm(x) * (1.0 + scale1) + shift1
        p_pool, p_r, p_k, p_v, gl_pool, gl_rwkv = jnp.split(h @ w_in[l], split_at, axis=-1)
        y_pool = pool_mixer(p_pool, pool_w[l], pool_scale[l]) @ pool_out[l]
        vres = None if l == 0 else (rk_vmu[l - 1], rk_v0[l - 1], rk_v1[l - 1], rk_v2[l - 1])
        y_rwkv, v_first = rwkv7_time_mix(
            h, p_r, p_k, p_v, rk_shift[l], rk_lerp[l], rk_w0[l], rk_w1[l], rk_w2[l],
            rk_a0[l], rk_a1[l], rk_a2[l], rk_g1[l], rk_g2[l], rk_kk[l], rk_ka[l], rk_rk[l],
            rk_gn_w[l], rk_gn_b[l], v_first, vres)
        y_rwkv = y_rwkv.astype(x.dtype) @ rk_out[l]
        merged = jax.nn.sigmoid(gl_pool) * y_pool + jax.nn.sigmoid(gl_rwkv) * y_rwkv
        x = x + gate1 * (merged @ mix_out[l])

        h = rms_norm(x) * (1.0 + scale2) + shift2
        if l % 2 == 0:
            f = swiglu(h, ffn_w1[l // 2], ffn_w3[l // 2], ffn_w2[l // 2])
        else:
            f = moe_swiglu(h, router[l // 2], moe_w1[l // 2], moe_w3[l // 2], moe_w2[l // 2])
        x = x + gate2 * f
    return rms_norm(x) * final_gain
```

```python
import functools

import jax
import jax.numpy as jnp
from jax import lax
from jax.experimental import pallas as pl
from jax.experimental.pallas import tpu as pltpu

F32 = jnp.float32
BF16 = jnp.bfloat16

NORM_EPS = 1e-6
GROUPNORM_EPS = 64e-5
L2_EPS = 1e-12
POOL_WINDOWS = (2, 4, 8, 16)
HEAD_DIM = 64
TOP_K = 2
LANES = 128
SUBLANES = 8
CHUNK = 64
LR_PAD = 384
POOL_HALO = 16
VMEM_LIMIT = 56 * 1024 * 1024


def _cparams(sem):
    return pltpu.CompilerParams(dimension_semantics=sem, vmem_limit_bytes=VMEM_LIMIT)


def _mm(a, b):
    return jnp.dot(a.astype(BF16), b.astype(BF16), preferred_element_type=F32)


def _mm_nt(a, b):
    return lax.dot_general(a.astype(BF16), b.astype(BF16), (((1,), (1,)), ((), ())),
                           preferred_element_type=F32)


def _split(x):
    hi = x.astype(BF16)
    lo = (x - hi.astype(F32)).astype(BF16)
    return hi, lo


def _mm_exact_rhs(x, m):
    hi, lo = _split(x)
    return (jnp.dot(hi, m, preferred_element_type=F32) + jnp.dot(lo, m, preferred_element_type=F32))


def _ada_kernel(c_ref, w_ref, b_ref, o_ref):
    c = c_ref[...]
    act = c * jax.nn.sigmoid(c)
    o_ref[0] = jnp.sum(act * w_ref[0], axis=0, keepdims=True) + b_ref[0]


def _ada(c_col, ada_w, ada_b):
    depth, d, n = ada_w.shape
    tn = 1024
    return pl.pallas_call(
        _ada_kernel,
        grid=(depth, n // tn),
        in_specs=[pl.BlockSpec((d, 1), lambda l, j: (0, 0)),
                  pl.BlockSpec((1, d, tn), lambda l, j: (l, 0, j)),
                  pl.BlockSpec((1, 1, tn), lambda l, j: (l, 0, j))],
        out_specs=pl.BlockSpec((1, 1, tn), lambda l, j: (l, 0, j)),
        out_shape=jax.ShapeDtypeStruct((depth, 1, n), F32),
        compiler_params=_cparams(("arbitrary", "arbitrary")),
        name="ada_gemv",
    )(c_col, ada_w, ada_b.reshape(depth, 1, n))


def _in_kernel(x_ref, sc_ref, sh_ref, w_ref, wlr_ref, o_ref, olr_ref, h_sc):
    @pl.when(pl.program_id(1) == 0)
    def _():
        x = x_ref[...]
        ms = jnp.mean(x * x, axis=-1, keepdims=True)
        h = x * lax.rsqrt(ms + NORM_EPS) * (1.0 + sc_ref[...]) + sh_ref[...]
        hb = h.astype(BF16)
        h_sc[...] = hb
        olr_ref[...] = jnp.dot(hb, wlr_ref[...], preferred_element_type=F32)

    o_ref[...] = jnp.dot(h_sc[...], w_ref[...].astype(BF16), preferred_element_type=F32)


def _in_proj(x, scale, shift, w_in, w_lr):
    t, d = x.shape
    n = w_in.shape[1]
    nlr = w_lr.shape[1]
    tm, tn = 512, 1024
    return pl.pallas_call(
        _in_kernel,
        grid=(t // tm, n // tn),
        in_specs=[pl.BlockSpec((tm, d), lambda i, j: (i, 0)),
                  pl.BlockSpec((1, d), lambda i, j: (0, 0)),
                  pl.BlockSpec((1, d), lambda i, j: (0, 0)),
                  pl.BlockSpec((d, tn), lambda i, j: (0, j)),
                  pl.BlockSpec((d, nlr), lambda i, j: (0, 0))],
        out_specs=[pl.BlockSpec((tm, tn), lambda i, j: (i, j)),
                   pl.BlockSpec((tm, nlr), lambda i, j: (i, 0))],
        out_shape=[jax.ShapeDtypeStruct((t, n), F32), jax.ShapeDtypeStruct((t, nlr), F32)],
        scratch_shapes=[pltpu.VMEM((tm, d), BF16)],
        compiler_params=_cparams(("arbitrary", "arbitrary")),
        name="in_proj",
    )(x, scale, shift, w_in, w_lr)


PV_W0, PV_A0, PV_V0, PV_KK, PV_KA, PV_GNW, PV_GNB, PV_MUR, PV_MUK, PV_MUV, PV_RK = range(11)
PV_ROWS = 16


def _rwkv_kernel(*refs, tt, has_vres):
    if has_vres:
        (r_ref, k_ref, v_ref, lr_ref, vf_ref, w2_ref, pv_ref, yr_ref, s_sc, c_rkv, c_lr) = refs
    else:
        (r_ref, k_ref, v_ref, lr_ref, w2_ref, pv_ref, yr_ref, vfo_ref, s_sc, c_rkv, c_lr) = refs

    @pl.when(pl.program_id(1) == 0)
    def _():
        s_sc[...] = jnp.zeros_like(s_sc)
        c_rkv[...] = jnp.zeros_like(c_rkv)
        c_lr[...] = jnp.zeros_like(c_lr)

    row = lax.broadcasted_iota(jnp.int32, (tt, 1), 0)

    def shift(x, carry_row):
        return jnp.where(row == 0, carry_row, pltpu.roll(x, 1, axis=0))

    pv = pv_ref[...]

    def prow(i):
        return pv[i:i + 1, :]

    r_raw, k_raw, v_raw = r_ref[...], k_ref[...], v_ref[...]
    zb = lr_ref[:, LR_PAD:2 * LR_PAD]
    r_prev = shift(r_raw, c_rkv[0:1, :])
    k_prev = shift(k_raw, c_rkv[1:2, :])
    v_prev = shift(v_raw, c_rkv[2:3, :])
    z = lr_ref[:, 0:LR_PAD] + shift(zb, c_lr[0:1, :])
    c_rkv[0:1, :] = r_raw[tt - 1:tt, :]
    c_rkv[1:2, :] = k_raw[tt - 1:tt, :]
    c_rkv[2:3, :] = v_raw[tt - 1:tt, :]
    c_lr[0:1, :] = zb[tt - 1:tt, :]

    r = r_raw + (r_prev - r_raw) * prow(PV_MUR)
    k = k_raw + (k_prev - k_raw) * prow(PV_MUK)
    v = v_raw + (v_prev - v_raw) * prow(PV_MUV)

    z_wa = z[:, 0:LANES]
    z_gv = z[:, LANES:LR_PAD]
    w_pre = prow(PV_W0) + _mm(jnp.tanh(z_wa), w2_ref[0:128, :])
    neg = -w_pre
    softplus = jnp.maximum(neg, 0.0) + jnp.log(1.0 + jnp.exp(-jnp.abs(neg)))
    logd = -jnp.exp(-softplus - 0.5)
    a_i = jax.nn.sigmoid(prow(PV_A0) + _mm(z_wa, w2_ref[128:256, :]))
    g = _mm(jax.nn.sigmoid(z_gv), w2_ref[256:512, :])
    if has_vres:
        v = v + (vf_ref[...] - v) * jax.nn.sigmoid(prow(PV_V0) + _mm(z_gv, w2_ref[512:768, :]))
    else:
        vfo_ref[...] = v

    li = lax.broadcasted_iota(jnp.int32, (LANES, LANES), 0)
    lj = lax.broadcasted_iota(jnp.int32, (LANES, LANES), 1)
    same_head = ((li >> 6) == (lj >> 6))
    head_ones = jnp.where(same_head, 1.0, 0.0).astype(BF16)

    def head_sum(x):
        return _mm_exact_rhs(x, head_ones)

    kk = k * prow(PV_KK)
    kk = kk / jnp.maximum(jnp.sqrt(head_sum(kk * kk)), L2_EPS)
    k2 = k * (1.0 + (a_i - 1.0) * prow(PV_KA))
    a_vec = -kk
    b_vec = kk * a_i
    bonus = head_sum(r * k2 * prow(PV_RK)) * v

    ci = lax.broadcasted_iota(jnp.int32, (CHUNK, CHUNK), 0)
    cj = lax.broadcasted_iota(jnp.int32, (CHUNK, CHUNK), 1)
    tri_incl = jnp.where(ci >= cj, 1.0, 0.0).astype(BF16)
    strict = (li & (CHUNK - 1)) > (lj & (CHUNK - 1))
    incl = (li & (CHUNK - 1)) >= (lj & (CHUNK - 1))
    eye = li == lj
    eye_f = jnp.where(eye, 1.0, 0.0)
    off_masks = [((li >> (s + 1)) == (lj >> (s + 1))) & (((li >> s) & 1) == 1) & (((lj >> s) & 1) == 0)
                 for s in range(CHUNK.bit_length() - 1)]
    head0 = lax.broadcasted_iota(jnp.int32, (CHUNK, LANES), 1) < HEAD_DIM

    def stack(x):
        return jnp.concatenate([jnp.where(head0, x, 0.0), jnp.where(head0, 0.0, x)], axis=0)

    for c in range(tt // CHUNK):
        sl = slice(c * CHUNK, (c + 1) * CHUNK)
        ld = logd[sl]
        ld_hi, ld_lo = _split(ld)
        lp = (jnp.dot(tri_incl, ld_hi, preferred_element_type=F32)
              + jnp.dot(tri_incl, ld_lo, preferred_element_type=F32))
        le = lp[CHUNK - 1:CHUNK, :]
        e_p = jnp.exp(lp)
        e_n = jnp.exp(-lp)
        e_e = jnp.exp(le - lp)
        rh = stack(r[sl] * e_p)
        ah = stack(a_vec[sl] * jnp.exp(lp - ld))
        bt = stack(b_vec[sl] * e_n)
        kt = stack(k2[sl] * e_n)
        bb = stack(b_vec[sl] * e_e)
        kb = stack(k2[sl] * e_e)
        vv = stack(v[sl])

        a_ab = jnp.where(strict, _mm_nt(ah, bt), 0.0)
        a_ak = jnp.where(strict, _mm_nt(ah, kt), 0.0)
        a_rb = jnp.where(incl, _mm_nt(rh, bt), 0.0)
        a_rk = jnp.where(incl, _mm_nt(rh, kt), 0.0)
        tm_ = eye_f + jnp.where(off_masks[0], a_ab, 0.0)
        for off in off_masks[1:]:
            tm_ = tm_ + _mm(_mm(tm_, jnp.where(off, a_ab, 0.0)), tm_)
        x_ = _mm(a_ak, vv)
        w1 = _mm(tm_, ah)
        w2 = _mm(tm_, x_)
        q = rh + _mm(a_rb, w1)
        y_in = _mm(a_rb, w2) + _mm(a_rk, vv)
        bb_t = bb.T
        m_ = jnp.where(eye, jnp.exp(le), 0.0) + _mm(bb_t, w1)
        c_ = _mm(bb_t, w2) + _mm(kb.T, vv)
        s_old = s_sc[...]
        ys = _mm(q, s_old) + y_in
        s_sc[...] = _mm(m_, s_old) + c_
        y = ys[0:CHUNK] + ys[CHUNK:2 * CHUNK]

        mean = head_sum(y) * (1.0 / HEAD_DIM)
        dlt = y - mean
        var = head_sum(dlt * dlt) * (1.0 / HEAD_DIM)
        yn = dlt * lax.rsqrt(var + GROUPNORM_EPS) * prow(PV_GNW) + prow(PV_GNB)
        yr_ref[sl, :] = (yn + bonus[sl]) * g[sl]


def _rwkv(g_main, g_lr, v_first, w2s, pvec, has_vres):
    t = g_main.shape[0]
    c = pvec.shape[1]
    tt = 256
    npair = c // LANES
    col0 = c // LANES
    in_specs = [pl.BlockSpec((tt, LANES), lambda p, i: (i, col0 + p)),
                pl.BlockSpec((tt, LANES), lambda p, i: (i, 2 * col0 + p)),
                pl.BlockSpec((tt, LANES), lambda p, i: (i, 3 * col0 + p)),
                pl.BlockSpec((tt, 2 * LR_PAD), lambda p, i: (i, 0))]
    args = [g_main, g_main, g_main, g_lr]
    if has_vres:
        in_specs.append(pl.BlockSpec((tt, LANES), lambda p, i: (i, p)))
        args.append(v_first)
    in_specs += [pl.BlockSpec((768, LANES), lambda p, i: (0, p)),
                 pl.BlockSpec((PV_ROWS, LANES), lambda p, i: (0, p))]
    args += [w2s, pvec]
    out_spec = pl.BlockSpec((tt, LANES), lambda p, i: (i, p))
    out_shape = jax.ShapeDtypeStruct((t, c), F32)
    if has_vres:
        out_specs, out_shapes = out_spec, out_shape
    else:
        out_specs, out_shapes = [out_spec, out_spec], [out_shape, out_shape]
    return pl.pallas_call(
        functools.partial(_rwkv_kernel, tt=tt, has_vres=has_vres),
        grid=(npair, t // tt),
        in_specs=in_specs,
        out_specs=out_specs,
        out_shape=out_shapes,
        scratch_shapes=[pltpu.VMEM((LANES, LANES), F32),
                        pltpu.VMEM((SUBLANES, LANES), F32),
                        pltpu.VMEM((SUBLANES, LR_PAD), F32)],
        compiler_params=_cparams(("arbitrary", "arbitrary")),
        name="rwkv7_vres" if has_vres else "rwkv7",
    )(*args)


def _mix_kernel(*refs, tm, has_router):
    if has_router:
        (x_ref, pp_ref, halo_ref, glp_ref, glr_ref, yr_ref, pw_ref, ps_ref, po_ref, ro_ref, mo_ref,
         g1_ref, sc2_ref, sh2_ref, rt_ref, xo_ref, h2_ref, idx_ref, gt_ref, ext) = refs
    else:
        (x_ref, pp_ref, halo_ref, glp_ref, glr_ref, yr_ref, pw_ref, ps_ref, po_ref, ro_ref, mo_ref,
         g1_ref, sc2_ref, sh2_ref, xo_ref, h2_ref, ext) = refs
    i = pl.program_id(0)
    ext[0:POOL_HALO, :] = jnp.where(i == 0, 0.0, halo_ref[...])
    ext[POOL_HALO:POOL_HALO + tm, :] = pp_ref[...]
    pos = i * tm + lax.broadcasted_iota(jnp.int32, (tm, 1), 0)
    gw = ext.shape[1] // len(POOL_WINDOWS)
    parts = []
    for gi, win in enumerate(POOL_WINDOWS):
        cs = slice(gi * gw, (gi + 1) * gw)
        cur = ext[POOL_HALO:POOL_HALO + tm, cs]
        s = cur
        for j in range(1, win):
            s = s + ext[POOL_HALO - j:POOL_HALO - j + tm, cs]
        cnt = jnp.minimum(pos + 1, win).astype(F32)
        pooled = s / cnt - cur
        parts.append(_mm(pooled, pw_ref[gi]))
    mixed = jnp.concatenate(parts, axis=-1) * ps_ref[...]
    y_pool = _mm(mixed, po_ref[...])
    y_rwkv = _mm(yr_ref[...], ro_ref[...])
    merged = jax.nn.sigmoid(glp_ref[...]) * y_pool + jax.nn.sigmoid(glr_ref[...]) * y_rwkv
    xn = x_ref[...] + g1_ref[...] * _mm(merged, mo_ref[...])
    xo_ref[...] = xn
    ms = jnp.mean(xn * xn, axis=-1, keepdims=True)
    h2 = xn * lax.rsqrt(ms + NORM_EPS) * (1.0 + sc2_ref[...]) + sh2_ref[...]
    h2_ref[...] = h2.astype(h2_ref.dtype)
    if has_router:
        rt = rt_ref[...]
        hi, lo = _split(h2)
        rhi, rlo = _split(rt)
        logits = (jnp.dot(hi, rhi, preferred_element_type=F32) + jnp.dot(lo, rhi, preferred_element_type=F32)
                  + jnp.dot(hi, rlo, preferred_element_type=F32))
        n_exp = 8
        lane = lax.broadcasted_iota(jnp.int32, logits.shape, 1)
        lane_f = lane.astype(F32)
        big = float(LANES)
        lg = jnp.where(lane < n_exp, logits, -jnp.inf)
        m1 = jnp.max(lg, axis=-1, keepdims=True)
        i1 = jnp.min(jnp.where(lg == m1, lane_f, big), axis=-1, keepdims=True)
        lg2 = jnp.where(lane_f == i1, -jnp.inf, lg)
        m2 = jnp.max(lg2, axis=-1, keepdims=True)
        i2 = jnp.min(jnp.where(lg2 == m2, lane_f, big), axis=-1, keepdims=True)
        e2 = jnp.exp(m2 - m1)
        p1 = 1.0 / (1.0 + e2)
        p2 = e2 / (1.0 + e2)
        idx_ref[...] = jnp.where(lane == 0, i1, jnp.where(lane == 1, i2, 0.0)).astype(jnp.int32)
        gt_ref[...] = jnp.where(lane == 0, p1, jnp.where(lane == 1, p2, 0.0))


def _mix(x, g_main, yr, pool_w, pool_scale, pool_out, rk_out, mix_out, gate1, scale2, shift2, router_pad):
    t, d = x.shape
    c = yr.shape[1]
    tm = 256
    has_router = router_pad is not None
    halo_blocks = tm // POOL_HALO
    cb_glp = (4 * c) // d
    in_specs = [pl.BlockSpec((tm, d), lambda i: (i, 0)),
                pl.BlockSpec((tm, c), lambda i: (i, 0)),
                pl.BlockSpec((POOL_HALO, c), lambda i: (jnp.maximum(i * halo_blocks - 1, 0), 0)),
                pl.BlockSpec((tm, d), lambda i: (i, cb_glp)),
                pl.BlockSpec((tm, d), lambda i: (i, cb_glp + 1)),
                pl.BlockSpec((tm, c), lambda i: (i, 0)),
                pl.BlockSpec(pool_w.shape, lambda i: (0, 0, 0)),
                pl.BlockSpec((1, c), lambda i: (0, 0)),
                pl.BlockSpec((c, d), lambda i: (0, 0)),
                pl.BlockSpec((c, d), lambda i: (0, 0)),
                pl.BlockSpec((d, d), lambda i: (0, 0)),
                pl.BlockSpec((1, d), lambda i: (0, 0)),
                pl.BlockSpec((1, d), lambda i: (0, 0)),
                pl.BlockSpec((1, d), lambda i: (0, 0))]
    args = [x, g_main, g_main, g_main, g_main, yr, pool_w, pool_scale, pool_out, rk_out, mix_out,
            gate1, scale2, shift2]
    out_specs = [pl.BlockSpec((tm, d), lambda i: (i, 0)), pl.BlockSpec((tm, d), lambda i: (i, 0))]
    out_shape = [jax.ShapeDtypeStruct((t, d), F32),
                 jax.ShapeDtypeStruct((t, d), F32 if has_router else BF16)]
    if has_router:
        in_specs.append(pl.BlockSpec((d, LANES), lambda i: (0, 0)))
        args.append(router_pad)
        out_specs += [pl.BlockSpec((tm, LANES), lambda i: (i, 0)), pl.BlockSpec((tm, LANES), lambda i: (i, 0))]
        out_shape += [jax.ShapeDtypeStruct((t, LANES), jnp.int32), jax.ShapeDtypeStruct((t, LANES), F32)]
    return pl.pallas_call(
        functools.partial(_mix_kernel, tm=tm, has_router=has_router),
        grid=(t // tm,),
        in_specs=in_specs,
        out_specs=out_specs,
        out_shape=out_shape,
        scratch_shapes=[pltpu.VMEM((POOL_HALO + tm, c), F32)],
        compiler_params=_cparams(("arbitrary",)),
        name="mix_router" if has_router else "mix",
    )(*args)


def _ffn_kernel(h_ref, x_ref, g_ref, w1_ref, w3_ref, w2_ref, o_ref, acc):
    f = pl.program_id(1)

    @pl.when(f == 0)
    def _():
        acc[...] = jnp.zeros_like(acc)

    h = h_ref[...]
    a = jnp.dot(h, w1_ref[...].astype(BF16), preferred_element_type=F32)
    b = jnp.dot(h, w3_ref[...].astype(BF16), preferred_element_type=F32)
    u = (a * jax.nn.sigmoid(a) * b).astype(BF16)
    acc[...] += jnp.dot(u, w2_ref[...].astype(BF16), preferred_element_type=F32)

    @pl.when(f == pl.num_programs(1) - 1)
    def _():
        o_ref[...] = x_ref[...] + g_ref[...] * acc[...]


def _ffn(h2, x, gate2, w1, w3, w2):
    t, d = x.shape
    ff = w1.shape[1]
    tm, tf = 512, 512
    return pl.pallas_call(
        _ffn_kernel,
        grid=(t // tm, ff // tf),
        in_specs=[pl.BlockSpec((tm, d), lambda i, f: (i, 0)),
                  pl.BlockSpec((tm, d), lambda i, f: (i, 0)),
                  pl.BlockSpec((1, d), lambda i, f: (0, 0)),
                  pl.BlockSpec((d, tf), lambda i, f: (0, f)),
                  pl.BlockSpec((d, tf), lambda i, f: (0, f)),
                  pl.BlockSpec((tf, d), lambda i, f: (f, 0))],
        out_specs=pl.BlockSpec((tm, d), lambda i, f: (i, 0)),
        out_shape=jax.ShapeDtypeStruct((t, d), F32),
        scratch_shapes=[pltpu.VMEM((tm, d), F32)],
        compiler_params=_cparams(("arbitrary", "arbitrary")),
        name="ffn_dense",
    )(h2, x, gate2, w1, w3, w2)


def _slab_copy(src_hbm, src_row, dst, slot, dst_row, slab, sem):
    src = src_hbm.at[pl.ds(pl.multiple_of(src_row * slab, slab), slab), :]
    return pltpu.make_async_copy(src, dst.at[slot, pl.ds(pl.multiple_of(dst_row * slab, slab), slab), :],
                                 sem.at[slot])


def _moe_kernel(te_ref, na_ref, tok_ref, h_hbm, w1_ref, w3_ref, w2_ref, o_ref, xg, xb, acc, sem, *, tm):
    b = pl.program_id(0)
    f = pl.program_id(1)
    nf = pl.num_programs(1)
    n_act = na_ref[0]
    slot = b % 2
    slab = xb.shape[1] // LANES

    def start_gather(tile, slt):
        def body(s, carry):
            _slab_copy(h_hbm, tok_ref[tile * tm + s], xg, slt, s, slab, sem).start()
            return carry
        lax.fori_loop(0, tm, body, 0)

    @pl.when(jnp.logical_and(f == 0, b < n_act))
    def _():
        @pl.when(b == 0)
        def _():
            start_gather(0, 0)

        @pl.when(b + 1 < n_act)
        def _():
            start_gather(b + 1, 1 - slot)

        pltpu.make_async_copy(h_hbm.at[pl.ds(0, tm * slab), :], xg.at[slot], sem.at[slot]).wait()
        for a in range(slab):
            xb[:, a * LANES:(a + 1) * LANES] = xg[slot, pl.ds(a, tm, stride=slab), :].astype(BF16)
        acc[...] = jnp.zeros_like(acc)

    @pl.when(b < n_act)
    def _():
        x = xb[...]
        a_ = jnp.dot(x, w1_ref[0].astype(BF16), preferred_element_type=F32)
        b_ = jnp.dot(x, w3_ref[0].astype(BF16), preferred_element_type=F32)
        u = (a_ * jax.nn.sigmoid(a_) * b_).astype(BF16)
        acc[...] += jnp.dot(u, w2_ref[0].astype(BF16), preferred_element_type=F32)

    @pl.when(f == nf - 1)
    def _():
        o_ref[...] = jnp.where(b < n_act, acc[...], 0.0)


def _moe(h2_slab, tile_e, n_active, slot_tok, w1, w3, w2, tm, n_tiles):
    n_exp, d, ff = w1.shape
    tf = 512
    nf = ff // tf

    def w13_map(b, f, te, na, tok):
        return (te[b], 0, jnp.where(b < na[0], f, nf - 1))

    def w2_map(b, f, te, na, tok):
        return (te[b], jnp.where(b < na[0], f, nf - 1), 0)

    grid_spec = pltpu.PrefetchScalarGridSpec(
        num_scalar_prefetch=3,
        grid=(n_tiles, nf),
        in_specs=[pl.BlockSpec(memory_space=pl.ANY),
                  pl.BlockSpec((1, d, tf), w13_map),
                  pl.BlockSpec((1, d, tf), w13_map),
                  pl.BlockSpec((1, tf, d), w2_map)],
        out_specs=pl.BlockSpec((tm, d), lambda b, f, te, na, tok: (b, 0)),
        scratch_shapes=[pltpu.VMEM((2, tm * (d // LANES), LANES), F32),
                        pltpu.VMEM((tm, d), BF16),
                        pltpu.VMEM((tm, d), F32),
                        pltpu.SemaphoreType.DMA((2,))],
    )
    return pl.pallas_call(
        functools.partial(_moe_kernel, tm=tm),
        grid_spec=grid_spec,
        out_shape=jax.ShapeDtypeStruct((n_tiles * tm, d), F32),
        compiler_params=_cparams(("arbitrary", "arbitrary")),
        name="moe_experts",
    )(tile_e, n_active, slot_tok, h2_slab, w1, w3, w2)


def _comb_kernel(dest_ref, yb_hbm, x_ref, gt_ref, g2_ref, fg_ref, o_ref, buf, sem, *, tm):
    i = pl.program_id(0)
    n = pl.num_programs(0)
    slot = i % 2
    nrow = TOP_K * tm
    slab = o_ref.shape[1] // LANES

    def start_gather(tile, slt):
        def body(s, carry):
            _slab_copy(yb_hbm, dest_ref[tile * nrow + s], buf, slt, s, slab, sem).start()
            return carry
        lax.fori_loop(0, nrow, body, 0)

    @pl.when(i == 0)
    def _():
        start_gather(0, 0)

    @pl.when(i + 1 < n)
    def _():
        start_gather(i + 1, 1 - slot)

    pltpu.make_async_copy(yb_hbm.at[pl.ds(0, nrow * slab), :], buf.at[slot], sem.at[slot]).wait()
    gt = gt_ref[...]
    p0 = gt[:, 0:1]
    p1 = gt[:, 1:2]
    cols = []
    for a in range(slab):
        y0 = buf[slot, pl.ds(a, tm, stride=TOP_K * slab), :]
        y1 = buf[slot, pl.ds(slab + a, tm, stride=TOP_K * slab), :]
        cols.append(y0 * p0 + y1 * p1)
    fmoe = jnp.concatenate(cols, axis=-1)
    xn = x_ref[...] + g2_ref[...] * fmoe
    ms = jnp.mean(xn * xn, axis=-1, keepdims=True)
    o_ref[...] = xn * lax.rsqrt(ms + NORM_EPS) * fg_ref[...]


def _combine(dest, yb_slab, x, gates, gate2, final_gain):
    t, d = x.shape
    tm = 256
    grid_spec = pltpu.PrefetchScalarGridSpec(
        num_scalar_prefetch=1,
        grid=(t // tm,),
        in_specs=[pl.BlockSpec(memory_space=pl.ANY),
                  pl.BlockSpec((tm, d), lambda i, dr: (i, 0)),
                  pl.BlockSpec((tm, LANES), lambda i, dr: (i, 0)),
                  pl.BlockSpec((1, d), lambda i, dr: (0, 0)),
                  pl.BlockSpec((1, d), lambda i, dr: (0, 0))],
        out_specs=pl.BlockSpec((tm, d), lambda i, dr: (i, 0)),
        scratch_shapes=[pltpu.VMEM((2, TOP_K * tm * (d // LANES), LANES), F32),
                        pltpu.SemaphoreType.DMA((2,))],
    )
    return pl.pallas_call(
        functools.partial(_comb_kernel, tm=tm),
        grid_spec=grid_spec,
        out_shape=jax.ShapeDtypeStruct((t, d), F32),
        compiler_params=_cparams(("arbitrary",)),
        name="moe_combine",
    )(dest, yb_slab, x, gates, gate2, final_gain)


def _pad_rows(w, rows):
    return jnp.pad(w, ((0, rows - w.shape[0]), (0, 0)))


def _lowrank_weights(l, rk_lerp, rk_w1, rk_a1, rk_g1, rk_vmu, rk_v1):
    d = rk_w1.shape[1]
    mus = [rk_lerp[l, 0], rk_lerp[l, 1], rk_lerp[l, 2]]
    ws = [rk_w1[l], rk_a1[l], rk_g1[l]]
    if l > 0:
        mus.append(rk_vmu[l - 1])
        ws.append(rk_v1[l - 1])
    wa = jnp.concatenate([(1.0 - m)[:, None] * w for m, w in zip(mus, ws)], axis=1)
    wb = jnp.concatenate([m[:, None] * w for m, w in zip(mus, ws)], axis=1)
    wa = jnp.pad(wa, ((0, 0), (0, LR_PAD - wa.shape[1])))
    wb = jnp.pad(wb, ((0, 0), (0, LR_PAD - wb.shape[1])))
    return jnp.concatenate([wa, wb], axis=1).astype(BF16)


def _second_stage_weights(l, rk_w2, rk_a2, rk_g2, rk_v2):
    c = rk_w2.shape[2]
    w2 = _pad_rows(rk_w2[l], 128)
    a2 = jnp.concatenate([jnp.zeros((64, c), F32), rk_a2[l]], axis=0)
    g2 = _pad_rows(rk_g2[l], 256)
    if l > 0:
        v2 = jnp.concatenate([jnp.zeros((160, c), F32), rk_v2[l - 1], jnp.zeros((64, c), F32)], axis=0)
    else:
        v2 = jnp.zeros((256, c), F32)
    return jnp.concatenate([w2, a2, g2, v2], axis=0).astype(BF16)


def _route(idx, tm, n_exp, n_tiles):
    n_assign = idx.size
    flat_e = idx.reshape(-1)
    onehot = (flat_e[:, None] == jnp.arange(n_exp, dtype=jnp.int32)[None, :]).astype(jnp.int32)
    csum = jnp.cumsum(onehot, axis=0)
    rank = jnp.take_along_axis(csum, flat_e[:, None], axis=1)[:, 0] - 1
    counts = csum[-1]
    padded = (counts + tm - 1) // tm * tm
    pad_end = jnp.cumsum(padded)
    pad_start = pad_end - padded
    dest = (pad_start[flat_e] + rank).astype(jnp.int32)
    slot_tok = jnp.zeros((n_tiles * tm,), jnp.int32).at[dest].set(
        jnp.arange(n_assign, dtype=jnp.int32) // TOP_K)
    n_active = (pad_end[-1] // tm).astype(jnp.int32)
    tile_ids = jnp.minimum(jnp.arange(n_tiles, dtype=jnp.int32), jnp.maximum(n_active - 1, 0))
    tile_e = jnp.minimum(jnp.searchsorted(pad_end, tile_ids * tm, side='right'), n_exp - 1).astype(jnp.int32)
    return dest, slot_tok, tile_e, n_active.reshape(1)


def kernel(x, c, ada_w, ada_b, w_in, pool_w, pool_scale, pool_out, rk_shift, rk_lerp, rk_w0, rk_w1, rk_w2, rk_a0, rk_a1, rk_a2, rk_g1, rk_g2, rk_kk, rk_ka, rk_rk, rk_gn_w, rk_gn_b, rk_vmu, rk_v0, rk_v1, rk_v2, rk_out, mix_out, ffn_w1, ffn_w3, ffn_w2, router, moe_w1, moe_w3, moe_w2, final_gain):
    bsz, t, d = x.shape
    depth = ada_w.shape[0]
    cw = rk_w0.shape[1]
    n_exp = router.shape[2]
    assert bsz == 1 and depth == 2 and cw % LANES == 0
    xs = x.reshape(t, d)
    mod = _ada(c.reshape(d, 1), ada_w, ada_b)

    def mod_row(l, i):
        return mod[l, :, i * d:(i + 1) * d]

    v_first = None
    out = None
    for l in range(depth):
        w_lr = _lowrank_weights(l, rk_lerp, rk_w1, rk_a1, rk_g1, rk_vmu, rk_v1)
        g_main, g_lr = _in_proj(xs, mod_row(l, 1), mod_row(l, 0), w_in[l], w_lr)
        w2s = _second_stage_weights(l, rk_w2, rk_a2, rk_g2, rk_v2)
        zeros_c = jnp.zeros((cw,), F32)
        rows = [rk_w0[l], rk_a0[l], rk_v0[l - 1] if l > 0 else zeros_c, rk_kk[l], rk_ka[l],
                rk_gn_w[l], rk_gn_b[l], rk_shift[l, 0], rk_shift[l, 1], rk_shift[l, 2],
                rk_rk[l].reshape(cw)]
        pvec = _pad_rows(jnp.stack(rows, axis=0), PV_ROWS)
        if l == 0:
            yr, v_first = _rwkv(g_main, g_lr, None, w2s, pvec, has_vres=False)
        else:
            yr = _rwkv(g_main, g_lr, v_first, w2s, pvec, has_vres=True)
        is_moe = (l % 2 == 1)
        router_pad = None
        if is_moe:
            router_pad = jnp.pad(router[l // 2], ((0, 0), (0, LANES - n_exp)))
        res = _mix(xs, g_main, yr, pool_w[l].astype(BF16), pool_scale[l].reshape(1, -1),
                   pool_out[l].astype(BF16), rk_out[l].astype(BF16), mix_out[l].astype(BF16),
                   mod_row(l, 2), mod_row(l, 4), mod_row(l, 3), router_pad)
        if not is_moe:
            x_new, h2 = res
            xs = _ffn(h2, x_new, mod_row(l, 5), ffn_w1[l // 2].astype(BF16), ffn_w3[l // 2].astype(BF16),
                      ffn_w2[l // 2].astype(BF16))
        else:
            x_new, h2, idx_pad, gate_pad = res
            tm_moe = 256
            n_tiles = (t * TOP_K) // tm_moe + n_exp
            dest, slot_tok, tile_e, n_active = _route(idx_pad[:, :TOP_K], tm_moe, n_exp, n_tiles)
            h2_slab = h2.reshape(t * (d // LANES), LANES)
            yb = _moe(h2_slab, tile_e, n_active, slot_tok, moe_w1[l // 2], moe_w3[l // 2], moe_w2[l // 2],
                      tm_moe, n_tiles)
            yb_slab = yb.reshape(n_tiles * tm_moe * (d // LANES), LANES)
            out = _combine(dest, yb_slab, x_new, gate_pad, mod_row(l, 5), final_gain.reshape(1, d))
    return out.reshape(bsz, t, d)
```

```python
import functools

import jax
import jax.numpy as jnp
from jax import lax
from jax.experimental import pallas as pl
from jax.experimental.pallas import tpu as pltpu

F32 = jnp.float32
BF16 = jnp.bfloat16

NORM_EPS = 1e-6
GROUPNORM_EPS = 64e-5
L2_EPS = 1e-12
POOL_WINDOWS = (2, 4, 8, 16)
HEAD_DIM = 64
TOP_K = 2
LANES = 128
SUBLANES = 8
CHUNK = 64
LR_PAD = 384
POOL_HALO = 16
VMEM_LIMIT = 56 * 1024 * 1024


def _cparams(sem):
    return pltpu.CompilerParams(dimension_semantics=sem, vmem_limit_bytes=VMEM_LIMIT)


def _mm(a, b):
    return jnp.dot(a.astype(BF16), b.astype(BF16), preferred_element_type=F32)


def _mm_nt(a, b):
    return lax.dot_general(a.astype(BF16), b.astype(BF16), (((1,), (1,)), ((), ())),
                           preferred_element_type=F32)


def _split(x):
    hi = x.astype(BF16)
    lo = (x - hi.astype(F32)).astype(BF16)
    return hi, lo


def _mm_exact_rhs(x, m):
    hi, lo = _split(x)
    return (jnp.dot(hi, m, preferred_element_type=F32) + jnp.dot(lo, m, preferred_element_type=F32))


def _ada_kernel(c_ref, w_ref, b_ref, o_ref):
    c = c_ref[...]
    act = c * jax.nn.sigmoid(c)
    o_ref[0] = jnp.sum(act * w_ref[0], axis=0, keepdims=True) + b_ref[0]


def _ada(c_col, ada_w, ada_b):
    depth, d, n = ada_w.shape
    tn = 1024
    return pl.pallas_call(
        _ada_kernel,
        grid=(depth, n // tn),
        in_specs=[pl.BlockSpec((d, 1), lambda l, j: (0, 0)),
                  pl.BlockSpec((1, d, tn), lambda l, j: (l, 0, j)),
                  pl.BlockSpec((1, 1, tn), lambda l, j: (l, 0, j))],
        out_specs=pl.BlockSpec((1, 1, tn), lambda l, j: (l, 0, j)),
        out_shape=jax.ShapeDtypeStruct((depth, 1, n), F32),
        compiler_params=_cparams(("arbitrary", "arbitrary")),
        name="ada_gemv",
    )(c_col, ada_w, ada_b.reshape(depth, 1, n))


def _in_kernel(x_ref, sc_ref, sh_ref, w_ref, wlr_ref, o_ref, olr_ref, h_sc):
    @pl.when(pl.program_id(1) == 0)
    def _():
        x = x_ref[...]
        ms = jnp.mean(x * x, axis=-1, keepdims=True)
        h = x * lax.rsqrt(ms + NORM_EPS) * (1.0 + sc_ref[...]) + sh_ref[...]
        hb = h.astype(BF16)
        h_sc[...] = hb
        olr_ref[...] = jnp.dot(hb, wlr_ref[...], preferred_element_type=F32)

    o_ref[...] = jnp.dot(h_sc[...], w_ref[0].astype(BF16), preferred_element_type=F32)


def _in_proj(x, scale, shift, w_in, layer, w_lr):
    t, d = x.shape
    n = w_in.shape[2]
    nlr = w_lr.shape[1]
    tm, tn = 1024, 512
    return pl.pallas_call(
        _in_kernel,
        grid=(t // tm, n // tn),
        in_specs=[pl.BlockSpec((tm, d), lambda i, j: (i, 0)),
                  pl.BlockSpec((1, d), lambda i, j: (0, 0)),
                  pl.BlockSpec((1, d), lambda i, j: (0, 0)),
                  pl.BlockSpec((1, d, tn), lambda i, j: (layer, 0, j)),
                  pl.BlockSpec((d, nlr), lambda i, j: (0, 0))],
        out_specs=[pl.BlockSpec((tm, tn), lambda i, j: (i, j)),
                   pl.BlockSpec((tm, nlr), lambda i, j: (i, 0))],
        out_shape=[jax.ShapeDtypeStruct((t, n), F32), jax.ShapeDtypeStruct((t, nlr), F32)],
        scratch_shapes=[pltpu.VMEM((tm, d), BF16)],
        compiler_params=_cparams(("arbitrary", "arbitrary")),
        name="in_proj",
    )(x, scale, shift, w_in, w_lr)


PV_W0, PV_A0, PV_V0, PV_KK, PV_KA, PV_GNW, PV_GNB, PV_MUR, PV_MUK, PV_MUV, PV_RK = range(11)
PV_ROWS = 16


def _rwkv_kernel(*refs, tt, has_vres):
    if has_vres:
        (r_ref, k_ref, v_ref, lr_ref, vf_ref, w2_ref, pv_ref, yr_ref, s_sc, c_rkv, c_lr) = refs
    else:
        (r_ref, k_ref, v_ref, lr_ref, w2_ref, pv_ref, yr_ref, vfo_ref, s_sc, c_rkv, c_lr) = refs

    @pl.when(pl.program_id(1) == 0)
    def _():
        s_sc[...] = jnp.zeros_like(s_sc)
        c_rkv[...] = jnp.zeros_like(c_rkv)
        c_lr[...] = jnp.zeros_like(c_lr)

    row = lax.broadcasted_iota(jnp.int32, (tt, 1), 0)

    def shift(x, carry_row):
        return jnp.where(row == 0, carry_row, pltpu.roll(x, 1, axis=0))

    pv = pv_ref[...]

    def prow(i):
        return pv[i:i + 1, :]

    r_raw, k_raw, v_raw = r_ref[...], k_ref[...], v_ref[...]
    zb = lr_ref[:, LR_PAD:2 * LR_PAD]
    r_prev = shift(r_raw, c_rkv[0:1, :])
    k_prev = shift(k_raw, c_rkv[1:2, :])
    v_prev = shift(v_raw, c_rkv[2:3, :])
    z = lr_ref[:, 0:LR_PAD] + shift(zb, c_lr[0:1, :])
    c_rkv[0:1, :] = r_raw[tt - 1:tt, :]
    c_rkv[1:2, :] = k_raw[tt - 1:tt, :]
    c_rkv[2:3, :] = v_raw[tt - 1:tt, :]
    c_lr[0:1, :] = zb[tt - 1:tt, :]

    r = r_raw + (r_prev - r_raw) * prow(PV_MUR)
    k = k_raw + (k_prev - k_raw) * prow(PV_MUK)
    v = v_raw + (v_prev - v_raw) * prow(PV_MUV)

    z_wa = z[:, 0:LANES]
    z_gv = z[:, LANES:LR_PAD]
    w_pre = prow(PV_W0) + _mm(jnp.tanh(z_wa), w2_ref[0:128, :])
    neg = -w_pre
    softplus = jnp.maximum(neg, 0.0) + jnp.log(1.0 + jnp.exp(-jnp.abs(neg)))
    logd = -jnp.exp(-softplus - 0.5)
    a_i = jax.nn.sigmoid(prow(PV_A0) + _mm(z_wa, w2_ref[128:256, :]))
    g = _mm(jax.nn.sigmoid(z_gv), w2_ref[256:512, :])
    if has_vres:
        v = v + (vf_ref[...] - v) * jax.nn.sigmoid(prow(PV_V0) + _mm(z_gv, w2_ref[512:768, :]))
    else:
        vfo_ref[...] = v

    li = lax.broadcasted_iota(jnp.int32, (LANES, LANES), 0)
    lj = lax.broadcasted_iota(jnp.int32, (LANES, LANES), 1)
    same_head = ((li >> 6) == (lj >> 6))
    head_ones = jnp.where(same_head, 1.0, 0.0).astype(BF16)

    def head_sum(x):
        return _mm_exact_rhs(x, head_ones)

    kk = k * prow(PV_KK)
    kk = kk / jnp.maximum(jnp.sqrt(head_sum(kk * kk)), L2_EPS)
    k2 = k * (1.0 + (a_i - 1.0) * prow(PV_KA))
    a_vec = -kk
    b_vec = kk * a_i
    bonus = head_sum(r * k2 * prow(PV_RK)) * v

    ci = lax.broadcasted_iota(jnp.int32, (CHUNK, CHUNK), 0)
    cj = lax.broadcasted_iota(jnp.int32, (CHUNK, CHUNK), 1)
    tri_incl = jnp.where(ci >= cj, 1.0, 0.0).astype(BF16)
    strict = (li & (CHUNK - 1)) > (lj & (CHUNK - 1))
    incl = (li & (CHUNK - 1)) >= (lj & (CHUNK - 1))
    eye = li == lj
    eye_f = jnp.where(eye, 1.0, 0.0)
    off_masks = [((li >> (s + 1)) == (lj >> (s + 1))) & (((li >> s) & 1) == 1) & (((lj >> s) & 1) == 0)
                 for s in range(CHUNK.bit_length() - 1)]
    head0 = lax.broadcasted_iota(jnp.int32, (CHUNK, LANES), 1) < HEAD_DIM

    def stack(x):
        return jnp.concatenate([jnp.where(head0, x, 0.0), jnp.where(head0, 0.0, x)], axis=0)

    chunks = range(tt // CHUNK)
    ops = []
    for c in chunks:
        sl = slice(c * CHUNK, (c + 1) * CHUNK)
        ld = logd[sl]
        ld_hi, ld_lo = _split(ld)
        lp = (jnp.dot(tri_incl, ld_hi, preferred_element_type=F32)
              + jnp.dot(tri_incl, ld_lo, preferred_element_type=F32))
        le = lp[CHUNK - 1:CHUNK, :]
        e_p = jnp.exp(lp)
        e_n = jnp.exp(-lp)
        e_e = jnp.exp(le - lp)
        ops.append(dict(
            rh=stack(r[sl] * e_p), ah=stack(a_vec[sl] * jnp.exp(lp - ld)),
            bt=stack(b_vec[sl] * e_n).astype(BF16), kt=stack(k2[sl] * e_n).astype(BF16),
            bb_t=stack(b_vec[sl] * e_e).T.astype(BF16), kb_t=stack(k2[sl] * e_e).T.astype(BF16),
            vv=stack(v[sl]).astype(BF16), diag=jnp.where(eye, jnp.exp(le), 0.0)))
    for o in ops:
        ar = jnp.concatenate([o['ah'], o['rh']], axis=0).astype(BF16)
        gb = _mm_nt(ar, o['bt'])
        gk = _mm_nt(ar, o['kt'])
        o['a_ab'] = jnp.where(strict, gb[0:LANES], 0.0)
        o['a_rb'] = jnp.where(incl, gb[LANES:], 0.0).astype(BF16)
        o['a_ak'] = jnp.where(strict, gk[0:LANES], 0.0).astype(BF16)
        o['a_rk'] = jnp.where(incl, gk[LANES:], 0.0).astype(BF16)
    for o in ops:
        o['x'] = _mm(o['a_ak'], o['vv'])
        o['yv'] = _mm(o['a_rk'], o['vv'])
        o['cv'] = _mm(o['kb_t'], o['vv'])
    for o in ops:
        o['t'] = eye_f + jnp.where(off_masks[0], o['a_ab'], 0.0)
    for off in off_masks[1:]:
        for o in ops:
            o['ta'] = _mm(o['t'], jnp.where(off, o['a_ab'], 0.0))
        for o in ops:
            o['t'] = o['t'] + _mm(o['ta'], o['t'])
    for o in ops:
        o['w'] = _mm(o['t'], jnp.concatenate([o['ah'], o['x']], axis=1)).astype(BF16)
    for o in ops:
        qy = _mm(o['a_rb'], o['w'])
        mc = _mm(o['bb_t'], o['w'])
        o['qm'] = jnp.concatenate([o['rh'] + qy[:, 0:LANES], o['diag'] + mc[:, 0:LANES]], axis=0).astype(BF16)
        o['y_in'] = qy[:, LANES:] + o['yv']
        o['c'] = mc[:, LANES:] + o['cv']
    s_cur = s_sc[...]
    ys = []
    for o in ops:
        res = _mm(o['qm'], s_cur)
        y2 = res[0:LANES] + o['y_in']
        ys.append(y2[0:CHUNK] + y2[CHUNK:2 * CHUNK])
        s_cur = res[LANES:] + o['c']
    s_sc[...] = s_cur
    y = jnp.concatenate(ys, axis=0)
    mean = head_sum(y) * (1.0 / HEAD_DIM)
    dlt = y - mean
    var = head_sum(dlt * dlt) * (1.0 / HEAD_DIM)
    yn = dlt * lax.rsqrt(var + GROUPNORM_EPS) * prow(PV_GNW) + prow(PV_GNB)
    yr_ref[...] = (yn + bonus) * g


def _rwkv(g_main, g_lr, v_first, w2s, pvec, has_vres):
    t = g_main.shape[0]
    c = pvec.shape[1]
    tt = 512
    npair = c // LANES
    col0 = c // LANES
    in_specs = [pl.BlockSpec((tt, LANES), lambda p, i: (i, col0 + p)),
                pl.BlockSpec((tt, LANES), lambda p, i: (i, 2 * col0 + p)),
                pl.BlockSpec((tt, LANES), lambda p, i: (i, 3 * col0 + p)),
                pl.BlockSpec((tt, 2 * LR_PAD), lambda p, i: (i, 0))]
    args = [g_main, g_main, g_main, g_lr]
    if has_vres:
        in_specs.append(pl.BlockSpec((tt, LANES), lambda p, i: (i, p)))
        args.append(v_first)
    in_specs += [pl.BlockSpec((768, LANES), lambda p, i: (0, p)),
                 pl.BlockSpec((PV_ROWS, LANES), lambda p, i: (0, p))]
    args += [w2s, pvec]
    out_spec = pl.BlockSpec((tt, LANES), lambda p, i: (i, p))
    out_shape = jax.ShapeDtypeStruct((t, c), F32)
    if has_vres:
        out_specs, out_shapes = out_spec, out_shape
    else:
        out_specs, out_shapes = [out_spec, out_spec], [out_shape, out_shape]
    return pl.pallas_call(
        functools.partial(_rwkv_kernel, tt=tt, has_vres=has_vres),
        grid=(npair, t // tt),
        in_specs=in_specs,
        out_specs=out_specs,
        out_shape=out_shapes,
        scratch_shapes=[pltpu.VMEM((LANES, LANES), F32),
                        pltpu.VMEM((SUBLANES, LANES), F32),
                        pltpu.VMEM((SUBLANES, LR_PAD), F32)],
        compiler_params=_cparams(("arbitrary", "arbitrary")),
        name="rwkv7_vres" if has_vres else "rwkv7",
    )(*args)


def _mix_kernel(*refs, tm, has_router):
    if has_router:
        (x_ref, pp_ref, halo_ref, glp_ref, glr_ref, yr_ref, pw_ref, ps_ref, po_ref, ro_ref, mo_ref,
         g1_ref, sc2_ref, sh2_ref, rt_ref, xo_ref, h2_ref, idx_ref, gt_ref, ext) = refs
    else:
        (x_ref, pp_ref, halo_ref, glp_ref, glr_ref, yr_ref, pw_ref, ps_ref, po_ref, ro_ref, mo_ref,
         g1_ref, sc2_ref, sh2_ref, xo_ref, h2_ref, ext) = refs
    i = pl.program_id(0)
    ext[0:POOL_HALO, :] = jnp.where(i == 0, 0.0, halo_ref[...])
    ext[POOL_HALO:POOL_HALO + tm, :] = pp_ref[...]
    pos = i * tm + lax.broadcasted_iota(jnp.int32, (tm, 1), 0)
    gw = ext.shape[1] // len(POOL_WINDOWS)
    parts = []
    for gi, win in enumerate(POOL_WINDOWS):
        cs = slice(gi * gw, (gi + 1) * gw)
        cur = ext[POOL_HALO:POOL_HALO + tm, cs]
        s = cur
        for j in range(1, win):
            s = s + ext[POOL_HALO - j:POOL_HALO - j + tm, cs]
        cnt = jnp.minimum(pos + 1, win).astype(F32)
        pooled = s / cnt - cur
        parts.append(_mm(pooled, pw_ref[gi]))
    mixed = jnp.concatenate(parts, axis=-1) * ps_ref[...]
    y_pool = _mm(mixed, po_ref[...])
    y_rwkv = _mm(yr_ref[...], ro_ref[...])
    merged = jax.nn.sigmoid(glp_ref[...]) * y_pool + jax.nn.sigmoid(glr_ref[...]) * y_rwkv
    xn = x_ref[...] + g1_ref[...] * _mm(merged, mo_ref[...])
    xo_ref[...] = xn
    ms = jnp.mean(xn * xn, axis=-1, keepdims=True)
    h2 = xn * lax.rsqrt(ms + NORM_EPS) * (1.0 + sc2_ref[...]) + sh2_ref[...]
    if has_router:
        slab = h2.shape[1] // LANES
        for a in range(slab):
            h2_ref[pl.ds(a, tm, stride=slab), :] = h2[:, a * LANES:(a + 1) * LANES]
        rt = rt_ref[...]
        hi, lo = _split(h2)
        rhi, rlo = _split(rt)
        logits = (jnp.dot(hi, rhi, preferred_element_type=F32) + jnp.dot(lo, rhi, preferred_element_type=F32)
                  + jnp.dot(hi, rlo, preferred_element_type=F32))
        n_exp = 8
        lane = lax.broadcasted_iota(jnp.int32, logits.shape, 1)
        lane_f = lane.astype(F32)
        big = float(LANES)
        lg = jnp.where(lane < n_exp, logits, -jnp.inf)
        m1 = jnp.max(lg, axis=-1, keepdims=True)
        i1 = jnp.min(jnp.where(lg == m1, lane_f, big), axis=-1, keepdims=True)
        lg2 = jnp.where(lane_f == i1, -jnp.inf, lg)
        m2 = jnp.max(lg2, axis=-1, keepdims=True)
        i2 = jnp.min(jnp.where(lg2 == m2, lane_f, big), axis=-1, keepdims=True)
        e2 = jnp.exp(m2 - m1)
        p1 = 1.0 / (1.0 + e2)
        p2 = e2 / (1.0 + e2)
        idx_ref[...] = jnp.where(lane == 0, i1, jnp.where(lane == 1, i2, 0.0)).astype(jnp.int32)
        gt_ref[...] = jnp.where(lane == 0, p1, jnp.where(lane == 1, p2, 0.0))
    else:
        h2_ref[...] = h2.astype(h2_ref.dtype)


def _mix(x, g_main, yr, pool_w, pool_scale, pool_out, rk_out, mix_out, gate1, scale2, shift2, router_pad):
    t, d = x.shape
    c = yr.shape[1]
    tm = 256
    has_router = router_pad is not None
    halo_blocks = tm // POOL_HALO
    cb_glp = (4 * c) // d
    in_specs = [pl.BlockSpec((tm, d), lambda i: (i, 0)),
                pl.BlockSpec((tm, c), lambda i: (i, 0)),
                pl.BlockSpec((POOL_HALO, c), lambda i: (jnp.maximum(i * halo_blocks - 1, 0), 0)),
                pl.BlockSpec((tm, d), lambda i: (i, cb_glp)),
                pl.BlockSpec((tm, d), lambda i: (i, cb_glp + 1)),
                pl.BlockSpec((tm, c), lambda i: (i, 0)),
                pl.BlockSpec(pool_w.shape, lambda i: (0, 0, 0)),
                pl.BlockSpec((1, c), lambda i: (0, 0)),
                pl.BlockSpec((c, d), lambda i: (0, 0)),
                pl.BlockSpec((c, d), lambda i: (0, 0)),
                pl.BlockSpec((d, d), lambda i: (0, 0)),
                pl.BlockSpec((1, d), lambda i: (0, 0)),
                pl.BlockSpec((1, d), lambda i: (0, 0)),
                pl.BlockSpec((1, d), lambda i: (0, 0))]
    args = [x, g_main, g_main, g_main, g_main, yr, pool_w, pool_scale, pool_out, rk_out, mix_out,
            gate1, scale2, shift2]
    out_specs = [pl.BlockSpec((tm, d), lambda i: (i, 0))]
    out_shape = [jax.ShapeDtypeStruct((t, d), F32)]
    if has_router:
        slab = d // LANES
        out_specs.append(pl.BlockSpec((tm * slab, LANES), lambda i: (i, 0)))
        out_shape.append(jax.ShapeDtypeStruct((t * slab, LANES), F32))
    else:
        out_specs.append(pl.BlockSpec((tm, d), lambda i: (i, 0)))
        out_shape.append(jax.ShapeDtypeStruct((t, d), BF16))
    if has_router:
        in_specs.append(pl.BlockSpec((d, LANES), lambda i: (0, 0)))
        args.append(router_pad)
        out_specs += [pl.BlockSpec((tm, LANES), lambda i: (i, 0)), pl.BlockSpec((tm, LANES), lambda i: (i, 0))]
        out_shape += [jax.ShapeDtypeStruct((t, LANES), jnp.int32), jax.ShapeDtypeStruct((t, LANES), F32)]
    return pl.pallas_call(
        functools.partial(_mix_kernel, tm=tm, has_router=has_router),
        grid=(t // tm,),
        in_specs=in_specs,
        out_specs=out_specs,
        out_shape=out_shape,
        scratch_shapes=[pltpu.VMEM((POOL_HALO + tm, c), F32)],
        compiler_params=_cparams(("arbitrary",)),
        name="mix_router" if has_router else "mix",
    )(*args)


def _ffn_kernel(h_ref, x_ref, g_ref, w1_ref, w3_ref, w2_ref, o_ref, acc):
    f = pl.program_id(1)

    @pl.when(f == 0)
    def _():
        acc[...] = jnp.zeros_like(acc)

    h = h_ref[...]
    a = jnp.dot(h, w1_ref[...].astype(BF16), preferred_element_type=F32)
    b = jnp.dot(h, w3_ref[...].astype(BF16), preferred_element_type=F32)
    u = (a * jax.nn.sigmoid(a) * b).astype(BF16)
    acc[...] += jnp.dot(u, w2_ref[...].astype(BF16), preferred_element_type=F32)

    @pl.when(f == pl.num_programs(1) - 1)
    def _():
        o_ref[...] = x_ref[...] + g_ref[...] * acc[...]


def _ffn(h2, x, gate2, w1, w3, w2):
    t, d = x.shape
    ff = w1.shape[1]
    tm, tf = 512, 512
    return pl.pallas_call(
        _ffn_kernel,
        grid=(t // tm, ff // tf),
        in_specs=[pl.BlockSpec((tm, d), lambda i, f: (i, 0)),
                  pl.BlockSpec((tm, d), lambda i, f: (i, 0)),
                  pl.BlockSpec((1, d), lambda i, f: (0, 0)),
                  pl.BlockSpec((d, tf), lambda i, f: (0, f)),
                  pl.BlockSpec((d, tf), lambda i, f: (0, f)),
                  pl.BlockSpec((tf, d), lambda i, f: (f, 0))],
        out_specs=pl.BlockSpec((tm, d), lambda i, f: (i, 0)),
        out_shape=jax.ShapeDtypeStruct((t, d), F32),
        scratch_shapes=[pltpu.VMEM((tm, d), F32)],
        compiler_params=_cparams(("arbitrary", "arbitrary")),
        name="ffn_dense",
    )(h2, x, gate2, w1, w3, w2)


def _slab_copy(src_hbm, src_row, dst, slot, dst_row, slab, sem):
    src = src_hbm.at[pl.ds(pl.multiple_of(src_row * slab, slab), slab), :]
    return pltpu.make_async_copy(src, dst.at[slot, pl.ds(pl.multiple_of(dst_row * slab, slab), slab), :],
                                 sem.at[slot])


def _moe_kernel(te_ref, na_ref, tok_ref, h_hbm, w1_ref, w3_ref, w2_ref, o_ref, xg, xb, acc, sem, *, tm):
    b = pl.program_id(0)
    f = pl.program_id(1)
    nf = pl.num_programs(1)
    n_act = na_ref[0]
    slot = b % 2
    slab = xb.shape[1] // LANES

    def start_gather(tile, slt):
        def body(s, carry):
            _slab_copy(h_hbm, tok_ref[tile * tm + s], xg, slt, s, slab, sem).start()
            return carry
        lax.fori_loop(0, tm, body, 0)

    @pl.when(jnp.logical_and(f == 0, b < n_act))
    def _():
        @pl.when(b == 0)
        def _():
            start_gather(0, 0)

        @pl.when(b + 1 < n_act)
        def _():
            start_gather(b + 1, 1 - slot)

        pltpu.make_async_copy(h_hbm.at[pl.ds(0, tm * slab), :], xg.at[slot], sem.at[slot]).wait()
        for a in range(slab):
            xb[:, a * LANES:(a + 1) * LANES] = xg[slot, pl.ds(a, tm, stride=slab), :].astype(BF16)
        acc[...] = jnp.zeros_like(acc)

    @pl.when(b < n_act)
    def _():
        x = xb[...]
        a_ = jnp.dot(x, w1_ref[0].astype(BF16), preferred_element_type=F32)
        b_ = jnp.dot(x, w3_ref[0].astype(BF16), preferred_element_type=F32)
        u = (a_ * jax.nn.sigmoid(a_) * b_).astype(BF16)
        acc[...] += jnp.dot(u, w2_ref[0].astype(BF16), preferred_element_type=F32)

    @pl.when(f == nf - 1)
    def _():
        res = jnp.where(b < n_act, acc[...], 0.0)
        for a in range(slab):
            o_ref[pl.ds(a, tm, stride=slab), :] = res[:, a * LANES:(a + 1) * LANES]


def _moe(h2_slab, tile_e, n_active, slot_tok, w1, w3, w2, tm, n_tiles):
    n_exp, d, ff = w1.shape
    tf = 512
    nf = ff // tf

    def w13_map(b, f, te, na, tok):
        return (te[b], 0, jnp.where(b < na[0], f, nf - 1))

    def w2_map(b, f, te, na, tok):
        return (te[b], jnp.where(b < na[0], f, nf - 1), 0)

    grid_spec = pltpu.PrefetchScalarGridSpec(
        num_scalar_prefetch=3,
        grid=(n_tiles, nf),
        in_specs=[pl.BlockSpec(memory_space=pl.ANY),
                  pl.BlockSpec((1, d, tf), w13_map),
                  pl.BlockSpec((1, d, tf), w13_map),
                  pl.BlockSpec((1, tf, d), w2_map)],
        out_specs=pl.BlockSpec((tm * (d // LANES), LANES), lambda b, f, te, na, tok: (b, 0)),
        scratch_shapes=[pltpu.VMEM((2, tm * (d // LANES), LANES), F32),
                        pltpu.VMEM((tm, d), BF16),
                        pltpu.VMEM((tm, d), F32),
                        pltpu.SemaphoreType.DMA((2,))],
    )
    return pl.pallas_call(
        functools.partial(_moe_kernel, tm=tm),
        grid_spec=grid_spec,
        out_shape=jax.ShapeDtypeStruct((n_tiles * tm * (d // LANES), LANES), F32),
        compiler_params=_cparams(("arbitrary", "arbitrary")),
        name="moe_experts",
    )(tile_e, n_active, slot_tok, h2_slab, w1, w3, w2)


def _comb_kernel(dest_ref, yb_hbm, x_ref, gt_ref, g2_ref, fg_ref, o_ref, buf, sem, *, tm):
    i = pl.program_id(0)
    n = pl.num_programs(0)
    slot = i % 2
    nrow = TOP_K * tm
    slab = o_ref.shape[1] // LANES

    def start_gather(tile, slt):
        def body(s, carry):
            _slab_copy(yb_hbm, dest_ref[tile * nrow + s], buf, slt, s, slab, sem).start()
            return carry
        lax.fori_loop(0, nrow, body, 0)

    @pl.when(i == 0)
    def _():
        start_gather(0, 0)

    @pl.when(i + 1 < n)
    def _():
        start_gather(i + 1, 1 - slot)

    pltpu.make_async_copy(yb_hbm.at[pl.ds(0, nrow * slab), :], buf.at[slot], sem.at[slot]).wait()
    gt = gt_ref[...]
    p0 = gt[:, 0:1]
    p1 = gt[:, 1:2]
    cols = []
    for a in range(slab):
        y0 = buf[slot, pl.ds(a, tm, stride=TOP_K * slab), :]
        y1 = buf[slot, pl.ds(slab + a, tm, stride=TOP_K * slab), :]
        cols.append(y0 * p0 + y1 * p1)
    fmoe = jnp.concatenate(cols, axis=-1)
    xn = x_ref[...] + g2_ref[...] * fmoe
    ms = jnp.mean(xn * xn, axis=-1, keepdims=True)
    o_ref[...] = xn * lax.rsqrt(ms + NORM_EPS) * fg_ref[...]


def _combine(dest, yb_slab, x, gates, gate2, final_gain):
    t, d = x.shape
    tm = 256
    grid_spec = pltpu.PrefetchScalarGridSpec(
        num_scalar_prefetch=1,
        grid=(t // tm,),
        in_specs=[pl.BlockSpec(memory_space=pl.ANY),
                  pl.BlockSpec((tm, d), lambda i, dr: (i, 0)),
                  pl.BlockSpec((tm, LANES), lambda i, dr: (i, 0)),
                  pl.BlockSpec((1, d), lambda i, dr: (0, 0)),
                  pl.BlockSpec((1, d), lambda i, dr: (0, 0))],
        out_specs=pl.BlockSpec((tm, d), lambda i, dr: (i, 0)),
        scratch_shapes=[pltpu.VMEM((2, TOP_K * tm * (d // LANES), LANES), F32),
                        pltpu.SemaphoreType.DMA((2,))],
    )
    return pl.pallas_call(
        functools.partial(_comb_kernel, tm=tm),
        grid_spec=grid_spec,
        out_shape=jax.ShapeDtypeStruct((t, d), F32),
        compiler_params=_cparams(("arbitrary",)),
        name="moe_combine",
    )(dest, yb_slab, x, gates, gate2, final_gain)


def _pad_rows(w, rows):
    return jnp.pad(w, ((0, rows - w.shape[0]), (0, 0)))


def _lowrank_weights(l, rk_lerp, rk_w1, rk_a1, rk_g1, rk_vmu, rk_v1):
    d = rk_w1.shape[1]
    mus = [rk_lerp[l, 0], rk_lerp[l, 1], rk_lerp[l, 2]]
    ws = [rk_w1[l], rk_a1[l], rk_g1[l]]
    if l > 0:
        mus.append(rk_vmu[l - 1])
        ws.append(rk_v1[l - 1])
    wa = jnp.concatenate([(1.0 - m)[:, None] * w for m, w in zip(mus, ws)], axis=1)
    wb = jnp.concatenate([m[:, None] * w for m, w in zip(mus, ws)], axis=1)
    wa = jnp.pad(wa, ((0, 0), (0, LR_PAD - wa.shape[1])))
    wb = jnp.pad(wb, ((0, 0), (0, LR_PAD - wb.shape[1])))
    return jnp.concatenate([wa, wb], axis=1).astype(BF16)


def _second_stage_weights(l, rk_w2, rk_a2, rk_g2, rk_v2):
    c = rk_w2.shape[2]
    w2 = _pad_rows(rk_w2[l], 128)
    a2 = jnp.concatenate([jnp.zeros((64, c), F32), rk_a2[l]], axis=0)
    g2 = _pad_rows(rk_g2[l], 256)
    if l > 0:
        v2 = jnp.concatenate([jnp.zeros((160, c), F32), rk_v2[l - 1], jnp.zeros((64, c), F32)], axis=0)
    else:
        v2 = jnp.zeros((256, c), F32)
    return jnp.concatenate([w2, a2, g2, v2], axis=0).astype(BF16)


def _route(idx, tm, n_exp, n_tiles):
    n_assign = idx.size
    flat_e = idx.reshape(-1)
    onehot = (flat_e[:, None] == jnp.arange(n_exp, dtype=jnp.int32)[None, :]).astype(jnp.int32)
    csum = jnp.cumsum(onehot, axis=0)
    rank = jnp.take_along_axis(csum, flat_e[:, None], axis=1)[:, 0] - 1
    counts = csum[-1]
    padded = (counts + tm - 1) // tm * tm
    pad_end = jnp.cumsum(padded)
    pad_start = pad_end - padded
    dest = (pad_start[flat_e] + rank).astype(jnp.int32)
    slot_tok = jnp.zeros((n_tiles * tm,), jnp.int32).at[dest].set(
        jnp.arange(n_assign, dtype=jnp.int32) // TOP_K)
    n_active = (pad_end[-1] // tm).astype(jnp.int32)
    tile_ids = jnp.minimum(jnp.arange(n_tiles, dtype=jnp.int32), jnp.maximum(n_active - 1, 0))
    tile_e = jnp.minimum(jnp.searchsorted(pad_end, tile_ids * tm, side='right'), n_exp - 1).astype(jnp.int32)
    return dest, slot_tok, tile_e, n_active.reshape(1)


def kernel(x, c, ada_w, ada_b, w_in, pool_w, pool_scale, pool_out, rk_shift, rk_lerp, rk_w0, rk_w1, rk_w2, rk_a0, rk_a1, rk_a2, rk_g1, rk_g2, rk_kk, rk_ka, rk_rk, rk_gn_w, rk_gn_b, rk_vmu, rk_v0, rk_v1, rk_v2, rk_out, mix_out, ffn_w1, ffn_w3, ffn_w2, router, moe_w1, moe_w3, moe_w2, final_gain):
    bsz, t, d = x.shape
    depth = ada_w.shape[0]
    cw = rk_w0.shape[1]
    n_exp = router.shape[2]
    assert bsz == 1 and depth == 2 and cw % LANES == 0
    xs = x.reshape(t, d)
    mod = _ada(c.reshape(d, 1), ada_w, ada_b)

    def mod_row(l, i):
        return mod[l, :, i * d:(i + 1) * d]

    v_first = None
    out = None
    for l in range(depth):
        w_lr = _lowrank_weights(l, rk_lerp, rk_w1, rk_a1, rk_g1, rk_vmu, rk_v1)
        g_main, g_lr = _in_proj(xs, mod_row(l, 1), mod_row(l, 0), w_in, l, w_lr)
        w2s = _second_stage_weights(l, rk_w2, rk_a2, rk_g2, rk_v2)
        zeros_c = jnp.zeros((cw,), F32)
        rows = [rk_w0[l], rk_a0[l], rk_v0[l - 1] if l > 0 else zeros_c, rk_kk[l], rk_ka[l],
                rk_gn_w[l], rk_gn_b[l], rk_shift[l, 0], rk_shift[l, 1], rk_shift[l, 2],
                rk_rk[l].reshape(cw)]
        pvec = _pad_rows(jnp.stack(rows, axis=0), PV_ROWS)
        if l == 0:
            yr, v_first = _rwkv(g_main, g_lr, None, w2s, pvec, has_vres=False)
        else:
            yr = _rwkv(g_main, g_lr, v_first, w2s, pvec, has_vres=True)
        is_moe = (l % 2 == 1)
        router_pad = None
        if is_moe:
            router_pad = jnp.pad(router[l // 2], ((0, 0), (0, LANES - n_exp)))
        res = _mix(xs, g_main, yr, pool_w[l].astype(BF16), pool_scale[l].reshape(1, -1),
                   pool_out[l].astype(BF16), rk_out[l].astype(BF16), mix_out[l].astype(BF16),
                   mod_row(l, 2), mod_row(l, 4), mod_row(l, 3), router_pad)
        if not is_moe:
            x_new, h2 = res
            xs = _ffn(h2, x_new, mod_row(l, 5), ffn_w1[l // 2].astype(BF16), ffn_w3[l // 2].astype(BF16),
                      ffn_w2[l // 2].astype(BF16))
        else:
            x_new, h2, idx_pad, gate_pad = res
            tm_moe = 512
            n_tiles = (t * TOP_K) // tm_moe + n_exp
            dest, slot_tok, tile_e, n_active = _route(idx_pad[:, :TOP_K], tm_moe, n_exp, n_tiles)
            yb_slab = _moe(h2, tile_e, n_active, slot_tok, moe_w1[l // 2], moe_w3[l // 2], moe_w2[l // 2],
                      tm_moe, n_tiles)
            out = _combine(dest, yb_slab, x_new, gate_pad, mod_row(l, 5), final_gain.reshape(1, d))
    return out.reshape(bsz, t, d)
```

```python
import functools

import jax
import jax.numpy as jnp
from jax import lax
from jax.experimental import pallas as pl
from jax.experimental.pallas import tpu as pltpu

F32 = jnp.float32
BF16 = jnp.bfloat16

NORM_EPS = 1e-6
GROUPNORM_EPS = 64e-5
L2_EPS = 1e-12
POOL_WINDOWS = (2, 4, 8, 16)
HEAD_DIM = 64
TOP_K = 2
LANES = 128
SUBLANES = 8
CHUNK = 64
LR_PAD = 384
POOL_HALO = 16
VMEM_LIMIT = 56 * 1024 * 1024


def _cparams(sem):
    return pltpu.CompilerParams(dimension_semantics=sem, vmem_limit_bytes=VMEM_LIMIT)


def _mm(a, b):
    return jnp.dot(a.astype(BF16), b.astype(BF16), preferred_element_type=F32)


def _mm_nt(a, b):
    return lax.dot_general(a.astype(BF16), b.astype(BF16), (((1,), (1,)), ((), ())),
                           preferred_element_type=F32)


def _split(x):
    hi = x.astype(BF16)
    lo = (x - hi.astype(F32)).astype(BF16)
    return hi, lo


def _mm_exact_rhs(x, m):
    hi, lo = _split(x)
    return (jnp.dot(hi, m, preferred_element_type=F32) + jnp.dot(lo, m, preferred_element_type=F32))


def _ada_kernel(c_ref, w_ref, b_ref, o_ref):
    c = c_ref[...]
    act = c * jax.nn.sigmoid(c)
    o_ref[0] = jnp.sum(act * w_ref[0], axis=0, keepdims=True) + b_ref[0]


def _ada(c_col, ada_w, ada_b):
    depth, d, n = ada_w.shape
    tn = 1024
    return pl.pallas_call(
        _ada_kernel,
        grid=(depth, n // tn),
        in_specs=[pl.BlockSpec((d, 1), lambda l, j: (0, 0)),
                  pl.BlockSpec((1, d, tn), lambda l, j: (l, 0, j)),
                  pl.BlockSpec((1, 1, tn), lambda l, j: (l, 0, j))],
        out_specs=pl.BlockSpec((1, 1, tn), lambda l, j: (l, 0, j)),
        out_shape=jax.ShapeDtypeStruct((depth, 1, n), F32),
        compiler_params=_cparams(("arbitrary", "arbitrary")),
        name="ada_gemv",
    )(c_col, ada_w, ada_b.reshape(depth, 1, n))


def _in_kernel(x_ref, sc_ref, sh_ref, w_ref, wlr_ref, o_ref, olr_ref, h_sc):
    @pl.when(pl.program_id(1) == 0)
    def _():
        x = x_ref[...]
        ms = jnp.mean(x * x, axis=-1, keepdims=True)
        h = x * lax.rsqrt(ms + NORM_EPS) * (1.0 + sc_ref[...]) + sh_ref[...]
        hb = h.astype(BF16)
        h_sc[...] = hb
        olr_ref[...] = jnp.dot(hb, wlr_ref[...], preferred_element_type=F32)

    o_ref[...] = jnp.dot(h_sc[...], w_ref[0].astype(BF16), preferred_element_type=F32)


def _in_proj(x, scale, shift, w_in, layer, w_lr):
    t, d = x.shape
    n = w_in.shape[2]
    nlr = w_lr.shape[1]
    tm, tn = 1024, 512
    return pl.pallas_call(
        _in_kernel,
        grid=(t // tm, n // tn),
        in_specs=[pl.BlockSpec((tm, d), lambda i, j: (i, 0)),
                  pl.BlockSpec((1, d), lambda i, j: (0, 0)),
                  pl.BlockSpec((1, d), lambda i, j: (0, 0)),
                  pl.BlockSpec((1, d, tn), lambda i, j: (layer, 0, j)),
                  pl.BlockSpec((d, nlr), lambda i, j: (0, 0))],
        out_specs=[pl.BlockSpec((tm, tn), lambda i, j: (i, j)),
                   pl.BlockSpec((tm, nlr), lambda i, j: (i, 0))],
        out_shape=[jax.ShapeDtypeStruct((t, n), F32), jax.ShapeDtypeStruct((t, nlr), F32)],
        scratch_shapes=[pltpu.VMEM((tm, d), BF16)],
        compiler_params=_cparams(("arbitrary", "arbitrary")),
        name="in_proj",
    )(x, scale, shift, w_in, w_lr)


PV_W0, PV_A0, PV_V0, PV_KK, PV_KA, PV_GNW, PV_GNB, PV_MUR, PV_MUK, PV_MUV, PV_RK = range(11)
PV_ROWS = 16


def _rwkv_kernel(*refs, tt, ntile, nsteps, has_vres):
    if has_vres:
        (r_ref, k_ref, v_ref, lr_ref, vf_ref, w2_ref, pv_ref, yr_ref,
         s_sc, c_rkv, c_lr, qm_sc, yin_sc, cc_sc, bg_sc, gn_sc) = refs
    else:
        (r_ref, k_ref, v_ref, lr_ref, w2_ref, pv_ref, yr_ref, vfo_ref,
         s_sc, c_rkv, c_lr, qm_sc, yin_sc, cc_sc, bg_sc, gn_sc) = refs
    step = pl.program_id(0)
    is_drain = step == nsteps - 1
    stage_tile = jnp.minimum(step, nsteps - 2) % ntile
    chain_tile = jnp.maximum(step - 1, 0) % ntile
    chunks = range(tt // CHUNK)

    @pl.when(step == 0)
    def _():
        for ref in (s_sc, qm_sc, yin_sc, cc_sc, bg_sc, gn_sc):
            ref[...] = jnp.zeros_like(ref)

    @pl.when(stage_tile == 0)
    def _():
        c_rkv[...] = jnp.zeros_like(c_rkv)
        c_lr[...] = jnp.zeros_like(c_lr)

    li = lax.broadcasted_iota(jnp.int32, (LANES, LANES), 0)
    lj = lax.broadcasted_iota(jnp.int32, (LANES, LANES), 1)
    same_head = ((li >> 6) == (lj >> 6))
    head_ones = jnp.where(same_head, 1.0, 0.0).astype(BF16)

    def head_sum(x):
        return _mm_exact_rhs(x, head_ones)


    row = lax.broadcasted_iota(jnp.int32, (tt, 1), 0)

    def shift(x, carry_row):
        return jnp.where(row == 0, carry_row, pltpu.roll(x, 1, axis=0))

    pv = pv_ref[...]

    def prow(i):
        return pv[i:i + 1, :]

    r_raw, k_raw, v_raw = r_ref[...], k_ref[...], v_ref[...]
    zb = lr_ref[:, LR_PAD:2 * LR_PAD]
    c_rkv_used = jnp.where(is_drain, c_rkv[4:7, :], c_rkv[0:3, :])
    c_z = jnp.where(is_drain, c_lr[1:2, :], c_lr[0:1, :])
    r_prev = shift(r_raw, c_rkv_used[0:1, :])
    k_prev = shift(k_raw, c_rkv_used[1:2, :])
    v_prev = shift(v_raw, c_rkv_used[2:3, :])
    z = lr_ref[:, 0:LR_PAD] + shift(zb, c_z)
    c_rkv[4:7, :] = c_rkv_used
    c_lr[1:2, :] = c_z
    c_rkv[0:1, :] = r_raw[tt - 1:tt, :]
    c_rkv[1:2, :] = k_raw[tt - 1:tt, :]
    c_rkv[2:3, :] = v_raw[tt - 1:tt, :]
    c_lr[0:1, :] = zb[tt - 1:tt, :]

    r = r_raw + (r_prev - r_raw) * prow(PV_MUR)
    k = k_raw + (k_prev - k_raw) * prow(PV_MUK)
    v = v_raw + (v_prev - v_raw) * prow(PV_MUV)

    z_wa = z[:, 0:LANES]
    z_gv = z[:, LANES:LR_PAD]
    w_pre = prow(PV_W0) + _mm(jnp.tanh(z_wa), w2_ref[0:128, :])
    neg = -w_pre
    softplus = jnp.maximum(neg, 0.0) + jnp.log(1.0 + jnp.exp(-jnp.abs(neg)))
    logd = -jnp.exp(-softplus - 0.5)
    a_i = jax.nn.sigmoid(prow(PV_A0) + _mm(z_wa, w2_ref[128:256, :]))
    g = _mm(jax.nn.sigmoid(z_gv), w2_ref[256:512, :])
    if has_vres:
        v = v + (vf_ref[...] - v) * jax.nn.sigmoid(prow(PV_V0) + _mm(z_gv, w2_ref[512:768, :]))
    else:
        vfo_ref[...] = v

    kk = k * prow(PV_KK)
    kk = kk / jnp.maximum(jnp.sqrt(head_sum(kk * kk)), L2_EPS)
    k2 = k * (1.0 + (a_i - 1.0) * prow(PV_KA))
    a_vec = -kk
    b_vec = kk * a_i
    bonus = head_sum(r * k2 * prow(PV_RK)) * v

    ci = lax.broadcasted_iota(jnp.int32, (CHUNK, CHUNK), 0)
    cj = lax.broadcasted_iota(jnp.int32, (CHUNK, CHUNK), 1)
    tri_incl = jnp.where(ci >= cj, 1.0, 0.0).astype(BF16)
    strict = (li & (CHUNK - 1)) > (lj & (CHUNK - 1))
    incl = (li & (CHUNK - 1)) >= (lj & (CHUNK - 1))
    eye = li == lj
    eye_f = jnp.where(eye, 1.0, 0.0)
    off_masks = [((li >> (s + 1)) == (lj >> (s + 1))) & (((li >> s) & 1) == 1) & (((lj >> s) & 1) == 0)
                 for s in range(CHUNK.bit_length() - 1)]
    head0 = lax.broadcasted_iota(jnp.int32, (CHUNK, LANES), 1) < HEAD_DIM

    def stack(x):
        return jnp.concatenate([jnp.where(head0, x, 0.0), jnp.where(head0, 0.0, x)], axis=0)

    def prep(c):
        sl = slice(c * CHUNK, (c + 1) * CHUNK)
        ld = logd[sl]
        ld_hi, ld_lo = _split(ld)
        lp = (jnp.dot(tri_incl, ld_hi, preferred_element_type=F32)
              + jnp.dot(tri_incl, ld_lo, preferred_element_type=F32))
        le = lp[CHUNK - 1:CHUNK, :]
        e_p = jnp.exp(lp)
        e_n = jnp.exp(-lp)
        e_e = jnp.exp(le - lp)
        return dict(
            c=c, rh=stack(r[sl] * e_p), ah=stack(a_vec[sl] * jnp.exp(lp - ld)),
            bt=stack(b_vec[sl] * e_n).astype(BF16), kt=stack(k2[sl] * e_n).astype(BF16),
            bb_t=stack(b_vec[sl] * e_e).T.astype(BF16), kb_t=stack(k2[sl] * e_e).T.astype(BF16),
            vv=stack(v[sl]).astype(BF16), diag=jnp.where(eye, jnp.exp(le), 0.0))

    def substages(ops):
        def gram():
            for o in ops:
                ar = jnp.concatenate([o['ah'], o['rh']], axis=0).astype(BF16)
                gb = _mm_nt(ar, o['bt'])
                gk = _mm_nt(ar, o['kt'])
                o['a_ab'] = jnp.where(strict, gb[0:LANES], 0.0)
                o['a_rb'] = jnp.where(incl, gb[LANES:], 0.0).astype(BF16)
                o['a_ak'] = jnp.where(strict, gk[0:LANES], 0.0).astype(BF16)
                o['a_rk'] = jnp.where(incl, gk[LANES:], 0.0).astype(BF16)

        def values():
            for o in ops:
                o['x'] = _mm(o['a_ak'], o['vv'])
                o['yv'] = _mm(o['a_rk'], o['vv'])
                o['cv'] = _mm(o['kb_t'], o['vv'])
                o['t'] = eye_f + jnp.where(off_masks[0], o['a_ab'], 0.0)

        def level_a(off):
            def run():
                for o in ops:
                    o['ta'] = _mm(o['t'], jnp.where(off, o['a_ab'], 0.0))
            return run

        def level_b():
            for o in ops:
                o['t'] = o['t'] + _mm(o['ta'], o['t'])

        def apply_inverse():
            for o in ops:
                o['w'] = _mm(o['t'], jnp.concatenate([o['ah'], o['x']], axis=1)).astype(BF16)

        def emit():
            for o in ops:
                qy = _mm(o['a_rb'], o['w'])
                mc = _mm(o['bb_t'], o['w'])
                qm_sc[o['c']] = jnp.concatenate([o['rh'] + qy[:, 0:LANES], o['diag'] + mc[:, 0:LANES]],
                                                axis=0).astype(BF16)
                yin_sc[o['c']] = qy[:, LANES:] + o['yv']
                cc_sc[o['c']] = mc[:, LANES:] + o['cv']

        steps = [gram, values]
        for off in off_masks[1:]:
            steps += [level_a(off), level_b]
        return steps + [apply_inverse, emit]

    chain = dict(s=jnp.where(chain_tile == 0, 0.0, s_sc[...]), ys=[])

    def link(c):
        def run():
            res = _mm(qm_sc[c], chain['s'])
            y2 = res[0:LANES] + yin_sc[c]
            chain['ys'].append(y2[0:CHUNK] + y2[CHUNK:2 * CHUNK])
            chain['s'] = res[LANES:] + cc_sc[c]
        return run

    def finish_previous():
        s_sc[...] = chain['s']
        y = jnp.concatenate(chain['ys'], axis=0)
        mean = head_sum(y) * (1.0 / HEAD_DIM)
        dlt = y - mean
        var = head_sum(dlt * dlt) * (1.0 / HEAD_DIM)
        yn = dlt * lax.rsqrt(var + GROUPNORM_EPS) * gn_sc[0:1, :] + gn_sc[1:2, :]
        yr_ref[...] = (yn + bg_sc[0]) * bg_sc[1]

    steps = substages([prep(c) for c in chunks])
    fill = [link(c) for c in chunks] + [finish_previous]
    assert len(fill) < len(steps)
    for i, run in enumerate(steps):
        run()
        if i < len(fill):
            fill[i]()
    bg_sc[0] = bonus
    bg_sc[1] = g
    gn_sc[0:1, :] = prow(PV_GNW)
    gn_sc[1:2, :] = prow(PV_GNB)


def _rwkv(g_main, g_lr, v_first, w2s, pvec, has_vres):
    t = g_main.shape[0]
    c = pvec.shape[1]
    tt = 512
    ntile = t // tt
    npair = c // LANES
    nsteps = npair * ntile + 1
    col0 = c // LANES
    nch = tt // CHUNK

    def stage(s):
        ss = jnp.minimum(s, nsteps - 2)
        return ss // ntile, ss % ntile

    def chain(s):
        cs = jnp.maximum(s - 1, 0)
        return cs // ntile, cs % ntile

    def in_map(col_mult):
        return lambda s: (stage(s)[1], col_mult * col0 + stage(s)[0])

    in_specs = [pl.BlockSpec((tt, LANES), in_map(1)),
                pl.BlockSpec((tt, LANES), in_map(2)),
                pl.BlockSpec((tt, LANES), in_map(3)),
                pl.BlockSpec((tt, 2 * LR_PAD), lambda s: (stage(s)[1], 0))]
    args = [g_main, g_main, g_main, g_lr]
    if has_vres:
        in_specs.append(pl.BlockSpec((tt, LANES), in_map(0)))
        args.append(v_first)
    in_specs += [pl.BlockSpec((768, LANES), lambda s: (0, stage(s)[0])),
                 pl.BlockSpec((PV_ROWS, LANES), lambda s: (0, stage(s)[0]))]
    args += [w2s, pvec]
    yr_spec = pl.BlockSpec((tt, LANES), lambda s: (chain(s)[1], chain(s)[0]))
    out_shape = jax.ShapeDtypeStruct((t, c), F32)
    if has_vres:
        out_specs, out_shapes = yr_spec, out_shape
    else:
        out_specs, out_shapes = [yr_spec, pl.BlockSpec((tt, LANES), in_map(0))], [out_shape, out_shape]
    return pl.pallas_call(
        functools.partial(_rwkv_kernel, tt=tt, ntile=ntile, nsteps=nsteps, has_vres=has_vres),
        grid=(nsteps,),
        in_specs=in_specs,
        out_specs=out_specs,
        out_shape=out_shapes,
        scratch_shapes=[pltpu.VMEM((LANES, LANES), F32),
                        pltpu.VMEM((SUBLANES, LANES), F32),
                        pltpu.VMEM((SUBLANES, LR_PAD), F32),
                        pltpu.VMEM((nch, 2 * LANES, LANES), BF16),
                        pltpu.VMEM((nch, LANES, LANES), F32),
                        pltpu.VMEM((nch, LANES, LANES), F32),
                        pltpu.VMEM((2, tt, LANES), F32),
                        pltpu.VMEM((SUBLANES, LANES), F32)],
        compiler_params=_cparams(("arbitrary",)),
        name="rwkv7_vres" if has_vres else "rwkv7",
    )(*args)


def _mix_kernel(*refs, tm, has_router):
    if has_router:
        (x_ref, pp_ref, halo_ref, glp_ref, glr_ref, yr_ref, pw_ref, ps_ref, po_ref, ro_ref, mo_ref,
         g1_ref, sc2_ref, sh2_ref, rt_ref, xo_ref, h2_ref, idx_ref, gt_ref, ext) = refs
    else:
        (x_ref, pp_ref, halo_ref, glp_ref, glr_ref, yr_ref, pw_ref, ps_ref, po_ref, ro_ref, mo_ref,
         g1_ref, sc2_ref, sh2_ref, xo_ref, h2_ref, ext) = refs
    i = pl.program_id(0)
    ext[0:POOL_HALO, :] = jnp.where(i == 0, 0.0, halo_ref[...])
    ext[POOL_HALO:POOL_HALO + tm, :] = pp_ref[...]
    pos = i * tm + lax.broadcasted_iota(jnp.int32, (tm, 1), 0)
    gw = ext.shape[1] // len(POOL_WINDOWS)
    parts = []
    for gi, win in enumerate(POOL_WINDOWS):
        cs = slice(gi * gw, (gi + 1) * gw)
        cur = ext[POOL_HALO:POOL_HALO + tm, cs]
        s = cur
        for j in range(1, win):
            s = s + ext[POOL_HALO - j:POOL_HALO - j + tm, cs]
        cnt = jnp.minimum(pos + 1, win).astype(F32)
        pooled = s / cnt - cur
        parts.append(_mm(pooled, pw_ref[gi]))
    mixed = jnp.concatenate(parts, axis=-1) * ps_ref[...]
    y_pool = _mm(mixed, po_ref[...])
    y_rwkv = _mm(yr_ref[...], ro_ref[...])
    merged = jax.nn.sigmoid(glp_ref[...]) * y_pool + jax.nn.sigmoid(glr_ref[...]) * y_rwkv
    xn = x_ref[...] + g1_ref[...] * _mm(merged, mo_ref[...])
    xo_ref[...] = xn
    ms = jnp.mean(xn * xn, axis=-1, keepdims=True)
    h2 = xn * lax.rsqrt(ms + NORM_EPS) * (1.0 + sc2_ref[...]) + sh2_ref[...]
    if has_router:
        slab = h2.shape[1] // LANES
        for a in range(slab):
            h2_ref[pl.ds(a, tm, stride=slab), :] = h2[:, a * LANES:(a + 1) * LANES]
        rt = rt_ref[...]
        hi, lo = _split(h2)
        rhi, rlo = _split(rt)
        logits = (jnp.dot(hi, rhi, preferred_element_type=F32) + jnp.dot(lo, rhi, preferred_element_type=F32)
                  + jnp.dot(hi, rlo, preferred_element_type=F32))
        n_exp = 8
        lane = lax.broadcasted_iota(jnp.int32, logits.shape, 1)
        lane_f = lane.astype(F32)
        big = float(LANES)
        lg = jnp.where(lane < n_exp, logits, -jnp.inf)
        m1 = jnp.max(lg, axis=-1, keepdims=True)
        i1 = jnp.min(jnp.where(lg == m1, lane_f, big), axis=-1, keepdims=True)
        lg2 = jnp.where(lane_f == i1, -jnp.inf, lg)
        m2 = jnp.max(lg2, axis=-1, keepdims=True)
        i2 = jnp.min(jnp.where(lg2 == m2, lane_f, big), axis=-1, keepdims=True)
        e2 = jnp.exp(m2 - m1)
        p1 = 1.0 / (1.0 + e2)
        p2 = e2 / (1.0 + e2)
        idx_ref[...] = jnp.where(lane == 0, i1, jnp.where(lane == 1, i2, 0.0)).astype(jnp.int32)
        gt_ref[...] = jnp.where(lane == 0, p1, jnp.where(lane == 1, p2, 0.0))
    else:
        h2_ref[...] = h2.astype(h2_ref.dtype)


def _mix(x, g_main, yr, pool_w, pool_scale, pool_out, rk_out, mix_out, gate1, scale2, shift2, router_pad):
    t, d = x.shape
    c = yr.shape[1]
    tm = 256
    has_router = router_pad is not None
    halo_blocks = tm // POOL_HALO
    cb_glp = (4 * c) // d
    in_specs = [pl.BlockSpec((tm, d), lambda i: (i, 0)),
                pl.BlockSpec((tm, c), lambda i: (i, 0)),
                pl.BlockSpec((POOL_HALO, c), lambda i: (jnp.maximum(i * halo_blocks - 1, 0), 0)),
                pl.BlockSpec((tm, d), lambda i: (i, cb_glp)),
                pl.BlockSpec((tm, d), lambda i: (i, cb_glp + 1)),
                pl.BlockSpec((tm, c), lambda i: (i, 0)),
                pl.BlockSpec(pool_w.shape, lambda i: (0, 0, 0)),
                pl.BlockSpec((1, c), lambda i: (0, 0)),
                pl.BlockSpec((c, d), lambda i: (0, 0)),
                pl.BlockSpec((c, d), lambda i: (0, 0)),
                pl.BlockSpec((d, d), lambda i: (0, 0)),
                pl.BlockSpec((1, d), lambda i: (0, 0)),
                pl.BlockSpec((1, d), lambda i: (0, 0)),
                pl.BlockSpec((1, d), lambda i: (0, 0))]
    args = [x, g_main, g_main, g_main, g_main, yr, pool_w, pool_scale, pool_out, rk_out, mix_out,
            gate1, scale2, shift2]
    out_specs = [pl.BlockSpec((tm, d), lambda i: (i, 0))]
    out_shape = [jax.ShapeDtypeStruct((t, d), F32)]
    if has_router:
        slab = d // LANES
        out_specs.append(pl.BlockSpec((tm * slab, LANES), lambda i: (i, 0)))
        out_shape.append(jax.ShapeDtypeStruct((t * slab, LANES), F32))
    else:
        out_specs.append(pl.BlockSpec((tm, d), lambda i: (i, 0)))
        out_shape.append(jax.ShapeDtypeStruct((t, d), BF16))
    if has_router:
        in_specs.append(pl.BlockSpec((d, LANES), lambda i: (0, 0)))
        args.append(router_pad)
        out_specs += [pl.BlockSpec((tm, LANES), lambda i: (i, 0)), pl.BlockSpec((tm, LANES), lambda i: (i, 0))]
        out_shape += [jax.ShapeDtypeStruct((t, LANES), jnp.int32), jax.ShapeDtypeStruct((t, LANES), F32)]
    return pl.pallas_call(
        functools.partial(_mix_kernel, tm=tm, has_router=has_router),
        grid=(t // tm,),
        in_specs=in_specs,
        out_specs=out_specs,
        out_shape=out_shape,
        scratch_shapes=[pltpu.VMEM((POOL_HALO + tm, c), F32)],
        compiler_params=_cparams(("arbitrary",)),
        name="mix_router" if has_router else "mix",
    )(*args)


def _ffn_kernel(h_ref, x_ref, g_ref, w1_ref, w3_ref, w2_ref, o_ref, acc):
    f = pl.program_id(1)

    @pl.when(f == 0)
    def _():
        acc[...] = jnp.zeros_like(acc)

    h = h_ref[...]
    a = jnp.dot(h, w1_ref[...].astype(BF16), preferred_element_type=F32)
    b = jnp.dot(h, w3_ref[...].astype(BF16), preferred_element_type=F32)
    u = (a * jax.nn.sigmoid(a) * b).astype(BF16)
    acc[...] += jnp.dot(u, w2_ref[...].astype(BF16), preferred_element_type=F32)

    @pl.when(f == pl.num_programs(1) - 1)
    def _():
        o_ref[...] = x_ref[...] + g_ref[...] * acc[...]


def _ffn(h2, x, gate2, w1, w3, w2):
    t, d = x.shape
    ff = w1.shape[1]
    tm, tf = 512, 512
    return pl.pallas_call(
        _ffn_kernel,
        grid=(t // tm, ff // tf),
        in_specs=[pl.BlockSpec((tm, d), lambda i, f: (i, 0)),
                  pl.BlockSpec((tm, d), lambda i, f: (i, 0)),
                  pl.BlockSpec((1, d), lambda i, f: (0, 0)),
                  pl.BlockSpec((d, tf), lambda i, f: (0, f)),
                  pl.BlockSpec((d, tf), lambda i, f: (0, f)),
                  pl.BlockSpec((tf, d), lambda i, f: (f, 0))],
        out_specs=pl.BlockSpec((tm, d), lambda i, f: (i, 0)),
        out_shape=jax.ShapeDtypeStruct((t, d), F32),
        scratch_shapes=[pltpu.VMEM((tm, d), F32)],
        compiler_params=_cparams(("arbitrary", "arbitrary")),
        name="ffn_dense",
    )(h2, x, gate2, w1, w3, w2)


def _slab_copy(src_hbm, src_row, dst, slot, dst_row, slab, sem):
    src = src_hbm.at[pl.ds(pl.multiple_of(src_row * slab, slab), slab), :]
    return pltpu.make_async_copy(src, dst.at[slot, pl.ds(pl.multiple_of(dst_row * slab, slab), slab), :],
                                 sem.at[slot])


def _moe_kernel(te_ref, na_ref, tok_ref, h_hbm, w1_ref, w3_ref, w2_ref, o_ref, xg, xb, acc, sem, *, tm):
    b = pl.program_id(0)
    f = pl.program_id(1)
    nf = pl.num_programs(1)
    n_act = na_ref[0]
    slot = b % 2
    slab = xb.shape[1] // LANES

    def start_gather(tile, slt):
        def body(s, carry):
            _slab_copy(h_hbm, tok_ref[tile * tm + s], xg, slt, s, slab, sem).start()
            return carry
        lax.fori_loop(0, tm, body, 0)

    @pl.when(jnp.logical_and(f == 0, b < n_act))
    def _():
        @pl.when(b == 0)
        def _():
            start_gather(0, 0)

        @pl.when(b + 1 < n_act)
        def _():
            start_gather(b + 1, 1 - slot)

        pltpu.make_async_copy(h_hbm.at[pl.ds(0, tm * slab), :], xg.at[slot], sem.at[slot]).wait()
        for a in range(slab):
            xb[:, a * LANES:(a + 1) * LANES] = xg[slot, pl.ds(a, tm, stride=slab), :].astype(BF16)
        acc[...] = jnp.zeros_like(acc)

    @pl.when(b < n_act)
    def _():
        x = xb[...]
        a_ = jnp.dot(x, w1_ref[0].astype(BF16), preferred_element_type=F32)
        b_ = jnp.dot(x, w3_ref[0].astype(BF16), preferred_element_type=F32)
        u = (a_ * jax.nn.sigmoid(a_) * b_).astype(BF16)
        acc[...] += jnp.dot(u, w2_ref[0].astype(BF16), preferred_element_type=F32)

    @pl.when(f == nf - 1)
    def _():
        res = jnp.where(b < n_act, acc[...], 0.0)
        for a in range(slab):
            o_ref[pl.ds(a, tm, stride=slab), :] = res[:, a * LANES:(a + 1) * LANES]


def _moe(h2_slab, tile_e, n_active, slot_tok, w1, w3, w2, tm, n_tiles):
    n_exp, d, ff = w1.shape
    tf = 512
    nf = ff // tf

    def w13_map(b, f, te, na, tok):
        return (te[b], 0, jnp.where(b < na[0], f, nf - 1))

    def w2_map(b, f, te, na, tok):
        return (te[b], jnp.where(b < na[0], f, nf - 1), 0)

    grid_spec = pltpu.PrefetchScalarGridSpec(
        num_scalar_prefetch=3,
        grid=(n_tiles, nf),
        in_specs=[pl.BlockSpec(memory_space=pl.ANY),
                  pl.BlockSpec((1, d, tf), w13_map),
                  pl.BlockSpec((1, d, tf), w13_map),
                  pl.BlockSpec((1, tf, d), w2_map)],
        out_specs=pl.BlockSpec((tm * (d // LANES), LANES), lambda b, f, te, na, tok: (b, 0)),
        scratch_shapes=[pltpu.VMEM((2, tm * (d // LANES), LANES), F32),
                        pltpu.VMEM((tm, d), BF16),
                        pltpu.VMEM((tm, d), F32),
                        pltpu.SemaphoreType.DMA((2,))],
    )
    return pl.pallas_call(
        functools.partial(_moe_kernel, tm=tm),
        grid_spec=grid_spec,
        out_shape=jax.ShapeDtypeStruct((n_tiles * tm * (d // LANES), LANES), F32),
        compiler_params=_cparams(("arbitrary", "arbitrary")),
        name="moe_experts",
    )(tile_e, n_active, slot_tok, h2_slab, w1, w3, w2)


def _comb_kernel(dest_ref, yb_hbm, x_ref, gt_ref, g2_ref, fg_ref, o_ref, buf, sem, *, tm):
    i = pl.program_id(0)
    n = pl.num_programs(0)
    slot = i % 2
    nrow = TOP_K * tm
    slab = o_ref.shape[1] // LANES

    def start_gather(tile, slt):
        def body(s, carry):
            _slab_copy(yb_hbm, dest_ref[tile * nrow + s], buf, slt, s, slab, sem).start()
            return carry
        lax.fori_loop(0, nrow, body, 0)

    @pl.when(i == 0)
    def _():
        start_gather(0, 0)

    @pl.when(i + 1 < n)
    def _():
        start_gather(i + 1, 1 - slot)

    pltpu.make_async_copy(yb_hbm.at[pl.ds(0, nrow * slab), :], buf.at[slot], sem.at[slot]).wait()
    gt = gt_ref[...]
    p0 = gt[:, 0:1]
    p1 = gt[:, 1:2]
    cols = []
    for a in range(slab):
        y0 = buf[slot, pl.ds(a, tm, stride=TOP_K * slab), :]
        y1 = buf[slot, pl.ds(slab + a, tm, stride=TOP_K * slab), :]
        cols.append(y0 * p0 + y1 * p1)
    fmoe = jnp.concatenate(cols, axis=-1)
    xn = x_ref[...] + g2_ref[...] * fmoe
    ms = jnp.mean(xn * xn, axis=-1, keepdims=True)
    o_ref[...] = xn * lax.rsqrt(ms + NORM_EPS) * fg_ref[...]


def _combine(dest, yb_slab, x, gates, gate2, final_gain):
    t, d = x.shape
    tm = 256
    grid_spec = pltpu.PrefetchScalarGridSpec(
        num_scalar_prefetch=1,
        grid=(t // tm,),
        in_specs=[pl.BlockSpec(memory_space=pl.ANY),
                  pl.BlockSpec((tm, d), lambda i, dr: (i, 0)),
                  pl.BlockSpec((tm, LANES), lambda i, dr: (i, 0)),
                  pl.BlockSpec((1, d), lambda i, dr: (0, 0)),
                  pl.BlockSpec((1, d), lambda i, dr: (0, 0))],
        out_specs=pl.BlockSpec((tm, d), lambda i, dr: (i, 0)),
        scratch_shapes=[pltpu.VMEM((2, TOP_K * tm * (d // LANES), LANES), F32),
                        pltpu.SemaphoreType.DMA((2,))],
    )
    return pl.pallas_call(
        functools.partial(_comb_kernel, tm=tm),
        grid_spec=grid_spec,
        out_shape=jax.ShapeDtypeStruct((t, d), F32),
        compiler_params=_cparams(("arbitrary",)),
        name="moe_combine",
    )(dest, yb_slab, x, gates, gate2, final_gain)


def _pad_rows(w, rows):
    return jnp.pad(w, ((0, rows - w.shape[0]), (0, 0)))


def _lowrank_weights(l, rk_lerp, rk_w1, rk_a1, rk_g1, rk_vmu, rk_v1):
    d = rk_w1.shape[1]
    mus = [rk_lerp[l, 0], rk_lerp[l, 1], rk_lerp[l, 2]]
    ws = [rk_w1[l], rk_a1[l], rk_g1[l]]
    if l > 0:
        mus.append(rk_vmu[l - 1])
        ws.append(rk_v1[l - 1])
    wa = jnp.concatenate([(1.0 - m)[:, None] * w for m, w in zip(mus, ws)], axis=1)
    wb = jnp.concatenate([m[:, None] * w for m, w in zip(mus, ws)], axis=1)
    wa = jnp.pad(wa, ((0, 0), (0, LR_PAD - wa.shape[1])))
    wb = jnp.pad(wb, ((0, 0), (0, LR_PAD - wb.shape[1])))
    return jnp.concatenate([wa, wb], axis=1).astype(BF16)


def _second_stage_weights(l, rk_w2, rk_a2, rk_g2, rk_v2):
    c = rk_w2.shape[2]
    w2 = _pad_rows(rk_w2[l], 128)
    a2 = jnp.concatenate([jnp.zeros((64, c), F32), rk_a2[l]], axis=0)
    g2 = _pad_rows(rk_g2[l], 256)
    if l > 0:
        v2 = jnp.concatenate([jnp.zeros((160, c), F32), rk_v2[l - 1], jnp.zeros((64, c), F32)], axis=0)
    else:
        v2 = jnp.zeros((256, c), F32)
    return jnp.concatenate([w2, a2, g2, v2], axis=0).astype(BF16)


def _route(idx, tm, n_exp, n_tiles):
    n_assign = idx.size
    flat_e = idx.reshape(-1)
    onehot = (flat_e[:, None] == jnp.arange(n_exp, dtype=jnp.int32)[None, :]).astype(jnp.int32)
    csum = jnp.cumsum(onehot, axis=0)
    rank = jnp.take_along_axis(csum, flat_e[:, None], axis=1)[:, 0] - 1
    counts = csum[-1]
    padded = (counts + tm - 1) // tm * tm
    pad_end = jnp.cumsum(padded)
    pad_start = pad_end - padded
    dest = (pad_start[flat_e] + rank).astype(jnp.int32)
    slot_tok = jnp.zeros((n_tiles * tm,), jnp.int32).at[dest].set(
        jnp.arange(n_assign, dtype=jnp.int32) // TOP_K)
    n_active = (pad_end[-1] // tm).astype(jnp.int32)
    tile_ids = jnp.minimum(jnp.arange(n_tiles, dtype=jnp.int32), jnp.maximum(n_active - 1, 0))
    tile_e = jnp.minimum(jnp.searchsorted(pad_end, tile_ids * tm, side='right'), n_exp - 1).astype(jnp.int32)
    return dest, slot_tok, tile_e, n_active.reshape(1)


def kernel(x, c, ada_w, ada_b, w_in, pool_w, pool_scale, pool_out, rk_shift, rk_lerp, rk_w0, rk_w1, rk_w2, rk_a0, rk_a1, rk_a2, rk_g1, rk_g2, rk_kk, rk_ka, rk_rk, rk_gn_w, rk_gn_b, rk_vmu, rk_v0, rk_v1, rk_v2, rk_out, mix_out, ffn_w1, ffn_w3, ffn_w2, router, moe_w1, moe_w3, moe_w2, final_gain):
    bsz, t, d = x.shape
    depth = ada_w.shape[0]
    cw = rk_w0.shape[1]
    n_exp = router.shape[2]
    assert bsz == 1 and depth == 2 and cw % LANES == 0
    xs = x.reshape(t, d)
    mod = _ada(c.reshape(d, 1), ada_w, ada_b)

    def mod_row(l, i):
        return mod[l, :, i * d:(i + 1) * d]

    v_first = None
    out = None
    for l in range(depth):
        w_lr = _lowrank_weights(l, rk_lerp, rk_w1, rk_a1, rk_g1, rk_vmu, rk_v1)
        g_main, g_lr = _in_proj(xs, mod_row(l, 1), mod_row(l, 0), w_in, l, w_lr)
        w2s = _second_stage_weights(l, rk_w2, rk_a2, rk_g2, rk_v2)
        zeros_c = jnp.zeros((cw,), F32)
        rows = [rk_w0[l], rk_a0[l], rk_v0[l - 1] if l > 0 else zeros_c, rk_kk[l], rk_ka[l],
                rk_gn_w[l], rk_gn_b[l], rk_shift[l, 0], rk_shift[l, 1], rk_shift[l, 2],
                rk_rk[l].reshape(cw)]
        pvec = _pad_rows(jnp.stack(rows, axis=0), PV_ROWS)
        if l == 0:
            yr, v_first = _rwkv(g_main, g_lr, None, w2s, pvec, has_vres=False)
        else:
            yr = _rwkv(g_main, g_lr, v_first, w2s, pvec, has_vres=True)
        is_moe = (l % 2 == 1)
        router_pad = None
        if is_moe:
            router_pad = jnp.pad(router[l // 2], ((0, 0), (0, LANES - n_exp)))
        res = _mix(xs, g_main, yr, pool_w[l].astype(BF16), pool_scale[l].reshape(1, -1),
                   pool_out[l].astype(BF16), rk_out[l].astype(BF16), mix_out[l].astype(BF16),
                   mod_row(l, 2), mod_row(l, 4), mod_row(l, 3), router_pad)
        if not is_moe:
            x_new, h2 = res
            xs = _ffn(h2, x_new, mod_row(l, 5), ffn_w1[l // 2].astype(BF16), ffn_w3[l // 2].astype(BF16),
                      ffn_w2[l // 2].astype(BF16))
        else:
            x_new, h2, idx_pad, gate_pad = res
            tm_moe = 512
            n_tiles = (t * TOP_K) // tm_moe + n_exp
            dest, slot_tok, tile_e, n_active = _route(idx_pad[:, :TOP_K], tm_moe, n_exp, n_tiles)
            yb_slab = _moe(h2, tile_e, n_active, slot_tok, moe_w1[l // 2], moe_w3[l // 2], moe_w2[l // 2],
                      tm_moe, n_tiles)
            out = _combine(dest, yb_slab, x_new, gate_pad, mod_row(l, 5), final_gain.reshape(1, d))
    return out.reshape(bsz, t, d)
```

```python
import functools

import jax
import jax.numpy as jnp
from jax import lax
from jax.experimental import pallas as pl
from jax.experimental.pallas import tpu as pltpu

F32 = jnp.float32
BF16 = jnp.bfloat16

NORM_EPS = 1e-6
GROUPNORM_EPS = 64e-5
L2_EPS = 1e-12
POOL_WINDOWS = (2, 4, 8, 16)
HEAD_DIM = 64
TOP_K = 2
LANES = 128
SUBLANES = 8
CHUNK = 64
LR_PAD = 384
POOL_HALO = 16
VMEM_LIMIT = 56 * 1024 * 1024


def _cparams(sem):
    return pltpu.CompilerParams(dimension_semantics=sem, vmem_limit_bytes=VMEM_LIMIT)


def _mm(a, b):
    return jnp.dot(a.astype(BF16), b.astype(BF16), preferred_element_type=F32)


def _mm_nt(a, b):
    return lax.dot_general(a.astype(BF16), b.astype(BF16), (((1,), (1,)), ((), ())),
                           preferred_element_type=F32)


def _split(x):
    hi = x.astype(BF16)
    lo = (x - hi.astype(F32)).astype(BF16)
    return hi, lo


def _mm_exact_rhs(x, m):
    hi, lo = _split(x)
    return (jnp.dot(hi, m, preferred_element_type=F32) + jnp.dot(lo, m, preferred_element_type=F32))


def _ada_kernel(c_ref, w_ref, b_ref, o_ref):
    c = c_ref[...]
    act = c * jax.nn.sigmoid(c)
    o_ref[0] = jnp.sum(act * w_ref[0], axis=0, keepdims=True) + b_ref[0]


def _ada(c_col, ada_w, ada_b):
    depth, d, n = ada_w.shape
    tn = 1024
    return pl.pallas_call(
        _ada_kernel,
        grid=(depth, n // tn),
        in_specs=[pl.BlockSpec((d, 1), lambda l, j: (0, 0)),
                  pl.BlockSpec((1, d, tn), lambda l, j: (l, 0, j)),
                  pl.BlockSpec((1, 1, tn), lambda l, j: (l, 0, j))],
        out_specs=pl.BlockSpec((1, 1, tn), lambda l, j: (l, 0, j)),
        out_shape=jax.ShapeDtypeStruct((depth, 1, n), F32),
        compiler_params=_cparams(("arbitrary", "arbitrary")),
        name="ada_gemv",
    )(c_col, ada_w, ada_b.reshape(depth, 1, n))


def _in_kernel(x_ref, sc_ref, sh_ref, w_ref, wlr_ref, o_ref, olr_ref, h_sc):
    @pl.when(pl.program_id(1) == 0)
    def _():
        x = x_ref[...]
        ms = jnp.mean(x * x, axis=-1, keepdims=True)
        h = x * lax.rsqrt(ms + NORM_EPS) * (1.0 + sc_ref[...]) + sh_ref[...]
        hb = h.astype(BF16)
        h_sc[...] = hb
        olr_ref[...] = jnp.dot(hb, wlr_ref[...], preferred_element_type=F32)

    o_ref[...] = jnp.dot(h_sc[...], w_ref[0].astype(BF16), preferred_element_type=F32)


def _in_proj(x, scale, shift, w_in, layer, w_lr):
    t, d = x.shape
    n = w_in.shape[2]
    nlr = w_lr.shape[1]
    tm, tn = 1024, 512
    return pl.pallas_call(
        _in_kernel,
        grid=(t // tm, n // tn),
        in_specs=[pl.BlockSpec((tm, d), lambda i, j: (i, 0)),
                  pl.BlockSpec((1, d), lambda i, j: (0, 0)),
                  pl.BlockSpec((1, d), lambda i, j: (0, 0)),
                  pl.BlockSpec((1, d, tn), lambda i, j: (layer, 0, j)),
                  pl.BlockSpec((d, nlr), lambda i, j: (0, 0))],
        out_specs=[pl.BlockSpec((tm, tn), lambda i, j: (i, j)),
                   pl.BlockSpec((tm, nlr), lambda i, j: (i, 0))],
        out_shape=[jax.ShapeDtypeStruct((t, n), F32), jax.ShapeDtypeStruct((t, nlr), F32)],
        scratch_shapes=[pltpu.VMEM((tm, d), BF16)],
        compiler_params=_cparams(("arbitrary", "arbitrary")),
        name="in_proj",
    )(x, scale, shift, w_in, w_lr)


PV_W0, PV_A0, PV_V0, PV_KK, PV_KA, PV_GNW, PV_GNB, PV_MUR, PV_MUK, PV_MUV, PV_RK = range(11)
PV_ROWS = 16


def _rwkv_kernel(*refs, tt, ntile, nsteps, has_vres, n_cast):
    n_in = 7 if has_vres else 6
    n_out = 1 if has_vres else 2
    ins = refs[:n_in]
    cast_src = refs[n_in:n_in + n_cast]
    outs = refs[n_in + n_cast:n_in + n_cast + n_out]
    cast_dst = refs[n_in + n_cast + n_out:n_in + 2 * n_cast + n_out]
    (s_sc, c_rkv, c_lr, qm_sc, yin_sc, cc_sc, bg_sc, gn_sc) = refs[n_in + 2 * n_cast + n_out:]
    if has_vres:
        (r_ref, k_ref, v_ref, lr_ref, vf_ref, w2_ref, pv_ref), (yr_ref,) = ins, outs
    else:
        (r_ref, k_ref, v_ref, lr_ref, w2_ref, pv_ref), (yr_ref, vfo_ref) = ins, outs
    step = pl.program_id(0)
    is_drain = step == nsteps - 1
    stage_tile = jnp.minimum(step, nsteps - 2) % ntile
    chain_tile = jnp.maximum(step - 1, 0) % ntile
    chunks = range(tt // CHUNK)

    @pl.when(step == 0)
    def _():
        for ref in (s_sc, qm_sc, yin_sc, cc_sc, bg_sc, gn_sc):
            ref[...] = jnp.zeros_like(ref)

    @pl.when(stage_tile == 0)
    def _():
        c_rkv[...] = jnp.zeros_like(c_rkv)
        c_lr[...] = jnp.zeros_like(c_lr)

    li = lax.broadcasted_iota(jnp.int32, (LANES, LANES), 0)
    lj = lax.broadcasted_iota(jnp.int32, (LANES, LANES), 1)
    same_head = ((li >> 6) == (lj >> 6))
    head_ones = jnp.where(same_head, 1.0, 0.0).astype(BF16)

    def head_sum(x):
        return _mm_exact_rhs(x, head_ones)

    row = lax.broadcasted_iota(jnp.int32, (tt, 1), 0)

    def shift(x, carry_row):
        return jnp.where(row == 0, carry_row, pltpu.roll(x, 1, axis=0))

    pv = pv_ref[...]

    def prow(i):
        return pv[i:i + 1, :]

    r_raw, k_raw, v_raw = r_ref[...], k_ref[...], v_ref[...]
    zb = lr_ref[:, LR_PAD:2 * LR_PAD]
    c_rkv_used = jnp.where(is_drain, c_rkv[4:7, :], c_rkv[0:3, :])
    c_z = jnp.where(is_drain, c_lr[1:2, :], c_lr[0:1, :])
    r_prev = shift(r_raw, c_rkv_used[0:1, :])
    k_prev = shift(k_raw, c_rkv_used[1:2, :])
    v_prev = shift(v_raw, c_rkv_used[2:3, :])
    z = lr_ref[:, 0:LR_PAD] + shift(zb, c_z)
    c_rkv[4:7, :] = c_rkv_used
    c_lr[1:2, :] = c_z
    c_rkv[0:1, :] = r_raw[tt - 1:tt, :]
    c_rkv[1:2, :] = k_raw[tt - 1:tt, :]
    c_rkv[2:3, :] = v_raw[tt - 1:tt, :]
    c_lr[0:1, :] = zb[tt - 1:tt, :]

    r = r_raw + (r_prev - r_raw) * prow(PV_MUR)
    k = k_raw + (k_prev - k_raw) * prow(PV_MUK)
    v = v_raw + (v_prev - v_raw) * prow(PV_MUV)

    z_wa = z[:, 0:LANES]
    z_gv = z[:, LANES:LR_PAD]
    w_pre = prow(PV_W0) + _mm(jnp.tanh(z_wa), w2_ref[0:128, :])
    neg = -w_pre
    softplus = jnp.maximum(neg, 0.0) + jnp.log(1.0 + jnp.exp(-jnp.abs(neg)))
    logd = -jnp.exp(-softplus - 0.5)
    a_i = jax.nn.sigmoid(prow(PV_A0) + _mm(z_wa, w2_ref[128:256, :]))
    g = _mm(jax.nn.sigmoid(z_gv), w2_ref[256:512, :])
    if has_vres:
        v = v + (vf_ref[...] - v) * jax.nn.sigmoid(prow(PV_V0) + _mm(z_gv, w2_ref[512:768, :]))
    else:
        vfo_ref[...] = v

    kk = k * prow(PV_KK)
    kk = kk / jnp.maximum(jnp.sqrt(head_sum(kk * kk)), L2_EPS)
    k2 = k * (1.0 + (a_i - 1.0) * prow(PV_KA))
    a_vec = -kk
    b_vec = kk * a_i
    bonus = head_sum(r * k2 * prow(PV_RK)) * v

    ci = lax.broadcasted_iota(jnp.int32, (CHUNK, CHUNK), 0)
    cj = lax.broadcasted_iota(jnp.int32, (CHUNK, CHUNK), 1)
    tri_incl = jnp.where(ci >= cj, 1.0, 0.0).astype(BF16)
    strict = (li & (CHUNK - 1)) > (lj & (CHUNK - 1))
    incl = (li & (CHUNK - 1)) >= (lj & (CHUNK - 1))
    eye = li == lj
    eye_f = jnp.where(eye, 1.0, 0.0)
    off_masks = [((li >> (s + 1)) == (lj >> (s + 1))) & (((li >> s) & 1) == 1) & (((lj >> s) & 1) == 0)
                 for s in range(CHUNK.bit_length() - 1)]
    head0 = lax.broadcasted_iota(jnp.int32, (CHUNK, LANES), 1) < HEAD_DIM

    def stack(x):
        return jnp.concatenate([jnp.where(head0, x, 0.0), jnp.where(head0, 0.0, x)], axis=0)

    def prep(c):
        sl = slice(c * CHUNK, (c + 1) * CHUNK)
        ld = logd[sl]
        ld_hi, ld_lo = _split(ld)
        lp = (jnp.dot(tri_incl, ld_hi, preferred_element_type=F32)
              + jnp.dot(tri_incl, ld_lo, preferred_element_type=F32))
        le = lp[CHUNK - 1:CHUNK, :]
        e_p = jnp.exp(lp)
        e_n = jnp.exp(-lp)
        e_e = jnp.exp(le - lp)
        return dict(
            c=c, rh=stack(r[sl] * e_p), ah=stack(a_vec[sl] * jnp.exp(lp - ld)),
            bt=stack(b_vec[sl] * e_n).astype(BF16), kt=stack(k2[sl] * e_n).astype(BF16),
            bb_t=stack(b_vec[sl] * e_e).T.astype(BF16), kb_t=stack(k2[sl] * e_e).T.astype(BF16),
            vv=stack(v[sl]).astype(BF16), diag=jnp.where(eye, jnp.exp(le), 0.0))

    def substages(ops):
        def gram():
            for o in ops:
                ar = jnp.concatenate([o['ah'], o['rh']], axis=0).astype(BF16)
                gb = _mm_nt(ar, o['bt'])
                gk = _mm_nt(ar, o['kt'])
                o['a_ab'] = jnp.where(strict, gb[0:LANES], 0.0)
                o['a_rb'] = jnp.where(incl, gb[LANES:], 0.0).astype(BF16)
                o['a_ak'] = jnp.where(strict, gk[0:LANES], 0.0).astype(BF16)
                o['a_rk'] = jnp.where(incl, gk[LANES:], 0.0).astype(BF16)

        def values():
            for o in ops:
                o['x'] = _mm(o['a_ak'], o['vv'])
                o['yv'] = _mm(o['a_rk'], o['vv'])
                o['cv'] = _mm(o['kb_t'], o['vv'])
                o['t'] = eye_f + jnp.where(off_masks[0], o['a_ab'], 0.0)

        def level_a(off):
            def run():
                for o in ops:
                    o['ta'] = _mm(o['t'], jnp.where(off, o['a_ab'], 0.0))
            return run

        def level_b():
            for o in ops:
                o['t'] = o['t'] + _mm(o['ta'], o['t'])

        def apply_inverse():
            for o in ops:
                o['w'] = _mm(o['t'], jnp.concatenate([o['ah'], o['x']], axis=1)).astype(BF16)

        def emit():
            for o in ops:
                qy = _mm(o['a_rb'], o['w'])
                mc = _mm(o['bb_t'], o['w'])
                qm_sc[o['c']] = jnp.concatenate([o['rh'] + qy[:, 0:LANES], o['diag'] + mc[:, 0:LANES]],
                                                axis=0).astype(BF16)
                yin_sc[o['c']] = qy[:, LANES:] + o['yv']
                cc_sc[o['c']] = mc[:, LANES:] + o['cv']

        steps = [gram, values]
        for off in off_masks[1:]:
            steps += [level_a(off), level_b]
        return steps + [apply_inverse, emit]

    chain = dict(s=jnp.where(chain_tile == 0, 0.0, s_sc[...]), ys=[])

    def link(c):
        def run():
            res = _mm(qm_sc[c], chain['s'])
            y2 = res[0:LANES] + yin_sc[c]
            chain['ys'].append(y2[0:CHUNK] + y2[CHUNK:2 * CHUNK])
            chain['s'] = res[LANES:] + cc_sc[c]
        return run

    def finish_previous():
        s_sc[...] = chain['s']
        y = jnp.concatenate(chain['ys'], axis=0)
        mean = head_sum(y) * (1.0 / HEAD_DIM)
        dlt = y - mean
        var = head_sum(dlt * dlt) * (1.0 / HEAD_DIM)
        yn = dlt * lax.rsqrt(var + GROUPNORM_EPS) * gn_sc[0:1, :] + gn_sc[1:2, :]
        yr_ref[...] = (yn + bg_sc[0]) * bg_sc[1]

    def cast_group(i, n):
        def run():
            for src, dst in list(zip(cast_src, cast_dst))[i::n]:
                dst[...] = src[...].astype(BF16)
        return run

    steps = substages([prep(c) for c in chunks])
    fill = [link(c) for c in chunks] + [finish_previous]
    n_free = len(steps) - 1 - len(fill)
    fill += [cast_group(i, n_free) for i in range(n_free)]
    assert len(fill) < len(steps)
    for i, run in enumerate(steps):
        run()
        if i < len(fill):
            fill[i]()
    bg_sc[0] = bonus
    bg_sc[1] = g
    gn_sc[0:1, :] = prow(PV_GNW)
    gn_sc[1:2, :] = prow(PV_GNB)


RWKV_TILE = 512


def _rwkv_work_steps(t, c):
    return (c // LANES) * (t // RWKV_TILE)


def _cast_job(src, lo, hi, steps):
    rows = hi - lo
    count = max(n for n in range(1, steps + 1) if rows % n == 0 and (rows // n) % 16 == 0)
    blk = rows // count
    assert lo % blk == 0
    return (src, blk, lo // blk, count)


def _rwkv(g_main, g_lr, v_first, w2s, pvec, has_vres, casts=()):
    t = g_main.shape[0]
    c = pvec.shape[1]
    tt = RWKV_TILE
    ntile = t // tt
    npair = c // LANES
    nsteps = _rwkv_work_steps(t, c) + 1
    col0 = c // LANES
    nch = tt // CHUNK

    def stage(s):
        ss = jnp.minimum(s, nsteps - 2)
        return ss // ntile, ss % ntile

    def chain(s):
        cs = jnp.maximum(s - 1, 0)
        return cs // ntile, cs % ntile

    def in_map(col_mult):
        return lambda s: (stage(s)[1], col_mult * col0 + stage(s)[0])

    in_specs = [pl.BlockSpec((tt, LANES), in_map(1)),
                pl.BlockSpec((tt, LANES), in_map(2)),
                pl.BlockSpec((tt, LANES), in_map(3)),
                pl.BlockSpec((tt, 2 * LR_PAD), lambda s: (stage(s)[1], 0))]
    args = [g_main, g_main, g_main, g_lr]
    if has_vres:
        in_specs.append(pl.BlockSpec((tt, LANES), in_map(0)))
        args.append(v_first)
    in_specs += [pl.BlockSpec((768, LANES), lambda s: (0, stage(s)[0])),
                 pl.BlockSpec((PV_ROWS, LANES), lambda s: (0, stage(s)[0]))]
    args += [w2s, pvec]
    yr_spec = pl.BlockSpec((tt, LANES), lambda s: (chain(s)[1], chain(s)[0]))
    out_shape = jax.ShapeDtypeStruct((t, c), F32)
    if has_vres:
        out_specs, out_shapes = [yr_spec], [out_shape]
    else:
        out_specs, out_shapes = [yr_spec, pl.BlockSpec((tt, LANES), in_map(0))], [out_shape, out_shape]
    for src, rows, first, count in casts:
        blk = (rows, src.shape[1])
        in_specs.append(pl.BlockSpec(blk, lambda s, first=first, count=count: (first + jnp.minimum(s, count - 1), 0)))
        args.append(src)
        out_specs.append(pl.BlockSpec(blk, lambda s, count=count: (jnp.minimum(s, count - 1), 0)))
        out_shapes.append(jax.ShapeDtypeStruct((rows * count, src.shape[1]), BF16))
    return pl.pallas_call(
        functools.partial(_rwkv_kernel, tt=tt, ntile=ntile, nsteps=nsteps, has_vres=has_vres, n_cast=len(casts)),
        grid=(nsteps,),
        in_specs=in_specs,
        out_specs=out_specs,
        out_shape=out_shapes,
        scratch_shapes=[pltpu.VMEM((LANES, LANES), F32),
                        pltpu.VMEM((SUBLANES, LANES), F32),
                        pltpu.VMEM((SUBLANES, LR_PAD), F32),
                        pltpu.VMEM((nch, 2 * LANES, LANES), BF16),
                        pltpu.VMEM((nch, LANES, LANES), F32),
                        pltpu.VMEM((nch, LANES, LANES), F32),
                        pltpu.VMEM((2, tt, LANES), F32),
                        pltpu.VMEM((SUBLANES, LANES), F32)],
        compiler_params=_cparams(("arbitrary",)),
        name="rwkv7_vres" if has_vres else "rwkv7",
    )(*args)


def _mix_kernel(*refs, tm, has_router):
    if has_router:
        (x_ref, pp_ref, halo_ref, glp_ref, glr_ref, yr_ref, pw_ref, ps_ref, po_ref, ro_ref, mo_ref,
         g1_ref, sc2_ref, sh2_ref, rt_ref, xo_ref, h2_ref, idx_ref, gt_ref, ext) = refs
    else:
        (x_ref, pp_ref, halo_ref, glp_ref, glr_ref, yr_ref, pw_ref, ps_ref, po_ref, ro_ref, mo_ref,
         g1_ref, sc2_ref, sh2_ref, xo_ref, h2_ref, ext) = refs
    i = pl.program_id(0)
    ext[0:POOL_HALO, :] = jnp.where(i == 0, 0.0, halo_ref[...])
    ext[POOL_HALO:POOL_HALO + tm, :] = pp_ref[...]
    pos = i * tm + lax.broadcasted_iota(jnp.int32, (tm, 1), 0)
    gw = ext.shape[1] // len(POOL_WINDOWS)
    parts = []
    for gi, win in enumerate(POOL_WINDOWS):
        cs = slice(gi * gw, (gi + 1) * gw)
        cur = ext[POOL_HALO:POOL_HALO + tm, cs]
        s = cur
        for j in range(1, win):
            s = s + ext[POOL_HALO - j:POOL_HALO - j + tm, cs]
        cnt = jnp.minimum(pos + 1, win).astype(F32)
        pooled = s / cnt - cur
        parts.append(_mm(pooled, pw_ref[gi]))
    mixed = jnp.concatenate(parts, axis=-1) * ps_ref[...]
    y_pool = _mm(mixed, po_ref[...])
    y_rwkv = _mm(yr_ref[...], ro_ref[...])
    merged = jax.nn.sigmoid(glp_ref[...]) * y_pool + jax.nn.sigmoid(glr_ref[...]) * y_rwkv
    xn = x_ref[...] + g1_ref[...] * _mm(merged, mo_ref[...])
    xo_ref[...] = xn
    ms = jnp.mean(xn * xn, axis=-1, keepdims=True)
    h2 = xn * lax.rsqrt(ms + NORM_EPS) * (1.0 + sc2_ref[...]) + sh2_ref[...]
    if has_router:
        slab = h2.shape[1] // LANES
        for a in range(slab):
            h2_ref[pl.ds(a, tm, stride=slab), :] = h2[:, a * LANES:(a + 1) * LANES]
        rt = rt_ref[...]
        hi, lo = _split(h2)
        rhi, rlo = _split(rt)
        logits = (jnp.dot(hi, rhi, preferred_element_type=F32) + jnp.dot(lo, rhi, preferred_element_type=F32)
                  + jnp.dot(hi, rlo, preferred_element_type=F32))
        n_exp = 8
        lane = lax.broadcasted_iota(jnp.int32, logits.shape, 1)
        lane_f = lane.astype(F32)
        big = float(LANES)
        lg = jnp.where(lane < n_exp, logits, -jnp.inf)
        m1 = jnp.max(lg, axis=-1, keepdims=True)
        i1 = jnp.min(jnp.where(lg == m1, lane_f, big), axis=-1, keepdims=True)
        lg2 = jnp.where(lane_f == i1, -jnp.inf, lg)
        m2 = jnp.max(lg2, axis=-1, keepdims=True)
        i2 = jnp.min(jnp.where(lg2 == m2, lane_f, big), axis=-1, keepdims=True)
        e2 = jnp.exp(m2 - m1)
        p1 = 1.0 / (1.0 + e2)
        p2 = e2 / (1.0 + e2)
        idx_ref[...] = jnp.where(lane == 0, i1, jnp.where(lane == 1, i2, 0.0)).astype(jnp.int32)
        gt_ref[...] = jnp.where(lane == 0, p1, jnp.where(lane == 1, p2, 0.0))
    else:
        h2_ref[...] = h2.astype(h2_ref.dtype)


def _mix(x, g_main, yr, pool_w, pool_scale, pool_out, rk_out, mix_out, gate1, scale2, shift2, router_pad):
    t, d = x.shape
    c = yr.shape[1]
    tm = 256
    has_router = router_pad is not None
    halo_blocks = tm // POOL_HALO
    cb_glp = (4 * c) // d
    in_specs = [pl.BlockSpec((tm, d), lambda i: (i, 0)),
                pl.BlockSpec((tm, c), lambda i: (i, 0)),
                pl.BlockSpec((POOL_HALO, c), lambda i: (jnp.maximum(i * halo_blocks - 1, 0), 0)),
                pl.BlockSpec((tm, d), lambda i: (i, cb_glp)),
                pl.BlockSpec((tm, d), lambda i: (i, cb_glp + 1)),
                pl.BlockSpec((tm, c), lambda i: (i, 0)),
                pl.BlockSpec(pool_w.shape, lambda i: (0, 0, 0)),
                pl.BlockSpec((1, c), lambda i: (0, 0)),
                pl.BlockSpec((c, d), lambda i: (0, 0)),
                pl.BlockSpec((c, d), lambda i: (0, 0)),
                pl.BlockSpec((d, d), lambda i: (0, 0)),
                pl.BlockSpec((1, d), lambda i: (0, 0)),
                pl.BlockSpec((1, d), lambda i: (0, 0)),
                pl.BlockSpec((1, d), lambda i: (0, 0))]
    args = [x, g_main, g_main, g_main, g_main, yr, pool_w, pool_scale, pool_out, rk_out, mix_out,
            gate1, scale2, shift2]
    out_specs = [pl.BlockSpec((tm, d), lambda i: (i, 0))]
    out_shape = [jax.ShapeDtypeStruct((t, d), F32)]
    if has_router:
        slab = d // LANES
        out_specs.append(pl.BlockSpec((tm * slab, LANES), lambda i: (i, 0)))
        out_shape.append(jax.ShapeDtypeStruct((t * slab, LANES), F32))
    else:
        out_specs.append(pl.BlockSpec((tm, d), lambda i: (i, 0)))
        out_shape.append(jax.ShapeDtypeStruct((t, d), BF16))
    if has_router:
        in_specs.append(pl.BlockSpec((d, LANES), lambda i: (0, 0)))
        args.append(router_pad)
        out_specs += [pl.BlockSpec((tm, LANES), lambda i: (i, 0)), pl.BlockSpec((tm, LANES), lambda i: (i, 0))]
        out_shape += [jax.ShapeDtypeStruct((t, LANES), jnp.int32), jax.ShapeDtypeStruct((t, LANES), F32)]
    return pl.pallas_call(
        functools.partial(_mix_kernel, tm=tm, has_router=has_router),
        grid=(t // tm,),
        in_specs=in_specs,
        out_specs=out_specs,
        out_shape=out_shape,
        scratch_shapes=[pltpu.VMEM((POOL_HALO + tm, c), F32)],
        compiler_params=_cparams(("arbitrary",)),
        name="mix_router" if has_router else "mix",
    )(*args)


def _ffn_kernel(h_ref, x_ref, g_ref, w1_ref, w3_ref, w2_ref, o_ref, acc):
    f = pl.program_id(1)

    @pl.when(f == 0)
    def _():
        acc[...] = jnp.zeros_like(acc)

    h = h_ref[...]
    a = jnp.dot(h, w1_ref[...].astype(BF16), preferred_element_type=F32)
    b = jnp.dot(h, w3_ref[...].astype(BF16), preferred_element_type=F32)
    u = (a * jax.nn.sigmoid(a) * b).astype(BF16)
    acc[...] += jnp.dot(u, w2_ref[...].astype(BF16), preferred_element_type=F32)

    @pl.when(f == pl.num_programs(1) - 1)
    def _():
        o_ref[...] = x_ref[...] + g_ref[...] * acc[...]


def _ffn(h2, x, gate2, w1, w3, w2):
    t, d = x.shape
    ff = w1.shape[1]
    tm, tf = 512, 512
    return pl.pallas_call(
        _ffn_kernel,
        grid=(t // tm, ff // tf),
        in_specs=[pl.BlockSpec((tm, d), lambda i, f: (i, 0)),
                  pl.BlockSpec((tm, d), lambda i, f: (i, 0)),
                  pl.BlockSpec((1, d), lambda i, f: (0, 0)),
                  pl.BlockSpec((d, tf), lambda i, f: (0, f)),
                  pl.BlockSpec((d, tf), lambda i, f: (0, f)),
                  pl.BlockSpec((tf, d), lambda i, f: (f, 0))],
        out_specs=pl.BlockSpec((tm, d), lambda i, f: (i, 0)),
        out_shape=jax.ShapeDtypeStruct((t, d), F32),
        scratch_shapes=[pltpu.VMEM((tm, d), F32)],
        compiler_params=_cparams(("arbitrary", "arbitrary")),
        name="ffn_dense",
    )(h2, x, gate2, w1, w3, w2)


def _slab_copy(src_hbm, src_row, dst, slot, dst_row, slab, sem):
    src = src_hbm.at[pl.ds(pl.multiple_of(src_row * slab, slab), slab), :]
    return pltpu.make_async_copy(src, dst.at[slot, pl.ds(pl.multiple_of(dst_row * slab, slab), slab), :],
                                 sem.at[slot])


def _moe_kernel(te_ref, na_ref, tok_ref, h_hbm, w1l_ref, w3l_ref, w2l_ref, w1h_ref, w3h_ref, w2h_ref,
                o_ref, xg, xb, acc, sem, *, tm, n_lo):
    b = pl.program_id(0)
    f = pl.program_id(1)
    nf = pl.num_programs(1)
    n_act = na_ref[0]
    slot = b % 2
    slab = xb.shape[1] // LANES

    def start_gather(tile, slt):
        def body(s, carry):
            _slab_copy(h_hbm, tok_ref[tile * tm + s], xg, slt, s, slab, sem).start()
            return carry
        lax.fori_loop(0, tm, body, 0)

    @pl.when(jnp.logical_and(f == 0, b < n_act))
    def _():
        @pl.when(b == 0)
        def _():
            start_gather(0, 0)

        @pl.when(b + 1 < n_act)
        def _():
            start_gather(b + 1, 1 - slot)

        pltpu.make_async_copy(h_hbm.at[pl.ds(0, tm * slab), :], xg.at[slot], sem.at[slot]).wait()
        for a in range(slab):
            xb[:, a * LANES:(a + 1) * LANES] = xg[slot, pl.ds(a, tm, stride=slab), :].astype(BF16)
        acc[...] = jnp.zeros_like(acc)

    def swiglu_step(w1_ref, w3_ref, w2_ref):
        x = xb[...]
        a_ = jnp.dot(x, w1_ref[0].astype(BF16), preferred_element_type=F32)
        b_ = jnp.dot(x, w3_ref[0].astype(BF16), preferred_element_type=F32)
        u = (a_ * jax.nn.sigmoid(a_) * b_).astype(BF16)
        acc[...] += jnp.dot(u, w2_ref[0].astype(BF16), preferred_element_type=F32)

    in_lo = te_ref[b] < n_lo

    @pl.when(jnp.logical_and(b < n_act, in_lo))
    def _():
        swiglu_step(w1l_ref, w3l_ref, w2l_ref)

    @pl.when(jnp.logical_and(b < n_act, jnp.logical_not(in_lo)))
    def _():
        swiglu_step(w1h_ref, w3h_ref, w2h_ref)

    @pl.when(f == nf - 1)
    def _():
        res = jnp.where(b < n_act, acc[...], 0.0)
        for a in range(slab):
            o_ref[pl.ds(a, tm, stride=slab), :] = res[:, a * LANES:(a + 1) * LANES]


def _moe(h2_slab, tile_e, n_active, slot_tok, w_lo, w_hi, tm, n_tiles):
    n_lo, d, ff = w_lo[0].shape
    tf = 512
    nf = ff // tf

    def maps(lo):
        def pick(b, f, te, na):
            e = te[b]
            mine = jnp.logical_and(b < na[0], (e < n_lo) if lo else (e >= n_lo))
            e_loc = jnp.minimum(e, n_lo - 1) if lo else jnp.maximum(e - n_lo, 0)
            return e_loc, jnp.where(mine, f, (nf - 1) if lo else 0)

        def w13_map(b, f, te, na, tok):
            e_loc, fb = pick(b, f, te, na)
            return (e_loc, 0, fb)

        def w2_map(b, f, te, na, tok):
            e_loc, fb = pick(b, f, te, na)
            return (e_loc, fb, 0)

        return [pl.BlockSpec((1, d, tf), w13_map), pl.BlockSpec((1, d, tf), w13_map),
                pl.BlockSpec((1, tf, d), w2_map)]

    grid_spec = pltpu.PrefetchScalarGridSpec(
        num_scalar_prefetch=3,
        grid=(n_tiles, nf),
        in_specs=[pl.BlockSpec(memory_space=pl.ANY)] + maps(True) + maps(False),
        out_specs=pl.BlockSpec((tm * (d // LANES), LANES), lambda b, f, te, na, tok: (b, 0)),
        scratch_shapes=[pltpu.VMEM((2, tm * (d // LANES), LANES), F32),
                        pltpu.VMEM((tm, d), BF16),
                        pltpu.VMEM((tm, d), F32),
                        pltpu.SemaphoreType.DMA((2,))],
    )
    return pl.pallas_call(
        functools.partial(_moe_kernel, tm=tm, n_lo=n_lo),
        grid_spec=grid_spec,
        out_shape=jax.ShapeDtypeStruct((n_tiles * tm * (d // LANES), LANES), F32),
        compiler_params=_cparams(("arbitrary", "arbitrary")),
        name="moe_experts",
    )(tile_e, n_active, slot_tok, h2_slab, *w_lo, *w_hi)


def _comb_kernel(dest_ref, yb_hbm, x_ref, gt_ref, g2_ref, fg_ref, o_ref, buf, sem, *, tm):
    i = pl.program_id(0)
    n = pl.num_programs(0)
    slot = i % 2
    nrow = TOP_K * tm
    slab = o_ref.shape[1] // LANES

    def start_gather(tile, slt):
        def body(s, carry):
            _slab_copy(yb_hbm, dest_ref[tile * nrow + s], buf, slt, s, slab, sem).start()
            return carry
        lax.fori_loop(0, nrow, body, 0)

    @pl.when(i == 0)
    def _():
        start_gather(0, 0)

    @pl.when(i + 1 < n)
    def _():
        start_gather(i + 1, 1 - slot)

    pltpu.make_async_copy(yb_hbm.at[pl.ds(0, nrow * slab), :], buf.at[slot], sem.at[slot]).wait()
    gt = gt_ref[...]
    p0 = gt[:, 0:1]
    p1 = gt[:, 1:2]
    cols = []
    for a in range(slab):
        y0 = buf[slot, pl.ds(a, tm, stride=TOP_K * slab), :]
        y1 = buf[slot, pl.ds(slab + a, tm, stride=TOP_K * slab), :]
        cols.append(y0 * p0 + y1 * p1)
    fmoe = jnp.concatenate(cols, axis=-1)
    xn = x_ref[...] + g2_ref[...] * fmoe
    ms = jnp.mean(xn * xn, axis=-1, keepdims=True)
    o_ref[...] = xn * lax.rsqrt(ms + NORM_EPS) * fg_ref[...]


def _combine(dest, yb_slab, x, gates, gate2, final_gain):
    t, d = x.shape
    tm = 256
    grid_spec = pltpu.PrefetchScalarGridSpec(
        num_scalar_prefetch=1,
        grid=(t // tm,),
        in_specs=[pl.BlockSpec(memory_space=pl.ANY),
                  pl.BlockSpec((tm, d), lambda i, dr: (i, 0)),
                  pl.BlockSpec((tm, LANES), lambda i, dr: (i, 0)),
                  pl.BlockSpec((1, d), lambda i, dr: (0, 0)),
                  pl.BlockSpec((1, d), lambda i, dr: (0, 0))],
        out_specs=pl.BlockSpec((tm, d), lambda i, dr: (i, 0)),
        scratch_shapes=[pltpu.VMEM((2, TOP_K * tm * (d // LANES), LANES), F32),
                        pltpu.SemaphoreType.DMA((2,))],
    )
    return pl.pallas_call(
        functools.partial(_comb_kernel, tm=tm),
        grid_spec=grid_spec,
        out_shape=jax.ShapeDtypeStruct((t, d), F32),
        compiler_params=_cparams(("arbitrary",)),
        name="moe_combine",
    )(dest, yb_slab, x, gates, gate2, final_gain)


def _pad_rows(w, rows):
    return jnp.pad(w, ((0, rows - w.shape[0]), (0, 0)))


def _lowrank_weights(l, rk_lerp, rk_w1, rk_a1, rk_g1, rk_vmu, rk_v1):
    mus = [rk_lerp[l, 0], rk_lerp[l, 1], rk_lerp[l, 2]]
    ws = [rk_w1[l], rk_a1[l], rk_g1[l]]
    if l > 0:
        mus.append(rk_vmu[l - 1])
        ws.append(rk_v1[l - 1])
    wa = jnp.concatenate([(1.0 - m)[:, None] * w for m, w in zip(mus, ws)], axis=1)
    wb = jnp.concatenate([m[:, None] * w for m, w in zip(mus, ws)], axis=1)
    wa = jnp.pad(wa, ((0, 0), (0, LR_PAD - wa.shape[1])))
    wb = jnp.pad(wb, ((0, 0), (0, LR_PAD - wb.shape[1])))
    return jnp.concatenate([wa, wb], axis=1).astype(BF16)


def _second_stage_weights(l, rk_w2, rk_a2, rk_g2, rk_v2):
    c = rk_w2.shape[2]
    w2 = _pad_rows(rk_w2[l], 128)
    a2 = jnp.concatenate([jnp.zeros((64, c), F32), rk_a2[l]], axis=0)
    g2 = _pad_rows(rk_g2[l], 256)
    if l > 0:
        v2 = jnp.concatenate([jnp.zeros((160, c), F32), rk_v2[l - 1], jnp.zeros((64, c), F32)], axis=0)
    else:
        v2 = jnp.zeros((256, c), F32)
    return jnp.concatenate([w2, a2, g2, v2], axis=0).astype(BF16)


def _route(idx, tm, n_exp, n_tiles):
    n_assign = idx.size
    flat_e = idx.reshape(-1)
    onehot = (flat_e[:, None] == jnp.arange(n_exp, dtype=jnp.int32)[None, :]).astype(jnp.int32)
    csum = jnp.cumsum(onehot, axis=0)
    rank = jnp.take_along_axis(csum, flat_e[:, None], axis=1)[:, 0] - 1
    counts = csum[-1]
    padded = (counts + tm - 1) // tm * tm
    pad_end = jnp.cumsum(padded)
    pad_start = pad_end - padded
    dest = (pad_start[flat_e] + rank).astype(jnp.int32)
    slot_tok = jnp.zeros((n_tiles * tm,), jnp.int32).at[dest].set(
        jnp.arange(n_assign, dtype=jnp.int32) // TOP_K)
    n_active = (pad_end[-1] // tm).astype(jnp.int32)
    tile_ids = jnp.minimum(jnp.arange(n_tiles, dtype=jnp.int32), jnp.maximum(n_active - 1, 0))
    tile_e = jnp.minimum(jnp.searchsorted(pad_end, tile_ids * tm, side='right'), n_exp - 1).astype(jnp.int32)
    return dest, slot_tok, tile_e, n_active.reshape(1)


def kernel(x, c, ada_w, ada_b, w_in, pool_w, pool_scale, pool_out, rk_shift, rk_lerp, rk_w0, rk_w1, rk_w2, rk_a0, rk_a1, rk_a2, rk_g1, rk_g2, rk_kk, rk_ka, rk_rk, rk_gn_w, rk_gn_b, rk_vmu, rk_v0, rk_v1, rk_v2, rk_out, mix_out, ffn_w1, ffn_w3, ffn_w2, router, moe_w1, moe_w3, moe_w2, final_gain):
    bsz, t, d = x.shape
    depth = ada_w.shape[0]
    cw = rk_w0.shape[1]
    n_exp = router.shape[2]
    assert bsz == 1 and depth == 2 and cw % LANES == 0
    xs = x.reshape(t, d)
    mod = _ada(c.reshape(d, 1), ada_w, ada_b)

    def mod_row(l, i):
        return mod[l, :, i * d:(i + 1) * d]

    v_first = None
    out = None
    for l in range(depth):
        w_lr = _lowrank_weights(l, rk_lerp, rk_w1, rk_a1, rk_g1, rk_vmu, rk_v1)
        g_main, g_lr = _in_proj(xs, mod_row(l, 1), mod_row(l, 0), w_in, l, w_lr)
        w2s = _second_stage_weights(l, rk_w2, rk_a2, rk_g2, rk_v2)
        zeros_c = jnp.zeros((cw,), F32)
        rows = [rk_w0[l], rk_a0[l], rk_v0[l - 1] if l > 0 else zeros_c, rk_kk[l], rk_ka[l],
                rk_gn_w[l], rk_gn_b[l], rk_shift[l, 0], rk_shift[l, 1], rk_shift[l, 2],
                rk_rk[l].reshape(cw)]
        pvec = _pad_rows(jnp.stack(rows, axis=0), PV_ROWS)
        steps = _rwkv_work_steps(t, cw)
        if l == 0:
            dense = [ffn_w1[0], ffn_w3[0], ffn_w2[0]]
            experts = [moe_w1[0].reshape(-1, moe_w1.shape[-1]), moe_w3[0].reshape(-1, moe_w3.shape[-1]),
                       moe_w2[0].reshape(-1, moe_w2.shape[-1])]
            casts = ([_cast_job(w, 0, w.shape[0], steps) for w in dense]
                     + [_cast_job(w, 0, w.shape[0] // 2, steps) for w in experts])
            yr, v_first, *cast_out = _rwkv(g_main, g_lr, None, w2s, pvec, has_vres=False, casts=casts)
            dense_bf, experts_lo = cast_out[:3], cast_out[3:]
        else:
            casts = [_cast_job(w, w.shape[0] // 2, w.shape[0], steps) for w in experts]
            yr, *experts_hi = _rwkv(g_main, g_lr, v_first, w2s, pvec, has_vres=True, casts=casts)
        is_moe = (l % 2 == 1)
        router_pad = None
        if is_moe:
            router_pad = jnp.pad(router[l // 2], ((0, 0), (0, LANES - n_exp)))
        res = _mix(xs, g_main, yr, pool_w[l].astype(BF16), pool_scale[l].reshape(1, -1),
                   pool_out[l].astype(BF16), rk_out[l].astype(BF16), mix_out[l].astype(BF16),
                   mod_row(l, 2), mod_row(l, 4), mod_row(l, 3), router_pad)
        if not is_moe:
            x_new, h2 = res
            xs = _ffn(h2, x_new, mod_row(l, 5), *dense_bf)
        else:
            x_new, h2, idx_pad, gate_pad = res
            tm_moe = 512
            n_tiles = (t * TOP_K) // tm_moe + n_exp
            dest, slot_tok, tile_e, n_active = _route(idx_pad[:, :TOP_K], tm_moe, n_exp, n_tiles)
            n_lo = n_exp // 2
            w_lo = [w.reshape((n_lo,) + src.shape[2:]) for w, src in zip(experts_lo, (moe_w1, moe_w3, moe_w2))]
            w_hi = [w.reshape((n_exp - n_lo,) + src.shape[2:])
                    for w, src in zip(experts_hi, (moe_w1, moe_w3, moe_w2))]
            yb_slab = _moe(h2, tile_e, n_active, slot_tok, w_lo, w_hi, tm_moe, n_tiles)
            out = _combine(dest, yb_slab, x_new, gate_pad, mod_row(l, 5), final_gain.reshape(1, d))
    return out.reshape(bsz, t, d)
```

```python
import functools

import jax
import jax.numpy as jnp
from jax import lax
from jax.experimental import pallas as pl
from jax.experimental.pallas import tpu as pltpu

F32 = jnp.float32
BF16 = jnp.bfloat16

NORM_EPS = 1e-6
GROUPNORM_EPS = 64e-5
L2_EPS = 1e-12
POOL_WINDOWS = (2, 4, 8, 16)
HEAD_DIM = 64
TOP_K = 2
LANES = 128
SUBLANES = 8
CHUNK = 64
LR_PAD = 384
POOL_HALO = 16
VMEM_LIMIT = 56 * 1024 * 1024


def _cparams(sem):
    return pltpu.CompilerParams(dimension_semantics=sem, vmem_limit_bytes=VMEM_LIMIT)


def _mm(a, b):
    return jnp.dot(a.astype(BF16), b.astype(BF16), preferred_element_type=F32)


def _mm_nt(a, b):
    return lax.dot_general(a.astype(BF16), b.astype(BF16), (((1,), (1,)), ((), ())),
                           preferred_element_type=F32)


def _split(x):
    hi = x.astype(BF16)
    lo = (x - hi.astype(F32)).astype(BF16)
    return hi, lo


def _mm_exact_rhs(x, m):
    hi, lo = _split(x)
    return (jnp.dot(hi, m, preferred_element_type=F32) + jnp.dot(lo, m, preferred_element_type=F32))


def _ada_kernel(c_ref, w_ref, b_ref, o_ref):
    c = c_ref[...]
    act = c * jax.nn.sigmoid(c)
    o_ref[0] = jnp.sum(act * w_ref[0], axis=0, keepdims=True) + b_ref[0]


def _ada(c_col, ada_w, ada_b):
    depth, d, n = ada_w.shape
    tn = 1024
    return pl.pallas_call(
        _ada_kernel,
        grid=(depth, n // tn),
        in_specs=[pl.BlockSpec((d, 1), lambda l, j: (0, 0)),
                  pl.BlockSpec((1, d, tn), lambda l, j: (l, 0, j)),
                  pl.BlockSpec((1, 1, tn), lambda l, j: (l, 0, j))],
        out_specs=pl.BlockSpec((1, 1, tn), lambda l, j: (l, 0, j)),
        out_shape=jax.ShapeDtypeStruct((depth, 1, n), F32),
        compiler_params=_cparams(("arbitrary", "arbitrary")),
        name="ada_gemv",
    )(c_col, ada_w, ada_b.reshape(depth, 1, n))


def _in_kernel(x_ref, sc_ref, sh_ref, w_ref, wlr_ref, o_ref, olr_ref, h_sc):
    @pl.when(pl.program_id(1) == 0)
    def _():
        x = x_ref[...]
        ms = jnp.mean(x * x, axis=-1, keepdims=True)
        h = x * lax.rsqrt(ms + NORM_EPS) * (1.0 + sc_ref[...]) + sh_ref[...]
        hb = h.astype(BF16)
        h_sc[...] = hb
        olr_ref[...] = jnp.dot(hb, wlr_ref[...], preferred_element_type=F32)

    o_ref[...] = jnp.dot(h_sc[...], w_ref[0].astype(BF16), preferred_element_type=F32)


def _in_proj(x, scale, shift, w_in, layer, w_lr):
    t, d = x.shape
    n = w_in.shape[2]
    nlr = w_lr.shape[1]
    tm, tn = 1024, 512
    return pl.pallas_call(
        _in_kernel,
        grid=(t // tm, n // tn),
        in_specs=[pl.BlockSpec((tm, d), lambda i, j: (i, 0)),
                  pl.BlockSpec((1, d), lambda i, j: (0, 0)),
                  pl.BlockSpec((1, d), lambda i, j: (0, 0)),
                  pl.BlockSpec((1, d, tn), lambda i, j: (layer, 0, j)),
                  pl.BlockSpec((d, nlr), lambda i, j: (0, 0))],
        out_specs=[pl.BlockSpec((tm, tn), lambda i, j: (i, j)),
                   pl.BlockSpec((tm, nlr), lambda i, j: (i, 0))],
        out_shape=[jax.ShapeDtypeStruct((t, n), F32), jax.ShapeDtypeStruct((t, nlr), F32)],
        scratch_shapes=[pltpu.VMEM((tm, d), BF16)],
        compiler_params=_cparams(("arbitrary", "arbitrary")),
        name="in_proj",
    )(x, scale, shift, w_in, w_lr)


PV_W0, PV_A0, PV_V0, PV_KK, PV_KA, PV_GNW, PV_GNB, PV_MUR, PV_MUK, PV_MUV, PV_RK = range(11)
PV_ROWS = 16


def _rwkv_kernel(*refs, tt, ntile, nsteps, has_vres, n_cast):
    n_in = 7 if has_vres else 6
    n_out = 1 if has_vres else 2
    ins = refs[:n_in]
    cast_src = refs[n_in:n_in + n_cast]
    outs = refs[n_in + n_cast:n_in + n_cast + n_out]
    cast_dst = refs[n_in + n_cast + n_out:n_in + 2 * n_cast + n_out]
    (s_sc, c_rkv, c_lr, qm_sc, yin_sc, cc_sc, bg_sc, gn_sc) = refs[n_in + 2 * n_cast + n_out:]
    if has_vres:
        (r_ref, k_ref, v_ref, lr_ref, vf_ref, w2_ref, pv_ref), (yr_ref,) = ins, outs
    else:
        (r_ref, k_ref, v_ref, lr_ref, w2_ref, pv_ref), (yr_ref, vfo_ref) = ins, outs
    step = pl.program_id(0)
    is_drain = step == nsteps - 1
    stage_tile = jnp.minimum(step, nsteps - 2) % ntile
    chain_tile = jnp.maximum(step - 1, 0) % ntile
    chunks = range(tt // CHUNK)

    @pl.when(step == 0)
    def _():
        for ref in (s_sc, qm_sc, yin_sc, cc_sc, bg_sc, gn_sc):
            ref[...] = jnp.zeros_like(ref)

    @pl.when(stage_tile == 0)
    def _():
        c_rkv[...] = jnp.zeros_like(c_rkv)
        c_lr[...] = jnp.zeros_like(c_lr)

    li = lax.broadcasted_iota(jnp.int32, (LANES, LANES), 0)
    lj = lax.broadcasted_iota(jnp.int32, (LANES, LANES), 1)
    same_head = ((li >> 6) == (lj >> 6))
    head_ones = jnp.where(same_head, 1.0, 0.0).astype(BF16)

    def head_sum(x):
        return _mm_exact_rhs(x, head_ones)

    row = lax.broadcasted_iota(jnp.int32, (tt, 1), 0)

    def shift(x, carry_row):
        return jnp.where(row == 0, carry_row, pltpu.roll(x, 1, axis=0))

    pv = pv_ref[...]

    def prow(i):
        return pv[i:i + 1, :]

    r_raw, k_raw, v_raw = r_ref[...], k_ref[...], v_ref[...]
    zb = lr_ref[:, LR_PAD:2 * LR_PAD]
    c_rkv_used = jnp.where(is_drain, c_rkv[4:7, :], c_rkv[0:3, :])
    c_z = jnp.where(is_drain, c_lr[1:2, :], c_lr[0:1, :])
    r_prev = shift(r_raw, c_rkv_used[0:1, :])
    k_prev = shift(k_raw, c_rkv_used[1:2, :])
    v_prev = shift(v_raw, c_rkv_used[2:3, :])
    z = lr_ref[:, 0:LR_PAD] + shift(zb, c_z)
    c_rkv[4:7, :] = c_rkv_used
    c_lr[1:2, :] = c_z
    c_rkv[0:1, :] = r_raw[tt - 1:tt, :]
    c_rkv[1:2, :] = k_raw[tt - 1:tt, :]
    c_rkv[2:3, :] = v_raw[tt - 1:tt, :]
    c_lr[0:1, :] = zb[tt - 1:tt, :]

    r = r_raw + (r_prev - r_raw) * prow(PV_MUR)
    k = k_raw + (k_prev - k_raw) * prow(PV_MUK)
    v = v_raw + (v_prev - v_raw) * prow(PV_MUV)

    z_wa = z[:, 0:LANES]
    z_gv = z[:, LANES:LR_PAD]
    w_pre = prow(PV_W0) + _mm(jnp.tanh(z_wa), w2_ref[0:128, :])
    neg = -w_pre
    softplus = jnp.maximum(neg, 0.0) + jnp.log(1.0 + jnp.exp(-jnp.abs(neg)))
    logd = -jnp.exp(-softplus - 0.5)
    a_i = jax.nn.sigmoid(prow(PV_A0) + _mm(z_wa, w2_ref[128:256, :]))
    g = _mm(jax.nn.sigmoid(z_gv), w2_ref[256:512, :])
    if has_vres:
        v = v + (vf_ref[...] - v) * jax.nn.sigmoid(prow(PV_V0) + _mm(z_gv, w2_ref[512:768, :]))
    else:
        vfo_ref[...] = v

    kk = k * prow(PV_KK)
    kk = kk / jnp.maximum(jnp.sqrt(head_sum(kk * kk)), L2_EPS)
    k2 = k * (1.0 + (a_i - 1.0) * prow(PV_KA))
    a_vec = -kk
    b_vec = kk * a_i
    bonus = head_sum(r * k2 * prow(PV_RK)) * v

    ci = lax.broadcasted_iota(jnp.int32, (CHUNK, CHUNK), 0)
    cj = lax.broadcasted_iota(jnp.int32, (CHUNK, CHUNK), 1)
    tri_incl = jnp.where(ci >= cj, 1.0, 0.0).astype(BF16)
    strict = (li & (CHUNK - 1)) > (lj & (CHUNK - 1))
    incl = (li & (CHUNK - 1)) >= (lj & (CHUNK - 1))
    eye = li == lj
    eye_f = jnp.where(eye, 1.0, 0.0)
    off_masks = [((li >> (s + 1)) == (lj >> (s + 1))) & (((li >> s) & 1) == 1) & (((lj >> s) & 1) == 0)
                 for s in range(CHUNK.bit_length() - 1)]
    head0 = lax.broadcasted_iota(jnp.int32, (CHUNK, LANES), 1) < HEAD_DIM

    def stack(x):
        return jnp.concatenate([jnp.where(head0, x, 0.0), jnp.where(head0, 0.0, x)], axis=0)

    def prep(c):
        sl = slice(c * CHUNK, (c + 1) * CHUNK)
        ld = logd[sl]
        ld_hi, ld_lo = _split(ld)
        lp = (jnp.dot(tri_incl, ld_hi, preferred_element_type=F32)
              + jnp.dot(tri_incl, ld_lo, preferred_element_type=F32))
        le = lp[CHUNK - 1:CHUNK, :]
        e_p = jnp.exp(lp)
        e_n = jnp.exp(-lp)
        e_e = jnp.exp(le - lp)
        return dict(
            c=c, rh=stack(r[sl] * e_p), ah=stack(a_vec[sl] * jnp.exp(lp - ld)),
            bt=stack(b_vec[sl] * e_n).astype(BF16), kt=stack(k2[sl] * e_n).astype(BF16),
            bb_t=stack(b_vec[sl] * e_e).T.astype(BF16), kb_t=stack(k2[sl] * e_e).T.astype(BF16),
            vv=stack(v[sl]).astype(BF16), diag=jnp.where(eye, jnp.exp(le), 0.0))

    def substages(ops):
        def gram():
            for o in ops:
                ar = jnp.concatenate([o['ah'], o['rh']], axis=0).astype(BF16)
                gb = _mm_nt(ar, o['bt'])
                gk = _mm_nt(ar, o['kt'])
                o['a_ab'] = jnp.where(strict, gb[0:LANES], 0.0)
                o['a_rb'] = jnp.where(incl, gb[LANES:], 0.0).astype(BF16)
                o['a_ak'] = jnp.where(strict, gk[0:LANES], 0.0).astype(BF16)
                o['a_rk'] = jnp.where(incl, gk[LANES:], 0.0).astype(BF16)

        def values():
            for o in ops:
                o['x'] = _mm(o['a_ak'], o['vv'])
                o['yv'] = _mm(o['a_rk'], o['vv'])
                o['cv'] = _mm(o['kb_t'], o['vv'])
                o['t'] = eye_f + jnp.where(off_masks[0], o['a_ab'], 0.0)

        def level_a(off):
            def run():
                for o in ops:
                    o['ta'] = _mm(o['t'], jnp.where(off, o['a_ab'], 0.0))
            return run

        def level_b():
            for o in ops:
                o['t'] = o['t'] + _mm(o['ta'], o['t'])

        def apply_inverse():
            for o in ops:
                o['w'] = _mm(o['t'], jnp.concatenate([o['ah'], o['x']], axis=1)).astype(BF16)

        def emit():
            for o in ops:
                qy = _mm(o['a_rb'], o['w'])
                mc = _mm(o['bb_t'], o['w'])
                qm_sc[o['c']] = jnp.concatenate([o['rh'] + qy[:, 0:LANES], o['diag'] + mc[:, 0:LANES]],
                                                axis=0).astype(BF16)
                yin_sc[o['c']] = qy[:, LANES:] + o['yv']
                cc_sc[o['c']] = mc[:, LANES:] + o['cv']

        steps = [gram, values]
        for off in off_masks[1:]:
            steps += [level_a(off), level_b]
        return steps + [apply_inverse, emit]

    chain = dict(s=jnp.where(chain_tile == 0, 0.0, s_sc[...]), ys=[])

    def link(c):
        def run():
            res = _mm(qm_sc[c], chain['s'])
            y2 = res[0:LANES] + yin_sc[c]
            chain['ys'].append(y2[0:CHUNK] + y2[CHUNK:2 * CHUNK])
            chain['s'] = res[LANES:] + cc_sc[c]
        return run

    def finish_previous():
        s_sc[...] = chain['s']
        y = jnp.concatenate(chain['ys'], axis=0)
        mean = head_sum(y) * (1.0 / HEAD_DIM)
        dlt = y - mean
        var = head_sum(dlt * dlt) * (1.0 / HEAD_DIM)
        yn = dlt * lax.rsqrt(var + GROUPNORM_EPS) * gn_sc[0:1, :] + gn_sc[1:2, :]
        yr_ref[...] = (yn + bg_sc[0]) * bg_sc[1]

    def cast_group(i, n):
        def run():
            for src, dst in list(zip(cast_src, cast_dst))[i::n]:
                dst[...] = src[...].astype(BF16)
        return run

    steps = substages([prep(c) for c in chunks])
    fill = [link(c) for c in chunks] + [finish_previous]
    n_free = len(steps) - 1 - len(fill)
    fill += [cast_group(i, n_free) for i in range(n_free)]
    assert len(fill) < len(steps)
    for i, run in enumerate(steps):
        run()
        if i < len(fill):
            fill[i]()
    bg_sc[0] = bonus
    bg_sc[1] = g
    gn_sc[0:1, :] = prow(PV_GNW)
    gn_sc[1:2, :] = prow(PV_GNB)


RWKV_TILE = 512


def _rwkv_work_steps(t, c):
    return (c // LANES) * (t // RWKV_TILE)


def _cast_job(src, lo, hi, steps):
    rows = hi - lo
    count = max(n for n in range(1, steps + 1) if rows % n == 0 and (rows // n) % 16 == 0)
    blk = rows // count
    assert lo % blk == 0
    return (src, blk, lo // blk, count)


def _rwkv(g_main, g_lr, v_first, w2s, pvec, has_vres, casts=()):
    t = g_main.shape[0]
    c = pvec.shape[1]
    tt = RWKV_TILE
    ntile = t // tt
    npair = c // LANES
    nsteps = _rwkv_work_steps(t, c) + 1
    col0 = c // LANES
    nch = tt // CHUNK

    def stage(s):
        ss = jnp.minimum(s, nsteps - 2)
        return ss // ntile, ss % ntile

    def chain(s):
        cs = jnp.maximum(s - 1, 0)
        return cs // ntile, cs % ntile

    def in_map(col_mult):
        return lambda s: (stage(s)[1], col_mult * col0 + stage(s)[0])

    in_specs = [pl.BlockSpec((tt, LANES), in_map(1)),
                pl.BlockSpec((tt, LANES), in_map(2)),
                pl.BlockSpec((tt, LANES), in_map(3)),
                pl.BlockSpec((tt, 2 * LR_PAD), lambda s: (stage(s)[1], 0))]
    args = [g_main, g_main, g_main, g_lr]
    if has_vres:
        in_specs.append(pl.BlockSpec((tt, LANES), in_map(0)))
        args.append(v_first)
    in_specs += [pl.BlockSpec((768, LANES), lambda s: (0, stage(s)[0])),
                 pl.BlockSpec((PV_ROWS, LANES), lambda s: (0, stage(s)[0]))]
    args += [w2s, pvec]
    yr_spec = pl.BlockSpec((tt, LANES), lambda s: (chain(s)[1], chain(s)[0]))
    out_shape = jax.ShapeDtypeStruct((t, c), F32)
    if has_vres:
        out_specs, out_shapes = [yr_spec], [out_shape]
    else:
        out_specs, out_shapes = [yr_spec, pl.BlockSpec((tt, LANES), in_map(0))], [out_shape, out_shape]
    for src, rows, first, count in casts:
        blk = (rows, src.shape[1])
        in_specs.append(pl.BlockSpec(blk, lambda s, first=first, count=count: (first + jnp.minimum(s, count - 1), 0)))
        args.append(src)
        out_specs.append(pl.BlockSpec(blk, lambda s, count=count: (jnp.minimum(s, count - 1), 0)))
        out_shapes.append(jax.ShapeDtypeStruct((rows * count, src.shape[1]), BF16))
    return pl.pallas_call(
        functools.partial(_rwkv_kernel, tt=tt, ntile=ntile, nsteps=nsteps, has_vres=has_vres, n_cast=len(casts)),
        grid=(nsteps,),
        in_specs=in_specs,
        out_specs=out_specs,
        out_shape=out_shapes,
        scratch_shapes=[pltpu.VMEM((LANES, LANES), F32),
                        pltpu.VMEM((SUBLANES, LANES), F32),
                        pltpu.VMEM((SUBLANES, LR_PAD), F32),
                        pltpu.VMEM((nch, 2 * LANES, LANES), BF16),
                        pltpu.VMEM((nch, LANES, LANES), F32),
                        pltpu.VMEM((nch, LANES, LANES), F32),
                        pltpu.VMEM((2, tt, LANES), F32),
                        pltpu.VMEM((SUBLANES, LANES), F32)],
        compiler_params=_cparams(("arbitrary",)),
        name="rwkv7_vres" if has_vres else "rwkv7",
    )(*args)


def _mix_kernel(*refs, tm, has_router):
    if has_router:
        (x_ref, pp_ref, halo_ref, glp_ref, glr_ref, yr_ref, pw_ref, ps_ref, po_ref, ro_ref, mo_ref,
         g1_ref, sc2_ref, sh2_ref, rt_ref, xo_ref, h2_ref, idx_ref, gt_ref, ext) = refs
    else:
        (x_ref, pp_ref, halo_ref, glp_ref, glr_ref, yr_ref, pw_ref, ps_ref, po_ref, ro_ref, mo_ref,
         g1_ref, sc2_ref, sh2_ref, xo_ref, h2_ref, ext) = refs
    i = pl.program_id(0)
    ext[0:POOL_HALO, :] = jnp.where(i == 0, 0.0, halo_ref[...])
    ext[POOL_HALO:POOL_HALO + tm, :] = pp_ref[...]
    pos = i * tm + lax.broadcasted_iota(jnp.int32, (tm, 1), 0)
    gw = ext.shape[1] // len(POOL_WINDOWS)
    parts = []
    for gi, win in enumerate(POOL_WINDOWS):
        cs = slice(gi * gw, (gi + 1) * gw)
        cur = ext[POOL_HALO:POOL_HALO + tm, cs]
        s = cur
        for j in range(1, win):
            s = s + ext[POOL_HALO - j:POOL_HALO - j + tm, cs]
        cnt = jnp.minimum(pos + 1, win).astype(F32)
        pooled = s / cnt - cur
        parts.append(_mm(pooled, pw_ref[gi]))
    mixed = jnp.concatenate(parts, axis=-1) * ps_ref[...]
    y_pool = _mm(mixed, po_ref[...])
    y_rwkv = _mm(yr_ref[...], ro_ref[...])
    merged = jax.nn.sigmoid(glp_ref[...]) * y_pool + jax.nn.sigmoid(glr_ref[...]) * y_rwkv
    xn = x_ref[...] + g1_ref[...] * _mm(merged, mo_ref[...])
    xo_ref[...] = xn
    ms = jnp.mean(xn * xn, axis=-1, keepdims=True)
    h2 = xn * lax.rsqrt(ms + NORM_EPS) * (1.0 + sc2_ref[...]) + sh2_ref[...]
    if has_router:
        slab = h2.shape[1] // LANES
        for a in range(slab):
            h2_ref[pl.ds(a, tm, stride=slab), :] = h2[:, a * LANES:(a + 1) * LANES]
        rt = rt_ref[...]
        hi, lo = _split(h2)
        rhi, rlo = _split(rt)
        logits = (jnp.dot(hi, rhi, preferred_element_type=F32) + jnp.dot(lo, rhi, preferred_element_type=F32)
                  + jnp.dot(hi, rlo, preferred_element_type=F32))
        n_exp = 8
        lane = lax.broadcasted_iota(jnp.int32, logits.shape, 1)
        lane_f = lane.astype(F32)
        big = float(LANES)
        lg = jnp.where(lane < n_exp, logits, -jnp.inf)
        m1 = jnp.max(lg, axis=-1, keepdims=True)
        i1 = jnp.min(jnp.where(lg == m1, lane_f, big), axis=-1, keepdims=True)
        lg2 = jnp.where(lane_f == i1, -jnp.inf, lg)
        m2 = jnp.max(lg2, axis=-1, keepdims=True)
        i2 = jnp.min(jnp.where(lg2 == m2, lane_f, big), axis=-1, keepdims=True)
        e2 = jnp.exp(m2 - m1)
        p1 = 1.0 / (1.0 + e2)
        p2 = e2 / (1.0 + e2)
        idx_ref[...] = jnp.where(lane == 0, i1, jnp.where(lane == 1, i2, 0.0)).astype(jnp.int32)
        gt_ref[...] = jnp.where(lane == 0, p1, jnp.where(lane == 1, p2, 0.0))
    else:
        h2_ref[...] = h2.astype(h2_ref.dtype)


def _mix(x, g_main, yr, pool_w, pool_scale, pool_out, rk_out, mix_out, gate1, scale2, shift2, router_pad):
    t, d = x.shape
    c = yr.shape[1]
    tm = 256
    has_router = router_pad is not None
    halo_blocks = tm // POOL_HALO
    cb_glp = (4 * c) // d
    in_specs = [pl.BlockSpec((tm, d), lambda i: (i, 0)),
                pl.BlockSpec((tm, c), lambda i: (i, 0)),
                pl.BlockSpec((POOL_HALO, c), lambda i: (jnp.maximum(i * halo_blocks - 1, 0), 0)),
                pl.BlockSpec((tm, d), lambda i: (i, cb_glp)),
                pl.BlockSpec((tm, d), lambda i: (i, cb_glp + 1)),
                pl.BlockSpec((tm, c), lambda i: (i, 0)),
                pl.BlockSpec(pool_w.shape, lambda i: (0, 0, 0)),
                pl.BlockSpec((1, c), lambda i: (0, 0)),
                pl.BlockSpec((c, d), lambda i: (0, 0)),
                pl.BlockSpec((c, d), lambda i: (0, 0)),
                pl.BlockSpec((d, d), lambda i: (0, 0)),
                pl.BlockSpec((1, d), lambda i: (0, 0)),
                pl.BlockSpec((1, d), lambda i: (0, 0)),
                pl.BlockSpec((1, d), lambda i: (0, 0))]
    args = [x, g_main, g_main, g_main, g_main, yr, pool_w, pool_scale, pool_out, rk_out, mix_out,
            gate1, scale2, shift2]
    out_specs = [pl.BlockSpec((tm, d), lambda i: (i, 0))]
    out_shape = [jax.ShapeDtypeStruct((t, d), F32)]
    if has_router:
        slab = d // LANES
        out_specs.append(pl.BlockSpec((tm * slab, LANES), lambda i: (i, 0)))
        out_shape.append(jax.ShapeDtypeStruct((t * slab, LANES), F32))
    else:
        out_specs.append(pl.BlockSpec((tm, d), lambda i: (i, 0)))
        out_shape.append(jax.ShapeDtypeStruct((t, d), BF16))
    if has_router:
        in_specs.append(pl.BlockSpec((d, LANES), lambda i: (0, 0)))
        args.append(router_pad)
        out_specs += [pl.BlockSpec((tm, LANES), lambda i: (i, 0)), pl.BlockSpec((tm, LANES), lambda i: (i, 0))]
        out_shape += [jax.ShapeDtypeStruct((t, LANES), jnp.int32), jax.ShapeDtypeStruct((t, LANES), F32)]
    return pl.pallas_call(
        functools.partial(_mix_kernel, tm=tm, has_router=has_router),
        grid=(t // tm,),
        in_specs=in_specs,
        out_specs=out_specs,
        out_shape=out_shape,
        scratch_shapes=[pltpu.VMEM((POOL_HALO + tm, c), F32)],
        compiler_params=_cparams(("arbitrary",)),
        name="mix_router" if has_router else "mix",
    )(*args)


def _ffn_kernel(h_ref, x_ref, g_ref, w1_ref, w3_ref, w2_ref, o_ref, acc):
    f = pl.program_id(1)

    @pl.when(f == 0)
    def _():
        acc[...] = jnp.zeros_like(acc)

    h = h_ref[...]
    a = jnp.dot(h, w1_ref[...].astype(BF16), preferred_element_type=F32)
    b = jnp.dot(h, w3_ref[...].astype(BF16), preferred_element_type=F32)
    u = (a * jax.nn.sigmoid(a) * b).astype(BF16)
    acc[...] += jnp.dot(u, w2_ref[...].astype(BF16), preferred_element_type=F32)

    @pl.when(f == pl.num_programs(1) - 1)
    def _():
        o_ref[...] = x_ref[...] + g_ref[...] * acc[...]


def _ffn(h2, x, gate2, w1, w3, w2):
    t, d = x.shape
    ff = w1.shape[1]
    tm, tf = 512, 512
    return pl.pallas_call(
        _ffn_kernel,
        grid=(t // tm, ff // tf),
        in_specs=[pl.BlockSpec((tm, d), lambda i, f: (i, 0)),
                  pl.BlockSpec((tm, d), lambda i, f: (i, 0)),
                  pl.BlockSpec((1, d), lambda i, f: (0, 0)),
                  pl.BlockSpec((d, tf), lambda i, f: (0, f)),
                  pl.BlockSpec((d, tf), lambda i, f: (0, f)),
                  pl.BlockSpec((tf, d), lambda i, f: (f, 0))],
        out_specs=pl.BlockSpec((tm, d), lambda i, f: (i, 0)),
        out_shape=jax.ShapeDtypeStruct((t, d), F32),
        scratch_shapes=[pltpu.VMEM((tm, d), F32)],
        compiler_params=_cparams(("arbitrary", "arbitrary")),
        name="ffn_dense",
    )(h2, x, gate2, w1, w3, w2)


def _slab_copy(src_hbm, src_row, dst, slot, dst_row, slab, sem):
    src = src_hbm.at[pl.ds(pl.multiple_of(src_row * slab, slab), slab), :]
    return pltpu.make_async_copy(src, dst.at[slot, pl.ds(pl.multiple_of(dst_row * slab, slab), slab), :],
                                 sem.at[slot])


def _moe_kernel(te_ref, tr_ref, na_ref, tok_ref, h_hbm, w1l_ref, w3l_ref, w2l_ref, w1h_ref, w3h_ref, w2h_ref,
                o_ref, xg, xb, acc, sem, *, tm, n_lo):
    b = pl.program_id(0)
    f = pl.program_id(1)
    nf = pl.num_programs(1)
    n_act = na_ref[0]
    slot = b % 2
    slab = xb.shape[1] // LANES

    def start_gather(tile, slt):
        def body(s, carry):
            _slab_copy(h_hbm, tok_ref[tile * tm + s], xg, slt, s, slab, sem).start()
            return carry
        lax.fori_loop(0, tm, body, 0, unroll=8)

    @pl.when(jnp.logical_and(f == 0, b < n_act))
    def _():
        @pl.when(b == 0)
        def _():
            start_gather(0, 0)

        @pl.when(b + 1 < n_act)
        def _():
            start_gather(b + 1, 1 - slot)

        pltpu.make_async_copy(h_hbm.at[pl.ds(0, tm * slab), :], xg.at[slot], sem.at[slot]).wait()
        for a in range(slab):
            xb[:, a * LANES:(a + 1) * LANES] = xg[slot, pl.ds(a, tm, stride=slab), :].astype(BF16)
        acc[...] = jnp.zeros_like(acc)

    def swiglu_step(w1_ref, w3_ref, w2_ref, rows):
        x = xb[0:rows, :]
        a_ = jnp.dot(x, w1_ref[0].astype(BF16), preferred_element_type=F32)
        b_ = jnp.dot(x, w3_ref[0].astype(BF16), preferred_element_type=F32)
        u = (a_ * jax.nn.sigmoid(a_) * b_).astype(BF16)
        acc[0:rows, :] += jnp.dot(u, w2_ref[0].astype(BF16), preferred_element_type=F32)

    in_lo = te_ref[b] < n_lo
    for rows in range(MOE_ROW_STEP, tm + 1, MOE_ROW_STEP):
        has_rows = jnp.logical_and(b < n_act, tr_ref[b] == rows)

        @pl.when(jnp.logical_and(has_rows, in_lo))
        def _():
            swiglu_step(w1l_ref, w3l_ref, w2l_ref, rows)

        @pl.when(jnp.logical_and(has_rows, jnp.logical_not(in_lo)))
        def _():
            swiglu_step(w1h_ref, w3h_ref, w2h_ref, rows)

    @pl.when(f == nf - 1)
    def _():
        res = jnp.where(b < n_act, acc[...], 0.0)
        for a in range(slab):
            o_ref[pl.ds(a, tm, stride=slab), :] = res[:, a * LANES:(a + 1) * LANES]


MOE_ROW_STEP = 128


def _moe(h2_slab, tile_e, tile_rows, n_active, slot_tok, w_lo, w_hi, tm, n_tiles):
    n_lo, d, ff = w_lo[0].shape
    tf = 512
    nf = ff // tf

    def maps(lo):
        def pick(b, f, te, na):
            e = te[b]
            mine = jnp.logical_and(b < na[0], (e < n_lo) if lo else (e >= n_lo))
            e_loc = jnp.minimum(e, n_lo - 1) if lo else jnp.maximum(e - n_lo, 0)
            return e_loc, jnp.where(mine, f, (nf - 1) if lo else 0)

        def w13_map(b, f, te, tr, na, tok):
            e_loc, fb = pick(b, f, te, na)
            return (e_loc, 0, fb)

        def w2_map(b, f, te, tr, na, tok):
            e_loc, fb = pick(b, f, te, na)
            return (e_loc, fb, 0)

        return [pl.BlockSpec((1, d, tf), w13_map), pl.BlockSpec((1, d, tf), w13_map),
                pl.BlockSpec((1, tf, d), w2_map)]

    grid_spec = pltpu.PrefetchScalarGridSpec(
        num_scalar_prefetch=4,
        grid=(n_tiles, nf),
        in_specs=[pl.BlockSpec(memory_space=pl.ANY)] + maps(True) + maps(False),
        out_specs=pl.BlockSpec((tm * (d // LANES), LANES), lambda b, f, te, tr, na, tok: (b, 0)),
        scratch_shapes=[pltpu.VMEM((2, tm * (d // LANES), LANES), F32),
                        pltpu.VMEM((tm, d), BF16),
                        pltpu.VMEM((tm, d), F32),
                        pltpu.SemaphoreType.DMA((2,))],
    )
    return pl.pallas_call(
        functools.partial(_moe_kernel, tm=tm, n_lo=n_lo),
        grid_spec=grid_spec,
        out_shape=jax.ShapeDtypeStruct((n_tiles * tm * (d // LANES), LANES), F32),
        compiler_params=_cparams(("arbitrary", "arbitrary")),
        name="moe_experts",
    )(tile_e, tile_rows, n_active, slot_tok, h2_slab, *w_lo, *w_hi)


def _comb_kernel(dest_ref, yb_hbm, x_ref, gt_ref, g2_ref, fg_ref, o_ref, buf, sem, *, tm):
    i = pl.program_id(0)
    n = pl.num_programs(0)
    slot = i % 2
    nrow = TOP_K * tm
    slab = o_ref.shape[1] // LANES

    def start_gather(tile, slt):
        def body(s, carry):
            _slab_copy(yb_hbm, dest_ref[tile * nrow + s], buf, slt, s, slab, sem).start()
            return carry
        lax.fori_loop(0, nrow, body, 0, unroll=8)

    @pl.when(i == 0)
    def _():
        start_gather(0, 0)

    @pl.when(i + 1 < n)
    def _():
        start_gather(i + 1, 1 - slot)

    pltpu.make_async_copy(yb_hbm.at[pl.ds(0, nrow * slab), :], buf.at[slot], sem.at[slot]).wait()
    gt = gt_ref[...]
    p0 = gt[:, 0:1]
    p1 = gt[:, 1:2]
    cols = []
    for a in range(slab):
        y0 = buf[slot, pl.ds(a, tm, stride=TOP_K * slab), :]
        y1 = buf[slot, pl.ds(slab + a, tm, stride=TOP_K * slab), :]
        cols.append(y0 * p0 + y1 * p1)
    fmoe = jnp.concatenate(cols, axis=-1)
    xn = x_ref[...] + g2_ref[...] * fmoe
    ms = jnp.mean(xn * xn, axis=-1, keepdims=True)
    o_ref[...] = xn * lax.rsqrt(ms + NORM_EPS) * fg_ref[...]


def _combine(dest, yb_slab, x, gates, gate2, final_gain):
    t, d = x.shape
    tm = 256
    grid_spec = pltpu.PrefetchScalarGridSpec(
        num_scalar_prefetch=1,
        grid=(t // tm,),
        in_specs=[pl.BlockSpec(memory_space=pl.ANY),
                  pl.BlockSpec((tm, d), lambda i, dr: (i, 0)),
                  pl.BlockSpec((tm, LANES), lambda i, dr: (i, 0)),
                  pl.BlockSpec((1, d), lambda i, dr: (0, 0)),
                  pl.BlockSpec((1, d), lambda i, dr: (0, 0))],
        out_specs=pl.BlockSpec((tm, d), lambda i, dr: (i, 0)),
        scratch_shapes=[pltpu.VMEM((2, TOP_K * tm * (d // LANES), LANES), F32),
                        pltpu.SemaphoreType.DMA((2,))],
    )
    return pl.pallas_call(
        functools.partial(_comb_kernel, tm=tm),
        grid_spec=grid_spec,
        out_shape=jax.ShapeDtypeStruct((t, d), F32),
        compiler_params=_cparams(("arbitrary",)),
        name="moe_combine",
    )(dest, yb_slab, x, gates, gate2, final_gain)


def _pad_rows(w, rows):
    return jnp.pad(w, ((0, rows - w.shape[0]), (0, 0)))


def _lowrank_weights(l, rk_lerp, rk_w1, rk_a1, rk_g1, rk_vmu, rk_v1):
    mus = [rk_lerp[l, 0], rk_lerp[l, 1], rk_lerp[l, 2]]
    ws = [rk_w1[l], rk_a1[l], rk_g1[l]]
    if l > 0:
        mus.append(rk_vmu[l - 1])
        ws.append(rk_v1[l - 1])
    wa = jnp.concatenate([(1.0 - m)[:, None] * w for m, w in zip(mus, ws)], axis=1)
    wb = jnp.concatenate([m[:, None] * w for m, w in zip(mus, ws)], axis=1)
    wa = jnp.pad(wa, ((0, 0), (0, LR_PAD - wa.shape[1])))
    wb = jnp.pad(wb, ((0, 0), (0, LR_PAD - wb.shape[1])))
    return jnp.concatenate([wa, wb], axis=1).astype(BF16)


def _second_stage_weights(l, rk_w2, rk_a2, rk_g2, rk_v2):
    c = rk_w2.shape[2]
    w2 = _pad_rows(rk_w2[l], 128)
    a2 = jnp.concatenate([jnp.zeros((64, c), F32), rk_a2[l]], axis=0)
    g2 = _pad_rows(rk_g2[l], 256)
    if l > 0:
        v2 = jnp.concatenate([jnp.zeros((160, c), F32), rk_v2[l - 1], jnp.zeros((64, c), F32)], axis=0)
    else:
        v2 = jnp.zeros((256, c), F32)
    return jnp.concatenate([w2, a2, g2, v2], axis=0).astype(BF16)


def _route(idx, tm, n_exp, n_tiles):
    n_assign = idx.size
    flat_e = idx.reshape(-1)
    onehot = (flat_e[:, None] == jnp.arange(n_exp, dtype=jnp.int32)[None, :]).astype(jnp.int32)
    csum = jnp.cumsum(onehot, axis=0)
    rank = jnp.take_along_axis(csum, flat_e[:, None], axis=1)[:, 0] - 1
    counts = csum[-1]
    padded = (counts + tm - 1) // tm * tm
    pad_end = jnp.cumsum(padded)
    pad_start = pad_end - padded
    dest = (pad_start[flat_e] + rank).astype(jnp.int32)
    slot_tok = jnp.zeros((n_tiles * tm,), jnp.int32).at[dest].set(
        jnp.arange(n_assign, dtype=jnp.int32) // TOP_K)
    n_active = (pad_end[-1] // tm).astype(jnp.int32)
    tile_ids = jnp.minimum(jnp.arange(n_tiles, dtype=jnp.int32), jnp.maximum(n_active - 1, 0))
    tile_e = jnp.minimum(jnp.searchsorted(pad_end, tile_ids * tm, side='right'), n_exp - 1).astype(jnp.int32)
    filled = counts[tile_e] - (tile_ids * tm - pad_start[tile_e])
    tile_rows = jnp.clip((filled + MOE_ROW_STEP - 1) // MOE_ROW_STEP * MOE_ROW_STEP, MOE_ROW_STEP, tm)
    return dest, slot_tok, tile_e, tile_rows.astype(jnp.int32), n_active.reshape(1)


def kernel(x, c, ada_w, ada_b, w_in, pool_w, pool_scale, pool_out, rk_shift, rk_lerp, rk_w0, rk_w1, rk_w2, rk_a0, rk_a1, rk_a2, rk_g1, rk_g2, rk_kk, rk_ka, rk_rk, rk_gn_w, rk_gn_b, rk_vmu, rk_v0, rk_v1, rk_v2, rk_out, mix_out, ffn_w1, ffn_w3, ffn_w2, router, moe_w1, moe_w3, moe_w2, final_gain):
    bsz, t, d = x.shape
    depth = ada_w.shape[0]
    cw = rk_w0.shape[1]
    n_exp = router.shape[2]
    assert bsz == 1 and depth == 2 and cw % LANES == 0
    xs = x.reshape(t, d)
    mod = _ada(c.reshape(d, 1), ada_w, ada_b)

    def mod_row(l, i):
        return mod[l, :, i * d:(i + 1) * d]

    v_first = None
    out = None
    for l in range(depth):
        w_lr = _lowrank_weights(l, rk_lerp, rk_w1, rk_a1, rk_g1, rk_vmu, rk_v1)
        if l == 0:
            g_main, g_lr = _in_proj(xs, mod_row(l, 1), mod_row(l, 0), w_in, l, w_lr)
        else:
            g_main, g_lr = _in_proj(xs, mod_row(l, 1), mod_row(l, 0), w_in_next.reshape(1, d, -1), 0, w_lr)
        w2s = _second_stage_weights(l, rk_w2, rk_a2, rk_g2, rk_v2)
        zeros_c = jnp.zeros((cw,), F32)
        rows = [rk_w0[l], rk_a0[l], rk_v0[l - 1] if l > 0 else zeros_c, rk_kk[l], rk_ka[l],
                rk_gn_w[l], rk_gn_b[l], rk_shift[l, 0], rk_shift[l, 1], rk_shift[l, 2],
                rk_rk[l].reshape(cw)]
        pvec = _pad_rows(jnp.stack(rows, axis=0), PV_ROWS)
        steps = _rwkv_work_steps(t, cw)
        if l == 0:
            dense = [ffn_w1[0], ffn_w3[0], ffn_w2[0]]
            experts = [moe_w1[0].reshape(-1, moe_w1.shape[-1]), moe_w3[0].reshape(-1, moe_w3.shape[-1]),
                       moe_w2[0].reshape(-1, moe_w2.shape[-1])]
            casts = ([_cast_job(w, 0, w.shape[0], steps) for w in dense]
                     + [_cast_job(w, 0, w.shape[0] // 2, steps) for w in experts]
                     + [_cast_job(w_in.reshape(-1, w_in.shape[-1]), d, 2 * d, steps)])
            yr, v_first, *cast_out = _rwkv(g_main, g_lr, None, w2s, pvec, has_vres=False, casts=casts)
            dense_bf, experts_lo, w_in_next = cast_out[:3], cast_out[3:6], cast_out[6]
        else:
            casts = [_cast_job(w, w.shape[0] // 2, w.shape[0], steps) for w in experts]
            yr, *experts_hi = _rwkv(g_main, g_lr, v_first, w2s, pvec, has_vres=True, casts=casts)
        is_moe = (l % 2 == 1)
        router_pad = None
        if is_moe:
            router_pad = jnp.pad(router[l // 2], ((0, 0), (0, LANES - n_exp)))
        res = _mix(xs, g_main, yr, pool_w[l].astype(BF16), pool_scale[l].reshape(1, -1),
                   pool_out[l].astype(BF16), rk_out[l].astype(BF16), mix_out[l].astype(BF16),
                   mod_row(l, 2), mod_row(l, 4), mod_row(l, 3), router_pad)
        if not is_moe:
            x_new, h2 = res
            xs = _ffn(h2, x_new, mod_row(l, 5), *dense_bf)
        else:
            x_new, h2, idx_pad, gate_pad = res
            tm_moe = 512
            n_tiles = (t * TOP_K) // tm_moe + n_exp
            dest, slot_tok, tile_e, tile_rows, n_active = _route(idx_pad[:, :TOP_K], tm_moe, n_exp, n_tiles)
            n_lo = n_exp // 2
            w_lo = [w.reshape((n_lo,) + src.shape[2:]) for w, src in zip(experts_lo, (moe_w1, moe_w3, moe_w2))]
            w_hi = [w.reshape((n_exp - n_lo,) + src.shape[2:])
                    for w, src in zip(experts_hi, (moe_w1, moe_w3, moe_w2))]
            yb_slab = _moe(h2, tile_e, tile_rows, n_active, slot_tok, w_lo, w_hi, tm_moe, n_tiles)
            out = _combine(dest, yb_slab, x_new, gate_pad, mod_row(l, 5), final_gain.reshape(1, d))
    return out.reshape(bsz, t, d)
```

```python
import functools

import jax
import jax.numpy as jnp
from jax import lax
from jax.experimental import pallas as pl
from jax.experimental.pallas import tpu as pltpu

F32 = jnp.float32
BF16 = jnp.bfloat16

NORM_EPS = 1e-6
GROUPNORM_EPS = 64e-5
L2_EPS = 1e-12
POOL_WINDOWS = (2, 4, 8, 16)
HEAD_DIM = 64
TOP_K = 2
LANES = 128
SUBLANES = 8
CHUNK = 64
LR_PAD = 384
POOL_HALO = 16
VMEM_LIMIT = 56 * 1024 * 1024


def _cparams(sem):
    return pltpu.CompilerParams(dimension_semantics=sem, vmem_limit_bytes=VMEM_LIMIT)


def _mm(a, b):
    return jnp.dot(a.astype(BF16), b.astype(BF16), preferred_element_type=F32)


def _mm_nt(a, b):
    return lax.dot_general(a.astype(BF16), b.astype(BF16), (((1,), (1,)), ((), ())),
                           preferred_element_type=F32)


def _split(x):
    hi = x.astype(BF16)
    lo = (x - hi.astype(F32)).astype(BF16)
    return hi, lo


def _ada_kernel(c_ref, w_ref, b_ref, o_ref):
    c = c_ref[...]
    act = c * jax.nn.sigmoid(c)
    o_ref[0] = jnp.sum(act * w_ref[0], axis=0, keepdims=True) + b_ref[0]


def _ada(c_col, ada_w, ada_b):
    depth, d, n = ada_w.shape
    tn = 1024
    return pl.pallas_call(
        _ada_kernel,
        grid=(depth, n // tn),
        in_specs=[pl.BlockSpec((d, 1), lambda l, j: (0, 0)),
                  pl.BlockSpec((1, d, tn), lambda l, j: (l, 0, j)),
                  pl.BlockSpec((1, 1, tn), lambda l, j: (l, 0, j))],
        out_specs=pl.BlockSpec((1, 1, tn), lambda l, j: (l, 0, j)),
        out_shape=jax.ShapeDtypeStruct((depth, 1, n), F32),
        compiler_params=_cparams(("arbitrary", "arbitrary")),
        name="ada_gemv",
    )(c_col, ada_w, ada_b.reshape(depth, 1, n))


def _in_kernel(x_ref, sc_ref, sh_ref, w_ref, wlr_ref, o_ref, olr_ref, h_sc):
    @pl.when(pl.program_id(1) == 0)
    def _():
        x = x_ref[...]
        ms = jnp.mean(x * x, axis=-1, keepdims=True)
        h = x * lax.rsqrt(ms + NORM_EPS) * (1.0 + sc_ref[...]) + sh_ref[...]
        hb = h.astype(BF16)
        h_sc[...] = hb
        olr_ref[...] = jnp.dot(hb, wlr_ref[...], preferred_element_type=F32)

    o_ref[...] = jnp.dot(h_sc[...], w_ref[0].astype(BF16), preferred_element_type=F32)


def _in_proj(x, scale, shift, w_in, layer, w_lr):
    t, d = x.shape
    n = w_in.shape[2]
    nlr = w_lr.shape[1]
    tm, tn = 1024, 512
    return pl.pallas_call(
        _in_kernel,
        grid=(t // tm, n // tn),
        in_specs=[pl.BlockSpec((tm, d), lambda i, j: (i, 0)),
                  pl.BlockSpec((1, d), lambda i, j: (0, 0)),
                  pl.BlockSpec((1, d), lambda i, j: (0, 0)),
                  pl.BlockSpec((1, d, tn), lambda i, j: (layer, 0, j)),
                  pl.BlockSpec((d, nlr), lambda i, j: (0, 0))],
        out_specs=[pl.BlockSpec((tm, tn), lambda i, j: (i, j)),
                   pl.BlockSpec((tm, nlr), lambda i, j: (i, 0))],
        out_shape=[jax.ShapeDtypeStruct((t, n), F32), jax.ShapeDtypeStruct((t, nlr), F32)],
        scratch_shapes=[pltpu.VMEM((tm, d), BF16)],
        compiler_params=_cparams(("arbitrary", "arbitrary")),
        name="in_proj",
    )(x, scale, shift, w_in, w_lr)


PV_W0, PV_A0, PV_V0, PV_KK, PV_KA, PV_GNW, PV_GNB, PV_MUR, PV_MUK, PV_MUV, PV_RK = range(11)
PV_ROWS = 16


def _rwkv_kernel(*refs, tt, ntile, nsteps, has_vres, n_cast):
    n_in = 7 if has_vres else 6
    n_out = 1 if has_vres else 2
    ins = refs[:n_in]
    cast_src = refs[n_in:n_in + n_cast]
    outs = refs[n_in + n_cast:n_in + n_cast + n_out]
    cast_dst = refs[n_in + n_cast + n_out:n_in + 2 * n_cast + n_out]
    (s_sc, c_rkv, c_lr, qm_sc, yin_sc, cc_sc, bg_sc, gn_sc,
     opf_sc, opb_sc, el_sc, bgp_sc, gnp_sc) = refs[n_in + 2 * n_cast + n_out:]
    if has_vres:
        (r_ref, k_ref, v_ref, lr_ref, vf_ref, w2_ref, pv_ref), (yr_ref,) = ins, outs
    else:
        (r_ref, k_ref, v_ref, lr_ref, w2_ref, pv_ref), (yr_ref, vfo_ref) = ins, outs
    step = pl.program_id(0)
    n_work = nsteps - 2
    is_drain = step >= n_work
    prep_tile = jnp.minimum(step, n_work - 1) % ntile
    chain_tile = jnp.maximum(step - 2, 0) % ntile
    wr = step % 2
    rd = 1 - wr
    chunks = range(tt // CHUNK)

    @pl.when(step == 0)
    def _():
        for ref in (s_sc, qm_sc, yin_sc, cc_sc, bg_sc, gn_sc, opf_sc, opb_sc, el_sc, bgp_sc, gnp_sc):
            ref[...] = jnp.zeros_like(ref)

    @pl.when(prep_tile == 0)
    def _():
        c_rkv[...] = jnp.zeros_like(c_rkv)
        c_lr[...] = jnp.zeros_like(c_lr)

    li = lax.broadcasted_iota(jnp.int32, (LANES, LANES), 0)
    lj = lax.broadcasted_iota(jnp.int32, (LANES, LANES), 1)
    same_head = ((li >> 6) == (lj >> 6))
    head_ones = jnp.where(same_head, 1.0, 0.0).astype(BF16)
    strict = (li & (CHUNK - 1)) > (lj & (CHUNK - 1))
    incl = (li & (CHUNK - 1)) >= (lj & (CHUNK - 1))
    eye = li == lj
    eye_f = jnp.where(eye, 1.0, 0.0)
    off_masks = [((li >> (s + 1)) == (lj >> (s + 1))) & (((li >> s) & 1) == 1) & (((lj >> s) & 1) == 0)
                 for s in range(CHUNK.bit_length() - 1)]

    def head_sum(x):
        return _mm(x, head_ones)

    OP_RH, OP_AH = 0, 1
    OP_BT, OP_KT, OP_VV, OP_BBT, OP_KBT = range(5)
    tile = {}

    def prep_tile_wide():
        row = lax.broadcasted_iota(jnp.int32, (tt, 1), 0)

        def shift(x, carry_row):
            return jnp.where(row == 0, carry_row, pltpu.roll(x, 1, axis=0))

        pv = pv_ref[...]

        def prow(i):
            return pv[i:i + 1, :]

        r_raw, k_raw, v_raw = r_ref[...], k_ref[...], v_ref[...]
        zb = lr_ref[:, LR_PAD:2 * LR_PAD]
        c_rkv_used = jnp.where(is_drain, c_rkv[4:7, :], c_rkv[0:3, :])
        c_z = jnp.where(is_drain, c_lr[1:2, :], c_lr[0:1, :])
        r_prev = shift(r_raw, c_rkv_used[0:1, :])
        k_prev = shift(k_raw, c_rkv_used[1:2, :])
        v_prev = shift(v_raw, c_rkv_used[2:3, :])
        z = lr_ref[:, 0:LR_PAD] + shift(zb, c_z)
        c_rkv[4:7, :] = c_rkv_used
        c_lr[1:2, :] = c_z
        c_rkv[0:1, :] = r_raw[tt - 1:tt, :]
        c_rkv[1:2, :] = k_raw[tt - 1:tt, :]
        c_rkv[2:3, :] = v_raw[tt - 1:tt, :]
        c_lr[0:1, :] = zb[tt - 1:tt, :]

        r = r_raw + (r_prev - r_raw) * prow(PV_MUR)
        k = k_raw + (k_prev - k_raw) * prow(PV_MUK)
        v = v_raw + (v_prev - v_raw) * prow(PV_MUV)

        z_wa = z[:, 0:LANES]
        z_gv = z[:, LANES:LR_PAD]
        w_pre = prow(PV_W0) + _mm(jnp.tanh(z_wa), w2_ref[0:128, :])
        neg = -w_pre
        softplus = jnp.maximum(neg, 0.0) + jnp.log(1.0 + jnp.exp(-jnp.abs(neg)))
        logd = -jnp.exp(-softplus - 0.5)
        a_i = jax.nn.sigmoid(prow(PV_A0) + _mm(z_wa, w2_ref[128:256, :]))
        g = _mm(jax.nn.sigmoid(z_gv), w2_ref[256:512, :])
        if has_vres:
            v = v + (vf_ref[...] - v) * jax.nn.sigmoid(prow(PV_V0) + _mm(z_gv, w2_ref[512:768, :]))
        else:
            vfo_ref[...] = v

        kk = k * prow(PV_KK)
        kk = kk / jnp.maximum(jnp.sqrt(head_sum(kk * kk)), L2_EPS)
        k2 = k * (1.0 + (a_i - 1.0) * prow(PV_KA))
        tile.update(r=r, v=v, k2=k2, a_vec=-kk, b_vec=kk * a_i, logd=logd, g=g,
                    bonus=head_sum(r * k2 * prow(PV_RK)) * v, gnw=prow(PV_GNW), gnb=prow(PV_GNB))

    ci = lax.broadcasted_iota(jnp.int32, (CHUNK, CHUNK), 0)
    cj = lax.broadcasted_iota(jnp.int32, (CHUNK, CHUNK), 1)
    tri_incl = jnp.where(ci >= cj, 1.0, 0.0).astype(BF16)
    head0 = lax.broadcasted_iota(jnp.int32, (CHUNK, LANES), 1) < HEAD_DIM

    def stack(x):
        return jnp.concatenate([jnp.where(head0, x, 0.0), jnp.where(head0, 0.0, x)], axis=0)

    def prep(c):
        def run():
            sl = slice(c * CHUNK, (c + 1) * CHUNK)
            ld = tile['logd'][sl]
            ld_hi, ld_lo = _split(ld)
            lp = (jnp.dot(tri_incl, ld_hi, preferred_element_type=F32)
                  + jnp.dot(tri_incl, ld_lo, preferred_element_type=F32))
            le = lp[CHUNK - 1:CHUNK, :]
            e_p = jnp.exp(lp)
            e_n = jnp.exp(-lp)
            e_e = jnp.exp(le - lp)
            b_c, k_c = tile['b_vec'][sl], tile['k2'][sl]
            opf_sc[wr, OP_RH, c] = stack(tile['r'][sl] * e_p)
            opf_sc[wr, OP_AH, c] = stack(tile['a_vec'][sl] * jnp.exp(lp - ld))
            opb_sc[wr, OP_BT, c] = stack(b_c * e_n).astype(BF16)
            opb_sc[wr, OP_KT, c] = stack(k_c * e_n).astype(BF16)
            opb_sc[wr, OP_VV, c] = stack(tile['v'][sl]).astype(BF16)
            opb_sc[wr, OP_BBT, c] = stack(b_c * e_e).T.astype(BF16)
            opb_sc[wr, OP_KBT, c] = stack(k_c * e_e).T.astype(BF16)
            el_sc[wr, c] = jnp.broadcast_to(jnp.exp(le), (SUBLANES, LANES))
        return run

    def substages():
        ops = [dict(c=c) for c in chunks]

        def gram():
            for o in ops:
                c = o['c']
                ar = jnp.concatenate([opf_sc[rd, OP_AH, c], opf_sc[rd, OP_RH, c]], axis=0).astype(BF16)
                gb = _mm_nt(ar, opb_sc[rd, OP_BT, c])
                gk = _mm_nt(ar, opb_sc[rd, OP_KT, c])
                o['a_ab'] = jnp.where(strict, gb[0:LANES], 0.0)
                o['a_rb'] = jnp.where(incl, gb[LANES:], 0.0).astype(BF16)
                o['a_ak'] = jnp.where(strict, gk[0:LANES], 0.0).astype(BF16)
                o['a_rk'] = jnp.where(incl, gk[LANES:], 0.0).astype(BF16)

        def values():
            for o in ops:
                vv = opb_sc[rd, OP_VV, o['c']]
                o['x'] = _mm(o['a_ak'], vv)
                o['yv'] = _mm(o['a_rk'], vv)
                o['cv'] = _mm(opb_sc[rd, OP_KBT, o['c']], vv)
                o['t'] = eye_f + jnp.where(off_masks[0], o['a_ab'], 0.0)

        def level_a(off):
            def run():
                for o in ops:
                    o['ta'] = _mm(o['t'], jnp.where(off, o['a_ab'], 0.0))
            return run

        def level_b():
            for o in ops:
                o['t'] = o['t'] + _mm(o['ta'], o['t'])

        def apply_inverse():
            for o in ops:
                ah_x = jnp.concatenate([opf_sc[rd, OP_AH, o['c']], o['x']], axis=1)
                o['w'] = _mm(o['t'], ah_x).astype(BF16)

        def emit():
            for o in ops:
                c = o['c']
                qy = _mm(o['a_rb'], o['w'])
                mc = _mm(opb_sc[rd, OP_BBT, c], o['w'])
                diag = jnp.where(eye, el_sc[rd, c][0:1, :], 0.0)
                qm_sc[c] = jnp.concatenate([opf_sc[rd, OP_RH, c] + qy[:, 0:LANES], diag + mc[:, 0:LANES]],
                                           axis=0).astype(BF16)
                yin_sc[c] = qy[:, LANES:] + o['yv']
                cc_sc[c] = mc[:, LANES:] + o['cv']

        steps = [gram, values]
        for off in off_masks[1:]:
            steps += [level_a(off), level_b]
        return steps + [apply_inverse, emit]

    chain = dict(s=jnp.where(chain_tile == 0, 0.0, s_sc[...]), ys=[])

    def link(c):
        def run():
            res = _mm(qm_sc[c], chain['s'])
            y2 = res[0:LANES] + yin_sc[c]
            chain['ys'].append(y2[0:CHUNK] + y2[CHUNK:2 * CHUNK])
            chain['s'] = res[LANES:] + cc_sc[c]
        return run

    def finish_chain():
        s_sc[...] = chain['s']
        y = jnp.concatenate(chain['ys'], axis=0)
        mean = head_sum(y) * (1.0 / HEAD_DIM)
        dlt = y - mean
        var = head_sum(dlt * dlt) * (1.0 / HEAD_DIM)
        yn = dlt * lax.rsqrt(var + GROUPNORM_EPS) * gn_sc[0:1, :] + gn_sc[1:2, :]
        yr_ref[...] = (yn + bg_sc[0]) * bg_sc[1]

    def cast(j):
        def run():
            cast_dst[j][...] = cast_src[j][...].astype(BF16)
        return run

    steps = substages()
    fill = [prep_tile_wide]
    for c in chunks:
        fill += [link(c), prep(c)] + [cast(j) for j in range(n_cast)[c::len(chunks)]]
    fill += [finish_chain]
    gaps = len(steps) - 1
    for i, run in enumerate(steps):
        run()
        for item in fill[i * len(fill) // gaps:(i + 1) * len(fill) // gaps] if i < gaps else ():
            item()
    bg_sc[...] = bgp_sc[...]
    gn_sc[...] = gnp_sc[...]
    bgp_sc[0] = tile['bonus']
    bgp_sc[1] = tile['g']
    gnp_sc[0:1, :] = tile['gnw']
    gnp_sc[1:2, :] = tile['gnb']


RWKV_TILE = 512
RWKV_STAGES = 3


def _rwkv_work_steps(t, c):
    return (c // LANES) * (t // RWKV_TILE)


def _cast_job(src, lo, hi, steps):
    rows = hi - lo
    count = max(n for n in range(1, steps + 1) if rows % n == 0 and (rows // n) % 16 == 0)
    blk = rows // count
    assert lo % blk == 0
    return (src, blk, lo // blk, count)


def _rwkv(g_main, g_lr, v_first, w2s, pvec, has_vres, casts=()):
    t = g_main.shape[0]
    c = pvec.shape[1]
    tt = RWKV_TILE
    ntile = t // tt
    n_work = _rwkv_work_steps(t, c)
    nsteps = n_work + RWKV_STAGES - 1
    col0 = c // LANES
    nch = tt // CHUNK

    def prep_of(s):
        ss = jnp.minimum(s, n_work - 1)
        return ss // ntile, ss % ntile

    def chain_of(s):
        cs = jnp.maximum(s - (RWKV_STAGES - 1), 0)
        return cs // ntile, cs % ntile

    def in_map(col_mult):
        return lambda s: (prep_of(s)[1], col_mult * col0 + prep_of(s)[0])

    in_specs = [pl.BlockSpec((tt, LANES), in_map(1)),
                pl.BlockSpec((tt, LANES), in_map(2)),
                pl.BlockSpec((tt, LANES), in_map(3)),
                pl.BlockSpec((tt, 2 * LR_PAD), lambda s: (prep_of(s)[1], 0))]
    args = [g_main, g_main, g_main, g_lr]
    if has_vres:
        in_specs.append(pl.BlockSpec((tt, LANES), in_map(0)))
        args.append(v_first)
    in_specs += [pl.BlockSpec((768, LANES), lambda s: (0, prep_of(s)[0])),
                 pl.BlockSpec((PV_ROWS, LANES), lambda s: (0, prep_of(s)[0]))]
    args += [w2s, pvec]
    yr_spec = pl.BlockSpec((tt, LANES), lambda s: (chain_of(s)[1], chain_of(s)[0]))
    out_shape = jax.ShapeDtypeStruct((t, c), F32)
    if has_vres:
        out_specs, out_shapes = [yr_spec], [out_shape]
    else:
        out_specs, out_shapes = [yr_spec, pl.BlockSpec((tt, LANES), in_map(0))], [out_shape, out_shape]
    for src, rows, first, count in casts:
        blk = (rows, src.shape[1])
        in_specs.append(pl.BlockSpec(blk, lambda s, first=first, count=count: (first + jnp.minimum(s, count - 1), 0)))
        args.append(src)
        out_specs.append(pl.BlockSpec(blk, lambda s, count=count: (jnp.minimum(s, count - 1), 0)))
        out_shapes.append(jax.ShapeDtypeStruct((rows * count, src.shape[1]), BF16))
    return pl.pallas_call(
        functools.partial(_rwkv_kernel, tt=tt, ntile=ntile, nsteps=nsteps, has_vres=has_vres, n_cast=len(casts)),
        grid=(nsteps,),
        in_specs=in_specs,
        out_specs=out_specs,
        out_shape=out_shapes,
        scratch_shapes=[pltpu.VMEM((LANES, LANES), F32),
                        pltpu.VMEM((SUBLANES, LANES), F32),
                        pltpu.VMEM((SUBLANES, LR_PAD), F32),
                        pltpu.VMEM((nch, 2 * LANES, LANES), BF16),
                        pltpu.VMEM((nch, LANES, LANES), F32),
                        pltpu.VMEM((nch, LANES, LANES), F32),
                        pltpu.VMEM((2, tt, LANES), F32),
                        pltpu.VMEM((SUBLANES, LANES), F32),
                        pltpu.VMEM((2, 2, nch, LANES, LANES), F32),
                        pltpu.VMEM((2, 5, nch, LANES, LANES), BF16),
                        pltpu.VMEM((2, nch, SUBLANES, LANES), F32),
                        pltpu.VMEM((2, tt, LANES), F32),
                        pltpu.VMEM((SUBLANES, LANES), F32)],
        compiler_params=_cparams(("arbitrary",)),
        name="rwkv7_vres" if has_vres else "rwkv7",
    )(*args)


def _mix_kernel(*refs, tm, has_router):
    if has_router:
        (x_ref, pp_ref, halo_ref, glp_ref, glr_ref, yr_ref, pw_ref, ps_ref, po_ref, ro_ref, mo_ref,
         g1_ref, sc2_ref, sh2_ref, rt_ref, xo_ref, h2_ref, idx_ref, gt_ref, ext) = refs
    else:
        (x_ref, pp_ref, halo_ref, glp_ref, glr_ref, yr_ref, pw_ref, ps_ref, po_ref, ro_ref, mo_ref,
         g1_ref, sc2_ref, sh2_ref, xo_ref, h2_ref, ext) = refs
    i = pl.program_id(0)
    ext[0:POOL_HALO, :] = jnp.where(i == 0, 0.0, halo_ref[...])
    ext[POOL_HALO:POOL_HALO + tm, :] = pp_ref[...]
    pos = i * tm + lax.broadcasted_iota(jnp.int32, (tm, 1), 0)
    gw = ext.shape[1] // len(POOL_WINDOWS)
    parts = []
    for gi, win in enumerate(POOL_WINDOWS):
        cs = slice(gi * gw, (gi + 1) * gw)
        cur = ext[POOL_HALO:POOL_HALO + tm, cs]
        s = cur
        for j in range(1, win):
            s = s + ext[POOL_HALO - j:POOL_HALO - j + tm, cs]
        cnt = jnp.minimum(pos + 1, win).astype(F32)
        pooled = s / cnt - cur
        parts.append(_mm(pooled, pw_ref[gi]))
    mixed = jnp.concatenate(parts, axis=-1) * ps_ref[...]
    y_pool = _mm(mixed, po_ref[...])
    y_rwkv = _mm(yr_ref[...], ro_ref[...])
    merged = jax.nn.sigmoid(glp_ref[...]) * y_pool + jax.nn.sigmoid(glr_ref[...]) * y_rwkv
    xn = x_ref[...] + g1_ref[...] * _mm(merged, mo_ref[...])
    xo_ref[...] = xn
    ms = jnp.mean(xn * xn, axis=-1, keepdims=True)
    h2 = xn * lax.rsqrt(ms + NORM_EPS) * (1.0 + sc2_ref[...]) + sh2_ref[...]
    if has_router:
        slab = h2.shape[1] // LANES
        for a in range(slab):
            h2_ref[pl.ds(a, tm, stride=slab), :] = h2[:, a * LANES:(a + 1) * LANES]
        rt = rt_ref[...]
        hi, lo = _split(h2)
        rhi, rlo = _split(rt)
        logits = (jnp.dot(hi, rhi, preferred_element_type=F32) + jnp.dot(lo, rhi, preferred_element_type=F32)
                  + jnp.dot(hi, rlo, preferred_element_type=F32))
        n_exp = 8
        lane = lax.broadcasted_iota(jnp.int32, logits.shape, 1)
        lane_f = lane.astype(F32)
        big = float(LANES)
        lg = jnp.where(lane < n_exp, logits, -jnp.inf)
        m1 = jnp.max(lg, axis=-1, keepdims=True)
        i1 = jnp.min(jnp.where(lg == m1, lane_f, big), axis=-1, keepdims=True)
        lg2 = jnp.where(lane_f == i1, -jnp.inf, lg)
        m2 = jnp.max(lg2, axis=-1, keepdims=True)
        i2 = jnp.min(jnp.where(lg2 == m2, lane_f, big), axis=-1, keepdims=True)
        e2 = jnp.exp(m2 - m1)
        p1 = 1.0 / (1.0 + e2)
        p2 = e2 / (1.0 + e2)
        idx_ref[...] = jnp.where(lane == 0, i1, jnp.where(lane == 1, i2, 0.0)).astype(jnp.int32)
        gt_ref[...] = jnp.where(lane == 0, p1, jnp.where(lane == 1, p2, 0.0))
    else:
        h2_ref[...] = h2.astype(h2_ref.dtype)


def _mix(x, g_main, yr, pool_w, pool_scale, pool_out, rk_out, mix_out, gate1, scale2, shift2, router_pad):
    t, d = x.shape
    c = yr.shape[1]
    tm = 256
    has_router = router_pad is not None
    halo_blocks = tm // POOL_HALO
    cb_glp = (4 * c) // d
    in_specs = [pl.BlockSpec((tm, d), lambda i: (i, 0)),
                pl.BlockSpec((tm, c), lambda i: (i, 0)),
                pl.BlockSpec((POOL_HALO, c), lambda i: (jnp.maximum(i * halo_blocks - 1, 0), 0)),
                pl.BlockSpec((tm, d), lambda i: (i, cb_glp)),
                pl.BlockSpec((tm, d), lambda i: (i, cb_glp + 1)),
                pl.BlockSpec((tm, c), lambda i: (i, 0)),
                pl.BlockSpec(pool_w.shape, lambda i: (0, 0, 0)),
                pl.BlockSpec((1, c), lambda i: (0, 0)),
                pl.BlockSpec((c, d), lambda i: (0, 0)),
                pl.BlockSpec((c, d), lambda i: (0, 0)),
                pl.BlockSpec((d, d), lambda i: (0, 0)),
                pl.BlockSpec((1, d), lambda i: (0, 0)),
                pl.BlockSpec((1, d), lambda i: (0, 0)),
                pl.BlockSpec((1, d), lambda i: (0, 0))]
    args = [x, g_main, g_main, g_main, g_main, yr, pool_w, pool_scale, pool_out, rk_out, mix_out,
            gate1, scale2, shift2]
    out_specs = [pl.BlockSpec((tm, d), lambda i: (i, 0))]
    out_shape = [jax.ShapeDtypeStruct((t, d), F32)]
    if has_router:
        slab = d // LANES
        out_specs.append(pl.BlockSpec((tm * slab, LANES), lambda i: (i, 0)))
        out_shape.append(jax.ShapeDtypeStruct((t * slab, LANES), F32))
    else:
        out_specs.append(pl.BlockSpec((tm, d), lambda i: (i, 0)))
        out_shape.append(jax.ShapeDtypeStruct((t, d), BF16))
    if has_router:
        in_specs.append(pl.BlockSpec((d, LANES), lambda i: (0, 0)))
        args.append(router_pad)
        out_specs += [pl.BlockSpec((tm, LANES), lambda i: (i, 0)), pl.BlockSpec((tm, LANES), lambda i: (i, 0))]
        out_shape += [jax.ShapeDtypeStruct((t, LANES), jnp.int32), jax.ShapeDtypeStruct((t, LANES), F32)]
    return pl.pallas_call(
        functools.partial(_mix_kernel, tm=tm, has_router=has_router),
        grid=(t // tm,),
        in_specs=in_specs,
        out_specs=out_specs,
        out_shape=out_shape,
        scratch_shapes=[pltpu.VMEM((POOL_HALO + tm, c), F32)],
        compiler_params=_cparams(("arbitrary",)),
        name="mix_router" if has_router else "mix",
    )(*args)


def _ffn_kernel(h_ref, x_ref, g_ref, w1_ref, w3_ref, w2_ref, o_ref, acc):
    f = pl.program_id(1)

    @pl.when(f == 0)
    def _():
        acc[...] = jnp.zeros_like(acc)

    h = h_ref[...]
    a = jnp.dot(h, w1_ref[...].astype(BF16), preferred_element_type=F32)
    b = jnp.dot(h, w3_ref[...].astype(BF16), preferred_element_type=F32)
    u = (a * jax.nn.sigmoid(a) * b).astype(BF16)
    acc[...] += jnp.dot(u, w2_ref[...].astype(BF16), preferred_element_type=F32)

    @pl.when(f == pl.num_programs(1) - 1)
    def _():
        o_ref[...] = x_ref[...] + g_ref[...] * acc[...]


def _ffn(h2, x, gate2, w1, w3, w2):
    t, d = x.shape
    ff = w1.shape[1]
    tm, tf = 512, 512
    return pl.pallas_call(
        _ffn_kernel,
        grid=(t // tm, ff // tf),
        in_specs=[pl.BlockSpec((tm, d), lambda i, f: (i, 0)),
                  pl.BlockSpec((tm, d), lambda i, f: (i, 0)),
                  pl.BlockSpec((1, d), lambda i, f: (0, 0)),
                  pl.BlockSpec((d, tf), lambda i, f: (0, f)),
                  pl.BlockSpec((d, tf), lambda i, f: (0, f)),
                  pl.BlockSpec((tf, d), lambda i, f: (f, 0))],
        out_specs=pl.BlockSpec((tm, d), lambda i, f: (i, 0)),
        out_shape=jax.ShapeDtypeStruct((t, d), F32),
        scratch_shapes=[pltpu.VMEM((tm, d), F32)],
        compiler_params=_cparams(("arbitrary", "arbitrary")),
        name="ffn_dense",
    )(h2, x, gate2, w1, w3, w2)


def _slab_copy(src_hbm, src_row, dst, slot, dst_row, slab, sem):
    src = src_hbm.at[pl.ds(pl.multiple_of(src_row * slab, slab), slab), :]
    return pltpu.make_async_copy(src, dst.at[slot, pl.ds(pl.multiple_of(dst_row * slab, slab), slab), :],
                                 sem.at[slot])


def _moe_kernel(te_ref, tr_ref, na_ref, tok_ref, h_hbm, w1l_ref, w3l_ref, w2l_ref, w1h_ref, w3h_ref, w2h_ref,
                o_ref, xg, xb, acc, sem, *, tm, n_lo):
    b = pl.program_id(0)
    f = pl.program_id(1)
    nf = pl.num_programs(1)
    n_act = na_ref[0]
    slot = b % 2
    slab = xb.shape[1] // LANES

    def start_gather(tile, slt):
        def body(s, carry):
            _slab_copy(h_hbm, tok_ref[tile * tm + s], xg, slt, s, slab, sem).start()
            return carry
        lax.fori_loop(0, tm, body, 0, unroll=8)

    @pl.when(jnp.logical_and(f == 0, b < n_act))
    def _():
        @pl.when(b == 0)
        def _():
            start_gather(0, 0)

        @pl.when(b + 1 < n_act)
        def _():
            start_gather(b + 1, 1 - slot)

        pltpu.make_async_copy(h_hbm.at[pl.ds(0, tm * slab), :], xg.at[slot], sem.at[slot]).wait()
        for a in range(slab):
            xb[:, a * LANES:(a + 1) * LANES] = xg[slot, pl.ds(a, tm, stride=slab), :].astype(BF16)
        acc[...] = jnp.zeros_like(acc)

    def swiglu_step(w1_ref, w3_ref, w2_ref, rows):
        x = xb[0:rows, :]
        a_ = jnp.dot(x, w1_ref[0].astype(BF16), preferred_element_type=F32)
        b_ = jnp.dot(x, w3_ref[0].astype(BF16), preferred_element_type=F32)
        u = (a_ * jax.nn.sigmoid(a_) * b_).astype(BF16)
        acc[0:rows, :] += jnp.dot(u, w2_ref[0].astype(BF16), preferred_element_type=F32)

    in_lo = te_ref[b] < n_lo
    for rows in range(MOE_ROW_STEP, tm + 1, MOE_ROW_STEP):
        has_rows = jnp.logical_and(b < n_act, tr_ref[b] == rows)

        @pl.when(jnp.logical_and(has_rows, in_lo))
        def _():
            swiglu_step(w1l_ref, w3l_ref, w2l_ref, rows)

        @pl.when(jnp.logical_and(has_rows, jnp.logical_not(in_lo)))
        def _():
            swiglu_step(w1h_ref, w3h_ref, w2h_ref, rows)

    @pl.when(f == nf - 1)
    def _():
        res = jnp.where(b < n_act, acc[...], 0.0)
        for a in range(slab):
            o_ref[pl.ds(a, tm, stride=slab), :] = res[:, a * LANES:(a + 1) * LANES]


MOE_ROW_STEP = 128


def _moe(h2_slab, tile_e, tile_rows, n_active, slot_tok, w_lo, w_hi, tm, n_tiles):
    n_lo, d, ff = w_lo[0].shape
    tf = 512
    nf = ff // tf

    def maps(lo):
        def pick(b, f, te, na):
            e = te[b]
            mine = jnp.logical_and(b < na[0], (e < n_lo) if lo else (e >= n_lo))
            e_loc = jnp.minimum(e, n_lo - 1) if lo else jnp.maximum(e - n_lo, 0)
            return e_loc, jnp.where(mine, f, (nf - 1) if lo else 0)

        def w13_map(b, f, te, tr, na, tok):
            e_loc, fb = pick(b, f, te, na)
            return (e_loc, 0, fb)

        def w2_map(b, f, te, tr, na, tok):
            e_loc, fb = pick(b, f, te, na)
            return (e_loc, fb, 0)

        return [pl.BlockSpec((1, d, tf), w13_map), pl.BlockSpec((1, d, tf), w13_map),
                pl.BlockSpec((1, tf, d), w2_map)]

    grid_spec = pltpu.PrefetchScalarGridSpec(
        num_scalar_prefetch=4,
        grid=(n_tiles, nf),
        in_specs=[pl.BlockSpec(memory_space=pl.ANY)] + maps(True) + maps(False),
        out_specs=pl.BlockSpec((tm * (d // LANES), LANES), lambda b, f, te, tr, na, tok: (b, 0)),
        scratch_shapes=[pltpu.VMEM((2, tm * (d // LANES), LANES), F32),
                        pltpu.VMEM((tm, d), BF16),
                        pltpu.VMEM((tm, d), F32),
                        pltpu.SemaphoreType.DMA((2,))],
    )
    return pl.pallas_call(
        functools.partial(_moe_kernel, tm=tm, n_lo=n_lo),
        grid_spec=grid_spec,
        out_shape=jax.ShapeDtypeStruct((n_tiles * tm * (d // LANES), LANES), F32),
        compiler_params=_cparams(("arbitrary", "arbitrary")),
        name="moe_experts",
    )(tile_e, tile_rows, n_active, slot_tok, h2_slab, *w_lo, *w_hi)


def _comb_kernel(dest_ref, yb_hbm, x_ref, gt_ref, g2_ref, fg_ref, o_ref, buf, sem, *, tm):
    i = pl.program_id(0)
    n = pl.num_programs(0)
    slot = i % 2
    nrow = TOP_K * tm
    slab = o_ref.shape[1] // LANES

    def start_gather(tile, slt):
        def body(s, carry):
            _slab_copy(yb_hbm, dest_ref[tile * nrow + s], buf, slt, s, slab, sem).start()
            return carry
        lax.fori_loop(0, nrow, body, 0, unroll=8)

    @pl.when(i == 0)
    def _():
        start_gather(0, 0)

    @pl.when(i + 1 < n)
    def _():
        start_gather(i + 1, 1 - slot)

    pltpu.make_async_copy(yb_hbm.at[pl.ds(0, nrow * slab), :], buf.at[slot], sem.at[slot]).wait()
    gt = gt_ref[...]
    p0 = gt[:, 0:1]
    p1 = gt[:, 1:2]
    cols = []
    for a in range(slab):
        y0 = buf[slot, pl.ds(a, tm, stride=TOP_K * slab), :]
        y1 = buf[slot, pl.ds(slab + a, tm, stride=TOP_K * slab), :]
        cols.append(y0 * p0 + y1 * p1)
    fmoe = jnp.concatenate(cols, axis=-1)
    xn = x_ref[...] + g2_ref[...] * fmoe
    ms = jnp.mean(xn * xn, axis=-1, keepdims=True)
    o_ref[...] = xn * lax.rsqrt(ms + NORM_EPS) * fg_ref[...]


def _combine(dest, yb_slab, x, gates, gate2, final_gain):
    t, d = x.shape
    tm = 256
    grid_spec = pltpu.PrefetchScalarGridSpec(
        num_scalar_prefetch=1,
        grid=(t // tm,),
        in_specs=[pl.BlockSpec(memory_space=pl.ANY),
                  pl.BlockSpec((tm, d), lambda i, dr: (i, 0)),
                  pl.BlockSpec((tm, LANES), lambda i, dr: (i, 0)),
                  pl.BlockSpec((1, d), lambda i, dr: (0, 0)),
                  pl.BlockSpec((1, d), lambda i, dr: (0, 0))],
        out_specs=pl.BlockSpec((tm, d), lambda i, dr: (i, 0)),
        scratch_shapes=[pltpu.VMEM((2, TOP_K * tm * (d // LANES), LANES), F32),
                        pltpu.SemaphoreType.DMA((2,))],
    )
    return pl.pallas_call(
        functools.partial(_comb_kernel, tm=tm),
        grid_spec=grid_spec,
        out_shape=jax.ShapeDtypeStruct((t, d), F32),
        compiler_params=_cparams(("arbitrary",)),
        name="moe_combine",
    )(dest, yb_slab, x, gates, gate2, final_gain)


def _pad_rows(w, rows):
    return jnp.pad(w, ((0, rows - w.shape[0]), (0, 0)))


def _lowrank_weights(l, rk_lerp, rk_w1, rk_a1, rk_g1, rk_vmu, rk_v1):
    mus = [rk_lerp[l, 0], rk_lerp[l, 1], rk_lerp[l, 2]]
    ws = [rk_w1[l], rk_a1[l], rk_g1[l]]
    if l > 0:
        mus.append(rk_vmu[l - 1])
        ws.append(rk_v1[l - 1])
    wa = jnp.concatenate([(1.0 - m)[:, None] * w for m, w in zip(mus, ws)], axis=1)
    wb = jnp.concatenate([m[:, None] * w for m, w in zip(mus, ws)], axis=1)
    wa = jnp.pad(wa, ((0, 0), (0, LR_PAD - wa.shape[1])))
    wb = jnp.pad(wb, ((0, 0), (0, LR_PAD - wb.shape[1])))
    return jnp.concatenate([wa, wb], axis=1).astype(BF16)


def _second_stage_weights(l, rk_w2, rk_a2, rk_g2, rk_v2):
    c = rk_w2.shape[2]
    w2 = _pad_rows(rk_w2[l], 128)
    a2 = jnp.concatenate([jnp.zeros((64, c), F32), rk_a2[l]], axis=0)
    g2 = _pad_rows(rk_g2[l], 256)
    if l > 0:
        v2 = jnp.concatenate([jnp.zeros((160, c), F32), rk_v2[l - 1], jnp.zeros((64, c), F32)], axis=0)
    else:
        v2 = jnp.zeros((256, c), F32)
    return jnp.concatenate([w2, a2, g2, v2], axis=0).astype(BF16)


def _route(idx, tm, n_exp, n_tiles):
    n_assign = idx.size
    flat_e = idx.reshape(-1)
    onehot = (flat_e[:, None] == jnp.arange(n_exp, dtype=jnp.int32)[None, :]).astype(jnp.int32)
    csum = jnp.cumsum(onehot, axis=0)
    rank = jnp.take_along_axis(csum, flat_e[:, None], axis=1)[:, 0] - 1
    counts = csum[-1]
    padded = (counts + tm - 1) // tm * tm
    pad_end = jnp.cumsum(padded)
    pad_start = pad_end - padded
    dest = (pad_start[flat_e] + rank).astype(jnp.int32)
    slot_tok = jnp.zeros((n_tiles * tm,), jnp.int32).at[dest].set(
        jnp.arange(n_assign, dtype=jnp.int32) // TOP_K)
    n_active = (pad_end[-1] // tm).astype(jnp.int32)
    tile_ids = jnp.minimum(jnp.arange(n_tiles, dtype=jnp.int32), jnp.maximum(n_active - 1, 0))
    tile_e = jnp.minimum(jnp.searchsorted(pad_end, tile_ids * tm, side='right'), n_exp - 1).astype(jnp.int32)
    filled = counts[tile_e] - (tile_ids * tm - pad_start[tile_e])
    tile_rows = jnp.clip((filled + MOE_ROW_STEP - 1) // MOE_ROW_STEP * MOE_ROW_STEP, MOE_ROW_STEP, tm)
    return dest, slot_tok, tile_e, tile_rows.astype(jnp.int32), n_active.reshape(1)


def kernel(x, c, ada_w, ada_b, w_in, pool_w, pool_scale, pool_out, rk_shift, rk_lerp, rk_w0, rk_w1, rk_w2, rk_a0, rk_a1, rk_a2, rk_g1, rk_g2, rk_kk, rk_ka, rk_rk, rk_gn_w, rk_gn_b, rk_vmu, rk_v0, rk_v1, rk_v2, rk_out, mix_out, ffn_w1, ffn_w3, ffn_w2, router, moe_w1, moe_w3, moe_w2, final_gain):
    bsz, t, d = x.shape
    depth = ada_w.shape[0]
    cw = rk_w0.shape[1]
    n_exp = router.shape[2]
    assert bsz == 1 and depth == 2 and cw % LANES == 0
    xs = x.reshape(t, d)
    mod = _ada(c.reshape(d, 1), ada_w, ada_b)

    def mod_row(l, i):
        return mod[l, :, i * d:(i + 1) * d]

    v_first = None
    out = None
    for l in range(depth):
        w_lr = _lowrank_weights(l, rk_lerp, rk_w1, rk_a1, rk_g1, rk_vmu, rk_v1)
        if l == 0:
            g_main, g_lr = _in_proj(xs, mod_row(l, 1), mod_row(l, 0), w_in, l, w_lr)
        else:
            g_main, g_lr = _in_proj(xs, mod_row(l, 1), mod_row(l, 0), w_in_next.reshape(1, d, -1), 0, w_lr)
        w2s = _second_stage_weights(l, rk_w2, rk_a2, rk_g2, rk_v2)
        zeros_c = jnp.zeros((cw,), F32)
        rows = [rk_w0[l], rk_a0[l], rk_v0[l - 1] if l > 0 else zeros_c, rk_kk[l], rk_ka[l],
                rk_gn_w[l], rk_gn_b[l], rk_shift[l, 0], rk_shift[l, 1], rk_shift[l, 2],
                rk_rk[l].reshape(cw)]
        pvec = _pad_rows(jnp.stack(rows, axis=0), PV_ROWS)
        steps = _rwkv_work_steps(t, cw)
        if l == 0:
            dense = [ffn_w1[0], ffn_w3[0], ffn_w2[0]]
            experts = [moe_w1[0].reshape(-1, moe_w1.shape[-1]), moe_w3[0].reshape(-1, moe_w3.shape[-1]),
                       moe_w2[0].reshape(-1, moe_w2.shape[-1])]
            casts = ([_cast_job(w, 0, w.shape[0], steps) for w in dense]
                     + [_cast_job(w, 0, w.shape[0] // 2, steps) for w in experts]
                     + [_cast_job(w_in.reshape(-1, w_in.shape[-1]), d, 2 * d, steps)])
            yr, v_first, *cast_out = _rwkv(g_main, g_lr, None, w2s, pvec, has_vres=False, casts=casts)
            dense_bf, experts_lo, w_in_next = cast_out[:3], cast_out[3:6], cast_out[6]
        else:
            casts = [_cast_job(w, w.shape[0] // 2, w.shape[0], steps) for w in experts]
            yr, *experts_hi = _rwkv(g_main, g_lr, v_first, w2s, pvec, has_vres=True, casts=casts)
        is_moe = (l % 2 == 1)
        router_pad = None
        if is_moe:
            router_pad = jnp.pad(router[l // 2], ((0, 0), (0, LANES - n_exp)))
        res = _mix(xs, g_main, yr, pool_w[l].astype(BF16), pool_scale[l].reshape(1, -1),
                   pool_out[l].astype(BF16), rk_out[l].astype(BF16), mix_out[l].astype(BF16),
                   mod_row(l, 2), mod_row(l, 4), mod_row(l, 3), router_pad)
        if not is_moe:
            x_new, h2 = res
            xs = _ffn(h2, x_new, mod_row(l, 5), *dense_bf)
        else:
            x_new, h2, idx_pad, gate_pad = res
            tm_moe = 512
            n_tiles = (t * TOP_K) // tm_moe + n_exp
            dest, slot_tok, tile_e, tile_rows, n_active = _route(idx_pad[:, :TOP_K], tm_moe, n_exp, n_tiles)
            n_lo = n_exp // 2
            w_lo = [w.reshape((n_lo,) + src.shape[2:]) for w, src in zip(experts_lo, (moe_w1, moe_w3, moe_w2))]
            w_hi = [w.reshape((n_exp - n_lo,) + src.shape[2:])
                    for w, src in zip(experts_hi, (moe_w1, moe_w3, moe_w2))]
            yb_slab = _moe(h2, tile_e, tile_rows, n_active, slot_tok, w_lo, w_hi, tm_moe, n_tiles)
            out = _combine(dest, yb_slab, x_new, gate_pad, mod_row(l, 5), final_gain.reshape(1, d))
    return out.reshape(bsz, t, d)
```

```python
import functools

import jax
import jax.numpy as jnp
from jax import lax
from jax.experimental import pallas as pl
from jax.experimental.pallas import tpu as pltpu

F32 = jnp.float32
BF16 = jnp.bfloat16

NORM_EPS = 1e-6
GROUPNORM_EPS = 64e-5
L2_EPS = 1e-12
POOL_WINDOWS = (2, 4, 8, 16)
HEAD_DIM = 64
TOP_K = 2
LANES = 128
SUBLANES = 8
CHUNK = 64
LR_PAD = 384
POOL_HALO = 16
VMEM_LIMIT = 56 * 1024 * 1024


def _cparams(sem):
    return pltpu.CompilerParams(dimension_semantics=sem, vmem_limit_bytes=VMEM_LIMIT)


def _mm(a, b):
    return jnp.dot(a.astype(BF16), b.astype(BF16), preferred_element_type=F32)


def _mm_nt(a, b):
    return lax.dot_general(a.astype(BF16), b.astype(BF16), (((1,), (1,)), ((), ())),
                           preferred_element_type=F32)


def _split(x):
    hi = x.astype(BF16)
    lo = (x - hi.astype(F32)).astype(BF16)
    return hi, lo


def _ada_kernel(c_ref, w_ref, b_ref, o_ref):
    c = c_ref[...]
    act = c * jax.nn.sigmoid(c)
    o_ref[0] = jnp.sum(act * w_ref[0], axis=0, keepdims=True) + b_ref[0]


def _ada(c_col, ada_w, ada_b):
    depth, d, n = ada_w.shape
    tn = 1024
    return pl.pallas_call(
        _ada_kernel,
        grid=(depth, n // tn),
        in_specs=[pl.BlockSpec((d, 1), lambda l, j: (0, 0)),
                  pl.BlockSpec((1, d, tn), lambda l, j: (l, 0, j)),
                  pl.BlockSpec((1, 1, tn), lambda l, j: (l, 0, j))],
        out_specs=pl.BlockSpec((1, 1, tn), lambda l, j: (l, 0, j)),
        out_shape=jax.ShapeDtypeStruct((depth, 1, n), F32),
        compiler_params=_cparams(("arbitrary", "arbitrary")),
        name="ada_gemv",
    )(c_col, ada_w, ada_b.reshape(depth, 1, n))


def _in_kernel(x_ref, sc_ref, sh_ref, w_ref, wlr_ref, o_ref, olr_ref, h_sc):
    @pl.when(pl.program_id(1) == 0)
    def _():
        x = x_ref[...]
        ms = jnp.mean(x * x, axis=-1, keepdims=True)
        h = x * lax.rsqrt(ms + NORM_EPS) * (1.0 + sc_ref[...]) + sh_ref[...]
        hb = h.astype(BF16)
        h_sc[...] = hb
        olr_ref[...] = jnp.dot(hb, wlr_ref[...], preferred_element_type=F32)

    o_ref[...] = jnp.dot(h_sc[...], w_ref[0].astype(BF16), preferred_element_type=F32).astype(o_ref.dtype)


def _in_proj(x, scale, shift, w_in, layer, w_lr):
    t, d = x.shape
    n = w_in.shape[2]
    nlr = w_lr.shape[1]
    tm, tn = 1024, 512
    return pl.pallas_call(
        _in_kernel,
        grid=(t // tm, n // tn),
        in_specs=[pl.BlockSpec((tm, d), lambda i, j: (i, 0)),
                  pl.BlockSpec((1, d), lambda i, j: (0, 0)),
                  pl.BlockSpec((1, d), lambda i, j: (0, 0)),
                  pl.BlockSpec((1, d, tn), lambda i, j: (layer, 0, j)),
                  pl.BlockSpec((d, nlr), lambda i, j: (0, 0))],
        out_specs=[pl.BlockSpec((tm, tn), lambda i, j: (i, j)),
                   pl.BlockSpec((tm, nlr), lambda i, j: (i, 0))],
        out_shape=[jax.ShapeDtypeStruct((t, n), BF16), jax.ShapeDtypeStruct((t, nlr), F32)],
        scratch_shapes=[pltpu.VMEM((tm, d), BF16)],
        compiler_params=_cparams(("arbitrary", "arbitrary")),
        name="in_proj",
    )(x, scale, shift, w_in, w_lr)


PV_W0, PV_A0, PV_V0, PV_KK, PV_KA, PV_GNW, PV_GNB, PV_MUR, PV_MUK, PV_MUV, PV_RK = range(11)
PV_ROWS = 16


def _rwkv_kernel(*refs, tt, ntile, nsteps, has_vres, n_cast):
    n_in = 7 if has_vres else 6
    n_out = 1 if has_vres else 2
    ins = refs[:n_in]
    cast_src = refs[n_in:n_in + n_cast]
    outs = refs[n_in + n_cast:n_in + n_cast + n_out]
    cast_dst = refs[n_in + n_cast + n_out:n_in + 2 * n_cast + n_out]
    (s_sc, c_rkv, c_lr, qm_sc, yin_sc, cc_sc, bg_sc, gn_sc,
     opf_sc, opb_sc, el_sc, bgp_sc, gnp_sc) = refs[n_in + 2 * n_cast + n_out:]
    if has_vres:
        (r_ref, k_ref, v_ref, lr_ref, vf_ref, w2_ref, pv_ref), (yr_ref,) = ins, outs
    else:
        (r_ref, k_ref, v_ref, lr_ref, w2_ref, pv_ref), (yr_ref, vfo_ref) = ins, outs
    step = pl.program_id(0)
    n_work = nsteps - 2
    is_drain = step >= n_work
    prep_tile = jnp.minimum(step, n_work - 1) % ntile
    chain_tile = jnp.maximum(step - 2, 0) % ntile
    wr = step % 2
    rd = 1 - wr
    chunks = range(tt // CHUNK)

    @pl.when(step == 0)
    def _():
        for ref in (s_sc, qm_sc, yin_sc, cc_sc, bg_sc, gn_sc, opf_sc, opb_sc, el_sc, bgp_sc, gnp_sc):
            ref[...] = jnp.zeros_like(ref)

    @pl.when(prep_tile == 0)
    def _():
        c_rkv[...] = jnp.zeros_like(c_rkv)
        c_lr[...] = jnp.zeros_like(c_lr)

    li = lax.broadcasted_iota(jnp.int32, (LANES, LANES), 0)
    lj = lax.broadcasted_iota(jnp.int32, (LANES, LANES), 1)
    same_head = ((li >> 6) == (lj >> 6))
    head_ones = jnp.where(same_head, 1.0, 0.0).astype(BF16)
    strict = (li & (CHUNK - 1)) > (lj & (CHUNK - 1))
    incl = (li & (CHUNK - 1)) >= (lj & (CHUNK - 1))
    eye = li == lj
    eye_f = jnp.where(eye, 1.0, 0.0)
    off_masks = [((li >> (s + 1)) == (lj >> (s + 1))) & (((li >> s) & 1) == 1) & (((lj >> s) & 1) == 0)
                 for s in range(CHUNK.bit_length() - 1)]

    def head_sum(x):
        return _mm(x, head_ones)

    OP_RH, OP_AH = 0, 1
    OP_BT, OP_KT, OP_VV, OP_BBT, OP_KBT = range(5)
    tile = {}

    def prep_tile_wide():
        row = lax.broadcasted_iota(jnp.int32, (tt, 1), 0)

        def shift(x, carry_row):
            return jnp.where(row == 0, carry_row, pltpu.roll(x, 1, axis=0))

        pv = pv_ref[...]

        def prow(i):
            return pv[i:i + 1, :]

        r_raw, k_raw, v_raw = (ref[...].astype(F32) for ref in (r_ref, k_ref, v_ref))
        zb = lr_ref[:, LR_PAD:2 * LR_PAD]
        c_rkv_used = jnp.where(is_drain, c_rkv[4:7, :], c_rkv[0:3, :])
        c_z = jnp.where(is_drain, c_lr[1:2, :], c_lr[0:1, :])
        r_prev = shift(r_raw, c_rkv_used[0:1, :])
        k_prev = shift(k_raw, c_rkv_used[1:2, :])
        v_prev = shift(v_raw, c_rkv_used[2:3, :])
        z = lr_ref[:, 0:LR_PAD] + shift(zb, c_z)
        c_rkv[4:7, :] = c_rkv_used
        c_lr[1:2, :] = c_z
        c_rkv[0:1, :] = r_raw[tt - 1:tt, :]
        c_rkv[1:2, :] = k_raw[tt - 1:tt, :]
        c_rkv[2:3, :] = v_raw[tt - 1:tt, :]
        c_lr[0:1, :] = zb[tt - 1:tt, :]

        r = r_raw + (r_prev - r_raw) * prow(PV_MUR)
        k = k_raw + (k_prev - k_raw) * prow(PV_MUK)
        v = v_raw + (v_prev - v_raw) * prow(PV_MUV)

        z_wa = z[:, 0:LANES]
        z_gv = z[:, LANES:LR_PAD]
        w_pre = prow(PV_W0) + _mm(jnp.tanh(z_wa), w2_ref[0:128, :])
        neg = -w_pre
        softplus = jnp.maximum(neg, 0.0) + jnp.log(1.0 + jnp.exp(-jnp.abs(neg)))
        logd = -jnp.exp(-softplus - 0.5)
        a_i = jax.nn.sigmoid(prow(PV_A0) + _mm(z_wa, w2_ref[128:256, :]))
        g = _mm(jax.nn.sigmoid(z_gv), w2_ref[256:512, :])
        if has_vres:
            v = v + (vf_ref[...] - v) * jax.nn.sigmoid(prow(PV_V0) + _mm(z_gv, w2_ref[512:768, :]))
        else:
            vfo_ref[...] = v

        kk = k * prow(PV_KK)
        kk = kk / jnp.maximum(jnp.sqrt(head_sum(kk * kk)), L2_EPS)
        k2 = k * (1.0 + (a_i - 1.0) * prow(PV_KA))
        tile.update(r=r, v=v, k2=k2, a_vec=-kk, b_vec=kk * a_i, logd=logd, g=g,
                    bonus=head_sum(r * k2 * prow(PV_RK)) * v, gnw=prow(PV_GNW), gnb=prow(PV_GNB))

    ci = lax.broadcasted_iota(jnp.int32, (CHUNK, CHUNK), 0)
    cj = lax.broadcasted_iota(jnp.int32, (CHUNK, CHUNK), 1)
    tri_incl = jnp.where(ci >= cj, 1.0, 0.0).astype(BF16)
    head0 = lax.broadcasted_iota(jnp.int32, (CHUNK, LANES), 1) < HEAD_DIM

    def stack(x):
        return jnp.concatenate([jnp.where(head0, x, 0.0), jnp.where(head0, 0.0, x)], axis=0)

    def prep(c):
        def run():
            sl = slice(c * CHUNK, (c + 1) * CHUNK)
            ld = tile['logd'][sl]
            ld_hi, ld_lo = _split(ld)
            lp = (jnp.dot(tri_incl, ld_hi, preferred_element_type=F32)
                  + jnp.dot(tri_incl, ld_lo, preferred_element_type=F32))
            le = lp[CHUNK - 1:CHUNK, :]
            e_p = jnp.exp(lp)
            e_n = jnp.exp(-lp)
            e_e = jnp.exp(le - lp)
            b_c, k_c = tile['b_vec'][sl], tile['k2'][sl]
            opf_sc[wr, OP_RH, c] = stack(tile['r'][sl] * e_p)
            opf_sc[wr, OP_AH, c] = stack(tile['a_vec'][sl] * jnp.exp(lp - ld))
            opb_sc[wr, OP_BT, c] = stack(b_c * e_n).astype(BF16)
            opb_sc[wr, OP_KT, c] = stack(k_c * e_n).astype(BF16)
            opb_sc[wr, OP_VV, c] = stack(tile['v'][sl]).astype(BF16)
            opb_sc[wr, OP_BBT, c] = stack(b_c * e_e).T.astype(BF16)
            opb_sc[wr, OP_KBT, c] = stack(k_c * e_e).T.astype(BF16)
            el_sc[wr, c] = jnp.broadcast_to(jnp.exp(le), (SUBLANES, LANES))
        return run

    def substages():
        ops = [dict(c=c) for c in chunks]

        def gram():
            for o in ops:
                c = o['c']
                ar = jnp.concatenate([opf_sc[rd, OP_AH, c], opf_sc[rd, OP_RH, c]], axis=0).astype(BF16)
                gb = _mm_nt(ar, opb_sc[rd, OP_BT, c])
                gk = _mm_nt(ar, opb_sc[rd, OP_KT, c])
                o['a_ab'] = jnp.where(strict, gb[0:LANES], 0.0)
                o['a_rb'] = jnp.where(incl, gb[LANES:], 0.0).astype(BF16)
                o['a_ak'] = jnp.where(strict, gk[0:LANES], 0.0).astype(BF16)
                o['a_rk'] = jnp.where(incl, gk[LANES:], 0.0).astype(BF16)

        def values():
            for o in ops:
                vv = opb_sc[rd, OP_VV, o['c']]
                o['x'] = _mm(o['a_ak'], vv)
                o['yv'] = _mm(o['a_rk'], vv)
                o['cv'] = _mm(opb_sc[rd, OP_KBT, o['c']], vv)
                o['t'] = eye_f + jnp.where(off_masks[0], o['a_ab'], 0.0)

        def level_a(off):
            def run():
                for o in ops:
                    o['ta'] = _mm(o['t'], jnp.where(off, o['a_ab'], 0.0))
            return run

        def level_b():
            for o in ops:
                o['t'] = o['t'] + _mm(o['ta'], o['t'])

        def apply_inverse():
            for o in ops:
                ah_x = jnp.concatenate([opf_sc[rd, OP_AH, o['c']], o['x']], axis=1)
                o['w'] = _mm(o['t'], ah_x).astype(BF16)

        def emit():
            for o in ops:
                c = o['c']
                qy = _mm(o['a_rb'], o['w'])
                mc = _mm(opb_sc[rd, OP_BBT, c], o['w'])
                diag = jnp.where(eye, el_sc[rd, c][0:1, :], 0.0)
                qm_sc[c] = jnp.concatenate([opf_sc[rd, OP_RH, c] + qy[:, 0:LANES], diag + mc[:, 0:LANES]],
                                           axis=0).astype(BF16)
                yin_sc[c] = qy[:, LANES:] + o['yv']
                cc_sc[c] = mc[:, LANES:] + o['cv']

        steps = [gram, values]
        for off in off_masks[1:]:
            steps += [level_a(off), level_b]
        return steps + [apply_inverse, emit]

    chain = dict(s=jnp.where(chain_tile == 0, 0.0, s_sc[...]), ys=[])

    def link(c):
        def run():
            res = _mm(qm_sc[c], chain['s'])
            y2 = res[0:LANES] + yin_sc[c]
            chain['ys'].append(y2[0:CHUNK] + y2[CHUNK:2 * CHUNK])
            chain['s'] = res[LANES:] + cc_sc[c]
        return run

    def finish_chain():
        s_sc[...] = chain['s']
        y = jnp.concatenate(chain['ys'], axis=0)
        mean = head_sum(y) * (1.0 / HEAD_DIM)
        dlt = y - mean
        var = head_sum(dlt * dlt) * (1.0 / HEAD_DIM)
        yn = dlt * lax.rsqrt(var + GROUPNORM_EPS) * gn_sc[0:1, :] + gn_sc[1:2, :]
        yr_ref[...] = (yn + bg_sc[0]) * bg_sc[1]

    def cast(j):
        def run():
            cast_dst[j][...] = cast_src[j][...].astype(BF16)
        return run

    steps = substages()
    fill = [prep_tile_wide]
    for c in chunks:
        fill += [link(c), prep(c)] + [cast(j) for j in range(n_cast)[c::len(chunks)]]
    fill += [finish_chain]
    gaps = len(steps) - 1
    for i, run in enumerate(steps):
        run()
        for item in fill[i * len(fill) // gaps:(i + 1) * len(fill) // gaps] if i < gaps else ():
            item()
    bg_sc[...] = bgp_sc[...]
    gn_sc[...] = gnp_sc[...]
    bgp_sc[0] = tile['bonus']
    bgp_sc[1] = tile['g']
    gnp_sc[0:1, :] = tile['gnw']
    gnp_sc[1:2, :] = tile['gnb']


RWKV_TILE = 512
RWKV_STAGES = 3


def _rwkv_work_steps(t, c):
    return (c // LANES) * (t // RWKV_TILE)


def _cast_job(src, lo, hi, steps):
    rows = hi - lo
    count = max(n for n in range(1, steps + 1) if rows % n == 0 and (rows // n) % 16 == 0)
    blk = rows // count
    assert lo % blk == 0
    return (src, blk, lo // blk, count)


def _rwkv(g_main, g_lr, v_first, w2s, pvec, has_vres, casts=()):
    t = g_main.shape[0]
    c = pvec.shape[1]
    tt = RWKV_TILE
    ntile = t // tt
    n_work = _rwkv_work_steps(t, c)
    nsteps = n_work + RWKV_STAGES - 1
    col0 = c // LANES
    nch = tt // CHUNK

    def prep_of(s):
        ss = jnp.minimum(s, n_work - 1)
        return ss // ntile, ss % ntile

    def chain_of(s):
        cs = jnp.maximum(s - (RWKV_STAGES - 1), 0)
        return cs // ntile, cs % ntile

    def in_map(col_mult):
        return lambda s: (prep_of(s)[1], col_mult * col0 + prep_of(s)[0])

    in_specs = [pl.BlockSpec((tt, LANES), in_map(1)),
                pl.BlockSpec((tt, LANES), in_map(2)),
                pl.BlockSpec((tt, LANES), in_map(3)),
                pl.BlockSpec((tt, 2 * LR_PAD), lambda s: (prep_of(s)[1], 0))]
    args = [g_main, g_main, g_main, g_lr]
    if has_vres:
        in_specs.append(pl.BlockSpec((tt, LANES), in_map(0)))
        args.append(v_first)
    in_specs += [pl.BlockSpec((768, LANES), lambda s: (0, prep_of(s)[0])),
                 pl.BlockSpec((PV_ROWS, LANES), lambda s: (0, prep_of(s)[0]))]
    args += [w2s, pvec]
    yr_spec = pl.BlockSpec((tt, LANES), lambda s: (chain_of(s)[1], chain_of(s)[0]))
    out_shape = jax.ShapeDtypeStruct((t, c), F32)
    if has_vres:
        out_specs, out_shapes = [yr_spec], [out_shape]
    else:
        out_specs, out_shapes = [yr_spec, pl.BlockSpec((tt, LANES), in_map(0))], [out_shape, out_shape]
    for src, rows, first, count in casts:
        blk = (rows, src.shape[1])
        in_specs.append(pl.BlockSpec(blk, lambda s, first=first, count=count: (first + jnp.minimum(s, count - 1), 0)))
        args.append(src)
        out_specs.append(pl.BlockSpec(blk, lambda s, count=count: (jnp.minimum(s, count - 1), 0)))
        out_shapes.append(jax.ShapeDtypeStruct((rows * count, src.shape[1]), BF16))
    return pl.pallas_call(
        functools.partial(_rwkv_kernel, tt=tt, ntile=ntile, nsteps=nsteps, has_vres=has_vres, n_cast=len(casts)),
        grid=(nsteps,),
        in_specs=in_specs,
        out_specs=out_specs,
        out_shape=out_shapes,
        scratch_shapes=[pltpu.VMEM((LANES, LANES), F32),
                        pltpu.VMEM((SUBLANES, LANES), F32),
                        pltpu.VMEM((SUBLANES, LR_PAD), F32),
                        pltpu.VMEM((nch, 2 * LANES, LANES), BF16),
                        pltpu.VMEM((nch, LANES, LANES), F32),
                        pltpu.VMEM((nch, LANES, LANES), F32),
                        pltpu.VMEM((2, tt, LANES), F32),
                        pltpu.VMEM((SUBLANES, LANES), F32),
                        pltpu.VMEM((2, 2, nch, LANES, LANES), F32),
                        pltpu.VMEM((2, 5, nch, LANES, LANES), BF16),
                        pltpu.VMEM((2, nch, SUBLANES, LANES), F32),
                        pltpu.VMEM((2, tt, LANES), F32),
                        pltpu.VMEM((SUBLANES, LANES), F32)],
        compiler_params=_cparams(("arbitrary",)),
        name="rwkv7_vres" if has_vres else "rwkv7",
    )(*args)


def _mix_kernel(*refs, tm, has_router):
    if has_router:
        (x_ref, pp_ref, halo_ref, glp_ref, glr_ref, yr_ref, pw_ref, ps_ref, po_ref, ro_ref, mo_ref,
         g1_ref, sc2_ref, sh2_ref, rt_ref, xo_ref, h2_ref, idx_ref, gt_ref, ext) = refs
    else:
        (x_ref, pp_ref, halo_ref, glp_ref, glr_ref, yr_ref, pw_ref, ps_ref, po_ref, ro_ref, mo_ref,
         g1_ref, sc2_ref, sh2_ref, xo_ref, h2_ref, ext) = refs
    i = pl.program_id(0)
    ext[0:POOL_HALO, :] = jnp.where(i == 0, 0.0, halo_ref[...].astype(F32))
    ext[POOL_HALO:POOL_HALO + tm, :] = pp_ref[...].astype(F32)
    pos = i * tm + lax.broadcasted_iota(jnp.int32, (tm, 1), 0)
    gw = ext.shape[1] // len(POOL_WINDOWS)
    parts = []
    for gi, win in enumerate(POOL_WINDOWS):
        cs = slice(gi * gw, (gi + 1) * gw)
        cur = ext[POOL_HALO:POOL_HALO + tm, cs]
        s = cur
        for j in range(1, win):
            s = s + ext[POOL_HALO - j:POOL_HALO - j + tm, cs]
        cnt = jnp.minimum(pos + 1, win).astype(F32)
        pooled = s / cnt - cur
        parts.append(_mm(pooled, pw_ref[gi]))
    mixed = jnp.concatenate(parts, axis=-1) * ps_ref[...]
    y_pool = _mm(mixed, po_ref[...])
    y_rwkv = _mm(yr_ref[...], ro_ref[...])
    merged = (jax.nn.sigmoid(glp_ref[...].astype(F32)) * y_pool
              + jax.nn.sigmoid(glr_ref[...].astype(F32)) * y_rwkv)
    xn = x_ref[...] + g1_ref[...] * _mm(merged, mo_ref[...])
    xo_ref[...] = xn
    ms = jnp.mean(xn * xn, axis=-1, keepdims=True)
    h2 = xn * lax.rsqrt(ms + NORM_EPS) * (1.0 + sc2_ref[...]) + sh2_ref[...]
    if has_router:
        slab = h2.shape[1] // LANES
        for a in range(slab):
            h2_ref[pl.ds(a, tm, stride=slab), :] = h2[:, a * LANES:(a + 1) * LANES]
        rt = rt_ref[...]
        hi, lo = _split(h2)
        rhi, rlo = _split(rt)
        logits = (jnp.dot(hi, rhi, preferred_element_type=F32) + jnp.dot(lo, rhi, preferred_element_type=F32)
                  + jnp.dot(hi, rlo, preferred_element_type=F32))
        n_exp = 8
        lane = lax.broadcasted_iota(jnp.int32, logits.shape, 1)
        lane_f = lane.astype(F32)
        big = float(LANES)
        lg = jnp.where(lane < n_exp, logits, -jnp.inf)
        m1 = jnp.max(lg, axis=-1, keepdims=True)
        i1 = jnp.min(jnp.where(lg == m1, lane_f, big), axis=-1, keepdims=True)
        lg2 = jnp.where(lane_f == i1, -jnp.inf, lg)
        m2 = jnp.max(lg2, axis=-1, keepdims=True)
        i2 = jnp.min(jnp.where(lg2 == m2, lane_f, big), axis=-1, keepdims=True)
        e2 = jnp.exp(m2 - m1)
        p1 = 1.0 / (1.0 + e2)
        p2 = e2 / (1.0 + e2)
        idx_ref[...] = jnp.where(lane == 0, i1, jnp.where(lane == 1, i2, 0.0)).astype(jnp.int32)
        gt_ref[...] = jnp.where(lane == 0, p1, jnp.where(lane == 1, p2, 0.0))
    else:
        h2_ref[...] = h2.astype(h2_ref.dtype)


def _mix(x, g_main, yr, pool_w, pool_scale, pool_out, rk_out, mix_out, gate1, scale2, shift2, router_pad):
    t, d = x.shape
    c = yr.shape[1]
    tm = 256
    has_router = router_pad is not None
    halo_blocks = tm // POOL_HALO
    cb_glp = (4 * c) // d
    in_specs = [pl.BlockSpec((tm, d), lambda i: (i, 0)),
                pl.BlockSpec((tm, c), lambda i: (i, 0)),
                pl.BlockSpec((POOL_HALO, c), lambda i: (jnp.maximum(i * halo_blocks - 1, 0), 0)),
                pl.BlockSpec((tm, d), lambda i: (i, cb_glp)),
                pl.BlockSpec((tm, d), lambda i: (i, cb_glp + 1)),
                pl.BlockSpec((tm, c), lambda i: (i, 0)),
                pl.BlockSpec(pool_w.shape, lambda i: (0, 0, 0)),
                pl.BlockSpec((1, c), lambda i: (0, 0)),
                pl.BlockSpec((c, d), lambda i: (0, 0)),
                pl.BlockSpec((c, d), lambda i: (0, 0)),
                pl.BlockSpec((d, d), lambda i: (0, 0)),
                pl.BlockSpec((1, d), lambda i: (0, 0)),
                pl.BlockSpec((1, d), lambda i: (0, 0)),
                pl.BlockSpec((1, d), lambda i: (0, 0))]
    args = [x, g_main, g_main, g_main, g_main, yr, pool_w, pool_scale, pool_out, rk_out, mix_out,
            gate1, scale2, shift2]
    out_specs = [pl.BlockSpec((tm, d), lambda i: (i, 0))]
    out_shape = [jax.ShapeDtypeStruct((t, d), F32)]
    if has_router:
        slab = d // LANES
        out_specs.append(pl.BlockSpec((tm * slab, LANES), lambda i: (i, 0)))
        out_shape.append(jax.ShapeDtypeStruct((t * slab, LANES), F32))
    else:
        out_specs.append(pl.BlockSpec((tm, d), lambda i: (i, 0)))
        out_shape.append(jax.ShapeDtypeStruct((t, d), BF16))
    if has_router:
        in_specs.append(pl.BlockSpec((d, LANES), lambda i: (0, 0)))
        args.append(router_pad)
        out_specs += [pl.BlockSpec((tm, LANES), lambda i: (i, 0)), pl.BlockSpec((tm, LANES), lambda i: (i, 0))]
        out_shape += [jax.ShapeDtypeStruct((t, LANES), jnp.int32), jax.ShapeDtypeStruct((t, LANES), F32)]
    return pl.pallas_call(
        functools.partial(_mix_kernel, tm=tm, has_router=has_router),
        grid=(t // tm,),
        in_specs=in_specs,
        out_specs=out_specs,
        out_shape=out_shape,
        scratch_shapes=[pltpu.VMEM((POOL_HALO + tm, c), F32)],
        compiler_params=_cparams(("arbitrary",)),
        name="mix_router" if has_router else "mix",
    )(*args)


def _ffn_kernel(h_ref, x_ref, g_ref, w1_ref, w3_ref, w2_ref, o_ref, acc):
    f = pl.program_id(1)

    @pl.when(f == 0)
    def _():
        acc[...] = jnp.zeros_like(acc)

    h = h_ref[...]
    a = jnp.dot(h, w1_ref[...].astype(BF16), preferred_element_type=F32)
    b = jnp.dot(h, w3_ref[...].astype(BF16), preferred_element_type=F32)
    u = (a * jax.nn.sigmoid(a) * b).astype(BF16)
    acc[...] += jnp.dot(u, w2_ref[...].astype(BF16), preferred_element_type=F32)

    @pl.when(f == pl.num_programs(1) - 1)
    def _():
        o_ref[...] = x_ref[...] + g_ref[...] * acc[...]


def _ffn(h2, x, gate2, w1, w3, w2):
    t, d = x.shape
    ff = w1.shape[1]
    tm, tf = 512, 512
    return pl.pallas_call(
        _ffn_kernel,
        grid=(t // tm, ff // tf),
        in_specs=[pl.BlockSpec((tm, d), lambda i, f: (i, 0)),
                  pl.BlockSpec((tm, d), lambda i, f: (i, 0)),
                  pl.BlockSpec((1, d), lambda i, f: (0, 0)),
                  pl.BlockSpec((d, tf), lambda i, f: (0, f)),
                  pl.BlockSpec((d, tf), lambda i, f: (0, f)),
                  pl.BlockSpec((tf, d), lambda i, f: (f, 0))],
        out_specs=pl.BlockSpec((tm, d), lambda i, f: (i, 0)),
        out_shape=jax.ShapeDtypeStruct((t, d), F32),
        scratch_shapes=[pltpu.VMEM((tm, d), F32)],
        compiler_params=_cparams(("arbitrary", "arbitrary")),
        name="ffn_dense",
    )(h2, x, gate2, w1, w3, w2)


def _slab_copy(src_hbm, src_row, dst, slot, dst_row, slab, sem):
    src = src_hbm.at[pl.ds(pl.multiple_of(src_row * slab, slab), slab), :]
    return pltpu.make_async_copy(src, dst.at[slot, pl.ds(pl.multiple_of(dst_row * slab, slab), slab), :],
                                 sem.at[slot])


def _moe_kernel(te_ref, tr_ref, na_ref, tok_ref, h_hbm, w1l_ref, w3l_ref, w2l_ref, w1h_ref, w3h_ref, w2h_ref,
                o_ref, xg, xb, acc, sem, *, tm, n_lo):
    b = pl.program_id(0)
    f = pl.program_id(1)
    nf = pl.num_programs(1)
    n_act = na_ref[0]
    slot = b % 2
    slab = xb.shape[1] // LANES

    def start_gather(tile, slt):
        def body(s, carry):
            _slab_copy(h_hbm, tok_ref[tile * tm + s], xg, slt, s, slab, sem).start()
            return carry
        lax.fori_loop(0, tm, body, 0, unroll=8)

    @pl.when(jnp.logical_and(f == 0, b < n_act))
    def _():
        @pl.when(b == 0)
        def _():
            start_gather(0, 0)

        @pl.when(b + 1 < n_act)
        def _():
            start_gather(b + 1, 1 - slot)

        pltpu.make_async_copy(h_hbm.at[pl.ds(0, tm * slab), :], xg.at[slot], sem.at[slot]).wait()
        for a in range(slab):
            xb[:, a * LANES:(a + 1) * LANES] = xg[slot, pl.ds(a, tm, stride=slab), :].astype(BF16)
        acc[...] = jnp.zeros_like(acc)

    def swiglu_step(w1_ref, w3_ref, w2_ref, rows):
        x = xb[0:rows, :]
        a_ = jnp.dot(x, w1_ref[0].astype(BF16), preferred_element_type=F32)
        b_ = jnp.dot(x, w3_ref[0].astype(BF16), preferred_element_type=F32)
        u = (a_ * jax.nn.sigmoid(a_) * b_).astype(BF16)
        acc[0:rows, :] += jnp.dot(u, w2_ref[0].astype(BF16), preferred_element_type=F32)

    in_lo = te_ref[b] < n_lo
    for rows in range(MOE_ROW_STEP, tm + 1, MOE_ROW_STEP):
        has_rows = jnp.logical_and(b < n_act, tr_ref[b] == rows)

        @pl.when(jnp.logical_and(has_rows, in_lo))
        def _():
            swiglu_step(w1l_ref, w3l_ref, w2l_ref, rows)

        @pl.when(jnp.logical_and(has_rows, jnp.logical_not(in_lo)))
        def _():
            swiglu_step(w1h_ref, w3h_ref, w2h_ref, rows)

    @pl.when(f == nf - 1)
    def _():
        res = jnp.where(b < n_act, acc[...], 0.0)
        for a in range(slab):
            o_ref[pl.ds(a, tm, stride=slab), :] = res[:, a * LANES:(a + 1) * LANES]


MOE_ROW_STEP = 128


def _moe(h2_slab, tile_e, tile_rows, n_active, slot_tok, w_lo, w_hi, tm, n_tiles):
    n_lo, d, ff = w_lo[0].shape
    tf = 512
    nf = ff // tf

    def maps(lo):
        def pick(b, f, te, na):
            e = te[b]
            mine = jnp.logical_and(b < na[0], (e < n_lo) if lo else (e >= n_lo))
            e_loc = jnp.minimum(e, n_lo - 1) if lo else jnp.maximum(e - n_lo, 0)
            return e_loc, jnp.where(mine, f, (nf - 1) if lo else 0)

        def w13_map(b, f, te, tr, na, tok):
            e_loc, fb = pick(b, f, te, na)
            return (e_loc, 0, fb)

        def w2_map(b, f, te, tr, na, tok):
            e_loc, fb = pick(b, f, te, na)
            return (e_loc, fb, 0)

        return [pl.BlockSpec((1, d, tf), w13_map), pl.BlockSpec((1, d, tf), w13_map),
                pl.BlockSpec((1, tf, d), w2_map)]

    grid_spec = pltpu.PrefetchScalarGridSpec(
        num_scalar_prefetch=4,
        grid=(n_tiles, nf),
        in_specs=[pl.BlockSpec(memory_space=pl.ANY)] + maps(True) + maps(False),
        out_specs=pl.BlockSpec((tm * (d // LANES), LANES), lambda b, f, te, tr, na, tok: (b, 0)),
        scratch_shapes=[pltpu.VMEM((2, tm * (d // LANES), LANES), F32),
                        pltpu.VMEM((tm, d), BF16),
                        pltpu.VMEM((tm, d), F32),
                        pltpu.SemaphoreType.DMA((2,))],
    )
    return pl.pallas_call(
        functools.partial(_moe_kernel, tm=tm, n_lo=n_lo),
        grid_spec=grid_spec,
        out_shape=jax.ShapeDtypeStruct((n_tiles * tm * (d // LANES), LANES), F32),
        compiler_params=_cparams(("arbitrary", "arbitrary")),
        name="moe_experts",
    )(tile_e, tile_rows, n_active, slot_tok, h2_slab, *w_lo, *w_hi)


def _comb_kernel(dest_ref, yb_hbm, x_ref, gt_ref, g2_ref, fg_ref, o_ref, buf, sem, *, tm):
    i = pl.program_id(0)
    n = pl.num_programs(0)
    slot = i % 2
    nrow = TOP_K * tm
    slab = o_ref.shape[1] // LANES

    def start_gather(tile, slt):
        def body(s, carry):
            _slab_copy(yb_hbm, dest_ref[tile * nrow + s], buf, slt, s, slab, sem).start()
            return carry
        lax.fori_loop(0, nrow, body, 0, unroll=8)

    @pl.when(i == 0)
    def _():
        start_gather(0, 0)

    @pl.when(i + 1 < n)
    def _():
        start_gather(i + 1, 1 - slot)

    pltpu.make_async_copy(yb_hbm.at[pl.ds(0, nrow * slab), :], buf.at[slot], sem.at[slot]).wait()
    gt = gt_ref[...]
    p0 = gt[:, 0:1]
    p1 = gt[:, 1:2]
    cols = []
    for a in range(slab):
        y0 = buf[slot, pl.ds(a, tm, stride=TOP_K * slab), :]
        y1 = buf[slot, pl.ds(slab + a, tm, stride=TOP_K * slab), :]
        cols.append(y0 * p0 + y1 * p1)
    fmoe = jnp.concatenate(cols, axis=-1)
    xn = x_ref[...] + g2_ref[...] * fmoe
    ms = jnp.mean(xn * xn, axis=-1, keepdims=True)
    o_ref[...] = xn * lax.rsqrt(ms + NORM_EPS) * fg_ref[...]


def _combine(dest, yb_slab, x, gates, gate2, final_gain):
    t, d = x.shape
    tm = 256
    grid_spec = pltpu.PrefetchScalarGridSpec(
        num_scalar_prefetch=1,
        grid=(t // tm,),
        in_specs=[pl.BlockSpec(memory_space=pl.ANY),
                  pl.BlockSpec((tm, d), lambda i, dr: (i, 0)),
                  pl.BlockSpec((tm, LANES), lambda i, dr: (i, 0)),
                  pl.BlockSpec((1, d), lambda i, dr: (0, 0)),
                  pl.BlockSpec((1, d), lambda i, dr: (0, 0))],
        out_specs=pl.BlockSpec((tm, d), lambda i, dr: (i, 0)),
        scratch_shapes=[pltpu.VMEM((2, TOP_K * tm * (d // LANES), LANES), F32),
                        pltpu.SemaphoreType.DMA((2,))],
    )
    return pl.pallas_call(
        functools.partial(_comb_kernel, tm=tm),
        grid_spec=grid_spec,
        out_shape=jax.ShapeDtypeStruct((t, d), F32),
        compiler_params=_cparams(("arbitrary",)),
        name="moe_combine",
    )(dest, yb_slab, x, gates, gate2, final_gain)


def _pad_rows(w, rows):
    return jnp.pad(w, ((0, rows - w.shape[0]), (0, 0)))


def _lowrank_weights(l, rk_lerp, rk_w1, rk_a1, rk_g1, rk_vmu, rk_v1):
    mus = [rk_lerp[l, 0], rk_lerp[l, 1], rk_lerp[l, 2]]
    ws = [rk_w1[l], rk_a1[l], rk_g1[l]]
    if l > 0:
        mus.append(rk_vmu[l - 1])
        ws.append(rk_v1[l - 1])
    wa = jnp.concatenate([(1.0 - m)[:, None] * w for m, w in zip(mus, ws)], axis=1)
    wb = jnp.concatenate([m[:, None] * w for m, w in zip(mus, ws)], axis=1)
    wa = jnp.pad(wa, ((0, 0), (0, LR_PAD - wa.shape[1])))
    wb = jnp.pad(wb, ((0, 0), (0, LR_PAD - wb.shape[1])))
    return jnp.concatenate([wa, wb], axis=1).astype(BF16)


def _second_stage_weights(l, rk_w2, rk_a2, rk_g2, rk_v2):
    c = rk_w2.shape[2]
    w2 = _pad_rows(rk_w2[l], 128)
    a2 = jnp.concatenate([jnp.zeros((64, c), F32), rk_a2[l]], axis=0)
    g2 = _pad_rows(rk_g2[l], 256)
    if l > 0:
        v2 = jnp.concatenate([jnp.zeros((160, c), F32), rk_v2[l - 1], jnp.zeros((64, c), F32)], axis=0)
    else:
        v2 = jnp.zeros((256, c), F32)
    return jnp.concatenate([w2, a2, g2, v2], axis=0).astype(BF16)


def _route(idx, tm, n_exp, n_tiles):
    n_assign = idx.size
    flat_e = idx.reshape(-1)
    onehot = (flat_e[:, None] == jnp.arange(n_exp, dtype=jnp.int32)[None, :]).astype(jnp.int32)
    csum = jnp.cumsum(onehot, axis=0)
    rank = jnp.take_along_axis(csum, flat_e[:, None], axis=1)[:, 0] - 1
    counts = csum[-1]
    padded = (counts + tm - 1) // tm * tm
    pad_end = jnp.cumsum(padded)
    pad_start = pad_end - padded
    dest = (pad_start[flat_e] + rank).astype(jnp.int32)
    slot_tok = jnp.zeros((n_tiles * tm,), jnp.int32).at[dest].set(
        jnp.arange(n_assign, dtype=jnp.int32) // TOP_K)
    n_active = (pad_end[-1] // tm).astype(jnp.int32)
    tile_ids = jnp.minimum(jnp.arange(n_tiles, dtype=jnp.int32), jnp.maximum(n_active - 1, 0))
    tile_e = jnp.minimum(jnp.searchsorted(pad_end, tile_ids * tm, side='right'), n_exp - 1).astype(jnp.int32)
    filled = counts[tile_e] - (tile_ids * tm - pad_start[tile_e])
    tile_rows = jnp.clip((filled + MOE_ROW_STEP - 1) // MOE_ROW_STEP * MOE_ROW_STEP, MOE_ROW_STEP, tm)
    return dest, slot_tok, tile_e, tile_rows.astype(jnp.int32), n_active.reshape(1)


def kernel(x, c, ada_w, ada_b, w_in, pool_w, pool_scale, pool_out, rk_shift, rk_lerp, rk_w0, rk_w1, rk_w2, rk_a0, rk_a1, rk_a2, rk_g1, rk_g2, rk_kk, rk_ka, rk_rk, rk_gn_w, rk_gn_b, rk_vmu, rk_v0, rk_v1, rk_v2, rk_out, mix_out, ffn_w1, ffn_w3, ffn_w2, router, moe_w1, moe_w3, moe_w2, final_gain):
    bsz, t, d = x.shape
    depth = ada_w.shape[0]
    cw = rk_w0.shape[1]
    n_exp = router.shape[2]
    assert bsz == 1 and depth == 2 and cw % LANES == 0
    xs = x.reshape(t, d)
    mod = _ada(c.reshape(d, 1), ada_w, ada_b)

    def mod_row(l, i):
        return mod[l, :, i * d:(i + 1) * d]

    v_first = None
    out = None
    for l in range(depth):
        w_lr = _lowrank_weights(l, rk_lerp, rk_w1, rk_a1, rk_g1, rk_vmu, rk_v1)
        if l == 0:
            g_main, g_lr = _in_proj(xs, mod_row(l, 1), mod_row(l, 0), w_in, l, w_lr)
        else:
            g_main, g_lr = _in_proj(xs, mod_row(l, 1), mod_row(l, 0), w_in_next.reshape(1, d, -1), 0, w_lr)
        w2s = _second_stage_weights(l, rk_w2, rk_a2, rk_g2, rk_v2)
        zeros_c = jnp.zeros((cw,), F32)
        rows = [rk_w0[l], rk_a0[l], rk_v0[l - 1] if l > 0 else zeros_c, rk_kk[l], rk_ka[l],
                rk_gn_w[l], rk_gn_b[l], rk_shift[l, 0], rk_shift[l, 1], rk_shift[l, 2],
                rk_rk[l].reshape(cw)]
        pvec = _pad_rows(jnp.stack(rows, axis=0), PV_ROWS)
        steps = _rwkv_work_steps(t, cw)
        mixer = [w.reshape(-1, w.shape[-1]) for w in (pool_w, pool_out, rk_out, mix_out)]
        mixer_casts = [_cast_job(w, l * (w.shape[0] // depth), (l + 1) * (w.shape[0] // depth), steps)
                       for w in mixer]
        if l == 0:
            dense = [ffn_w1[0], ffn_w3[0], ffn_w2[0]]
            experts = [moe_w1[0].reshape(-1, moe_w1.shape[-1]), moe_w3[0].reshape(-1, moe_w3.shape[-1]),
                       moe_w2[0].reshape(-1, moe_w2.shape[-1])]
            casts = ([_cast_job(w, 0, w.shape[0], steps) for w in dense]
                     + [_cast_job(w, 0, w.shape[0] // 2, steps) for w in experts]
                     + [_cast_job(w_in.reshape(-1, w_in.shape[-1]), d, 2 * d, steps)] + mixer_casts)
            yr, v_first, *cast_out = _rwkv(g_main, g_lr, None, w2s, pvec, has_vres=False, casts=casts)
            dense_bf, experts_lo, w_in_next, mixer_bf = cast_out[:3], cast_out[3:6], cast_out[6], cast_out[7:]
        else:
            casts = [_cast_job(w, w.shape[0] // 2, w.shape[0], steps) for w in experts] + mixer_casts
            yr, *cast_out = _rwkv(g_main, g_lr, v_first, w2s, pvec, has_vres=True, casts=casts)
            experts_hi, mixer_bf = cast_out[:3], cast_out[3:]
        is_moe = (l % 2 == 1)
        router_pad = None
        if is_moe:
            router_pad = jnp.pad(router[l // 2], ((0, 0), (0, LANES - n_exp)))
        res = _mix(xs, g_main, yr, mixer_bf[0].reshape(pool_w.shape[1:]), pool_scale[l].reshape(1, -1),
                   mixer_bf[1], mixer_bf[2], mixer_bf[3], mod_row(l, 2), mod_row(l, 4), mod_row(l, 3), router_pad)
        if not is_moe:
            x_new, h2 = res
            xs = _ffn(h2, x_new, mod_row(l, 5), *dense_bf)
        else:
            x_new, h2, idx_pad, gate_pad = res
            tm_moe = 512
            n_tiles = (t * TOP_K) // tm_moe + n_exp
            dest, slot_tok, tile_e, tile_rows, n_active = _route(idx_pad[:, :TOP_K], tm_moe, n_exp, n_tiles)
            n_lo = n_exp // 2
            w_lo = [w.reshape((n_lo,) + src.shape[2:]) for w, src in zip(experts_lo, (moe_w1, moe_w3, moe_w2))]
            w_hi = [w.reshape((n_exp - n_lo,) + src.shape[2:])
                    for w, src in zip(experts_hi, (moe_w1, moe_w3, moe_w2))]
            yb_slab = _moe(h2, tile_e, tile_rows, n_active, slot_tok, w_lo, w_hi, tm_moe, n_tiles)
            out = _combine(dest, yb_slab, x_new, gate_pad, mod_row(l, 5), final_gain.reshape(1, d))
    return out.reshape(bsz, t, d)
```

```python
import functools

import jax
import jax.numpy as jnp
from jax import lax
from jax.experimental import pallas as pl
from jax.experimental.pallas import tpu as pltpu

F32 = jnp.float32
BF16 = jnp.bfloat16

NORM_EPS = 1e-6
GROUPNORM_EPS = 64e-5
L2_EPS = 1e-12
POOL_WINDOWS = (2, 4, 8, 16)
HEAD_DIM = 64
TOP_K = 2
LANES = 128
SUBLANES = 8
CHUNK = 64
LR_PAD = 384
POOL_HALO = 16
VMEM_LIMIT = 56 * 1024 * 1024


def _cparams(sem):
    return pltpu.CompilerParams(dimension_semantics=sem, vmem_limit_bytes=VMEM_LIMIT)


def _mm(a, b):
    return jnp.dot(a.astype(BF16), b.astype(BF16), preferred_element_type=F32)


def _mm_nt(a, b):
    return lax.dot_general(a.astype(BF16), b.astype(BF16), (((1,), (1,)), ((), ())),
                           preferred_element_type=F32)


def _split(x):
    hi = x.astype(BF16)
    lo = (x - hi.astype(F32)).astype(BF16)
    return hi, lo


def _ada_kernel(c_ref, w_ref, b_ref, o_ref):
    c = c_ref[...]
    act = c * jax.nn.sigmoid(c)
    o_ref[0] = jnp.sum(act * w_ref[0], axis=0, keepdims=True) + b_ref[0]


def _ada(c_col, ada_w, ada_b):
    depth, d, n = ada_w.shape
    tn = 1024
    return pl.pallas_call(
        _ada_kernel,
        grid=(depth, n // tn),
        in_specs=[pl.BlockSpec((d, 1), lambda l, j: (0, 0)),
                  pl.BlockSpec((1, d, tn), lambda l, j: (l, 0, j)),
                  pl.BlockSpec((1, 1, tn), lambda l, j: (l, 0, j))],
        out_specs=pl.BlockSpec((1, 1, tn), lambda l, j: (l, 0, j)),
        out_shape=jax.ShapeDtypeStruct((depth, 1, n), F32),
        compiler_params=_cparams(("arbitrary", "arbitrary")),
        name="ada_gemv",
    )(c_col, ada_w, ada_b.reshape(depth, 1, n))


def _in_kernel(x_ref, sc_ref, sh_ref, w_ref, wlr_ref, o_ref, olr_ref, h_sc):
    @pl.when(pl.program_id(1) == 0)
    def _():
        x = x_ref[...]
        ms = jnp.mean(x * x, axis=-1, keepdims=True)
        h = x * lax.rsqrt(ms + NORM_EPS) * (1.0 + sc_ref[...]) + sh_ref[...]
        hb = h.astype(BF16)
        h_sc[...] = hb
        olr_ref[...] = jnp.dot(hb, wlr_ref[...], preferred_element_type=F32)

    o_ref[...] = jnp.dot(h_sc[...], w_ref[0].astype(BF16), preferred_element_type=F32).astype(o_ref.dtype)


def _in_proj(x, scale, shift, w_in, layer, w_lr):
    t, d = x.shape
    n = w_in.shape[2]
    nlr = w_lr.shape[1]
    tm, tn = 1024, 512
    return pl.pallas_call(
        _in_kernel,
        grid=(t // tm, n // tn),
        in_specs=[pl.BlockSpec((tm, d), lambda i, j: (i, 0)),
                  pl.BlockSpec((1, d), lambda i, j: (0, 0)),
                  pl.BlockSpec((1, d), lambda i, j: (0, 0)),
                  pl.BlockSpec((1, d, tn), lambda i, j: (layer, 0, j)),
                  pl.BlockSpec((d, nlr), lambda i, j: (0, 0))],
        out_specs=[pl.BlockSpec((tm, tn), lambda i, j: (i, j)),
                   pl.BlockSpec((tm, nlr), lambda i, j: (i, 0))],
        out_shape=[jax.ShapeDtypeStruct((t, n), BF16), jax.ShapeDtypeStruct((t, nlr), F32)],
        scratch_shapes=[pltpu.VMEM((tm, d), BF16)],
        compiler_params=_cparams(("arbitrary", "arbitrary")),
        name="in_proj",
    )(x, scale, shift, w_in, w_lr)


PV_W0, PV_A0, PV_V0, PV_KK, PV_KA, PV_GNW, PV_GNB, PV_MUR, PV_MUK, PV_MUV, PV_RK = range(11)
PV_ROWS = 16


def _rwkv_kernel(*refs, tt, ntile, nsteps, has_vres, n_cast):
    n_in = 7 if has_vres else 6
    n_out = 1 if has_vres else 2
    ins = refs[:n_in]
    cast_src = refs[n_in:n_in + n_cast]
    outs = refs[n_in + n_cast:n_in + n_cast + n_out]
    cast_dst = refs[n_in + n_cast + n_out:n_in + 2 * n_cast + n_out]
    (s_sc, c_rkv, c_lr, qm_sc, yin_sc, cc_sc, bg_sc, gn_sc,
     opf_sc, opb_sc, el_sc, bgp_sc, gnp_sc) = refs[n_in + 2 * n_cast + n_out:]
    if has_vres:
        (r_ref, k_ref, v_ref, lr_ref, vf_ref, w2_ref, pv_ref), (yr_ref,) = ins, outs
    else:
        (r_ref, k_ref, v_ref, lr_ref, w2_ref, pv_ref), (yr_ref, vfo_ref) = ins, outs
    step = pl.program_id(0)
    n_work = nsteps - 2
    is_drain = step >= n_work
    prep_tile = jnp.minimum(step, n_work - 1) % ntile
    chain_tile = jnp.maximum(step - 2, 0) % ntile
    wr = step % 2
    rd = 1 - wr
    chunks = range(tt // CHUNK)

    @pl.when(step == 0)
    def _():
        for ref in (s_sc, qm_sc, yin_sc, cc_sc, bg_sc, gn_sc, opf_sc, opb_sc, el_sc, bgp_sc, gnp_sc):
            ref[...] = jnp.zeros_like(ref)

    @pl.when(prep_tile == 0)
    def _():
        c_rkv[...] = jnp.zeros_like(c_rkv)
        c_lr[...] = jnp.zeros_like(c_lr)

    li = lax.broadcasted_iota(jnp.int32, (LANES, LANES), 0)
    lj = lax.broadcasted_iota(jnp.int32, (LANES, LANES), 1)
    same_head = ((li >> 6) == (lj >> 6))
    head_ones = jnp.where(same_head, 1.0, 0.0).astype(BF16)
    strict = (li & (CHUNK - 1)) > (lj & (CHUNK - 1))
    incl = (li & (CHUNK - 1)) >= (lj & (CHUNK - 1))
    eye = li == lj
    eye_f = jnp.where(eye, 1.0, 0.0)
    off_masks = [((li >> (s + 1)) == (lj >> (s + 1))) & (((li >> s) & 1) == 1) & (((lj >> s) & 1) == 0)
                 for s in range(CHUNK.bit_length() - 1)]

    def head_sum(x):
        return _mm(x, head_ones)

    OP_RH, OP_AH = 0, 1
    OP_BT, OP_KT, OP_VV, OP_BBT, OP_KBT = range(5)
    tile = {}

    def prep_tile_wide():
        row = lax.broadcasted_iota(jnp.int32, (tt, 1), 0)

        def shift(x, carry_row):
            return jnp.where(row == 0, carry_row, pltpu.roll(x, 1, axis=0))

        pv = pv_ref[...]

        def prow(i):
            return pv[i:i + 1, :]

        r_raw, k_raw, v_raw = (ref[...].astype(F32) for ref in (r_ref, k_ref, v_ref))
        zb = lr_ref[:, LR_PAD:2 * LR_PAD]
        c_rkv_used = jnp.where(is_drain, c_rkv[4:7, :], c_rkv[0:3, :])
        c_z = jnp.where(is_drain, c_lr[1:2, :], c_lr[0:1, :])
        r_prev = shift(r_raw, c_rkv_used[0:1, :])
        k_prev = shift(k_raw, c_rkv_used[1:2, :])
        v_prev = shift(v_raw, c_rkv_used[2:3, :])
        z = lr_ref[:, 0:LR_PAD] + shift(zb, c_z)
        c_rkv[4:7, :] = c_rkv_used
        c_lr[1:2, :] = c_z
        c_rkv[0:1, :] = r_raw[tt - 1:tt, :]
        c_rkv[1:2, :] = k_raw[tt - 1:tt, :]
        c_rkv[2:3, :] = v_raw[tt - 1:tt, :]
        c_lr[0:1, :] = zb[tt - 1:tt, :]

        r = r_raw + (r_prev - r_raw) * prow(PV_MUR)
        k = k_raw + (k_prev - k_raw) * prow(PV_MUK)
        v = v_raw + (v_prev - v_raw) * prow(PV_MUV)

        z_wa = z[:, 0:LANES]
        z_gv = z[:, LANES:LR_PAD]
        w_pre = prow(PV_W0) + _mm(jnp.tanh(z_wa), w2_ref[0:128, :])
        neg = -w_pre
        softplus = jnp.maximum(neg, 0.0) + jnp.log(1.0 + jnp.exp(-jnp.abs(neg)))
        logd = -jnp.exp(-softplus - 0.5)
        a_i = jax.nn.sigmoid(prow(PV_A0) + _mm(z_wa, w2_ref[128:256, :]))
        g = _mm(jax.nn.sigmoid(z_gv), w2_ref[256:512, :])
        if has_vres:
            v = v + (vf_ref[...] - v) * jax.nn.sigmoid(prow(PV_V0) + _mm(z_gv, w2_ref[512:768, :]))
        else:
            vfo_ref[...] = v

        kk = k * prow(PV_KK)
        kk = kk / jnp.maximum(jnp.sqrt(head_sum(kk * kk)), L2_EPS)
        k2 = k * (1.0 + (a_i - 1.0) * prow(PV_KA))
        tile.update(r=r, v=v, k2=k2, a_vec=-kk, b_vec=kk * a_i, logd=logd, g=g,
                    bonus=head_sum(r * k2 * prow(PV_RK)) * v, gnw=prow(PV_GNW), gnb=prow(PV_GNB))

    ci = lax.broadcasted_iota(jnp.int32, (CHUNK, CHUNK), 0)
    cj = lax.broadcasted_iota(jnp.int32, (CHUNK, CHUNK), 1)
    tri_incl = jnp.where(ci >= cj, 1.0, 0.0).astype(BF16)
    head0 = lax.broadcasted_iota(jnp.int32, (CHUNK, LANES), 1) < HEAD_DIM

    def stack(x):
        return jnp.concatenate([jnp.where(head0, x, 0.0), jnp.where(head0, 0.0, x)], axis=0)

    def prep(c):
        def run():
            sl = slice(c * CHUNK, (c + 1) * CHUNK)
            ld = tile['logd'][sl]
            ld_hi, ld_lo = _split(ld)
            lp = (jnp.dot(tri_incl, ld_hi, preferred_element_type=F32)
                  + jnp.dot(tri_incl, ld_lo, preferred_element_type=F32))
            le = lp[CHUNK - 1:CHUNK, :]
            e_p = jnp.exp(lp)
            e_n = jnp.exp(-lp)
            e_e = jnp.exp(le - lp)
            b_c, k_c = tile['b_vec'][sl], tile['k2'][sl]
            opf_sc[wr, OP_RH, c] = stack(tile['r'][sl] * e_p)
            opf_sc[wr, OP_AH, c] = stack(tile['a_vec'][sl] * jnp.exp(lp - ld))
            opb_sc[wr, OP_BT, c] = stack(b_c * e_n).astype(BF16)
            opb_sc[wr, OP_KT, c] = stack(k_c * e_n).astype(BF16)
            opb_sc[wr, OP_VV, c] = stack(tile['v'][sl]).astype(BF16)
            opb_sc[wr, OP_BBT, c] = stack(b_c * e_e).T.astype(BF16)
            opb_sc[wr, OP_KBT, c] = stack(k_c * e_e).T.astype(BF16)
            el_sc[wr, c] = jnp.broadcast_to(jnp.exp(le), (SUBLANES, LANES))
        return run

    def substages():
        ops = [dict(c=c) for c in chunks]

        def gram():
            for o in ops:
                c = o['c']
                ar = jnp.concatenate([opf_sc[rd, OP_AH, c], opf_sc[rd, OP_RH, c]], axis=0).astype(BF16)
                gb = _mm_nt(ar, opb_sc[rd, OP_BT, c])
                gk = _mm_nt(ar, opb_sc[rd, OP_KT, c])
                o['a_ab'] = jnp.where(strict, gb[0:LANES], 0.0)
                o['a_rb'] = jnp.where(incl, gb[LANES:], 0.0).astype(BF16)
                o['a_ak'] = jnp.where(strict, gk[0:LANES], 0.0).astype(BF16)
                o['a_rk'] = jnp.where(incl, gk[LANES:], 0.0).astype(BF16)

        def values():
            for o in ops:
                vv = opb_sc[rd, OP_VV, o['c']]
                o['x'] = _mm(o['a_ak'], vv)
                o['yv'] = _mm(o['a_rk'], vv)
                o['cv'] = _mm(opb_sc[rd, OP_KBT, o['c']], vv)
                o['t'] = eye_f + jnp.where(off_masks[0], o['a_ab'], 0.0)

        def level_a(off):
            def run():
                for o in ops:
                    o['ta'] = _mm(o['t'], jnp.where(off, o['a_ab'], 0.0))
            return run

        def level_b():
            for o in ops:
                o['t'] = o['t'] + _mm(o['ta'], o['t'])

        def apply_inverse():
            for o in ops:
                ah_x = jnp.concatenate([opf_sc[rd, OP_AH, o['c']], o['x']], axis=1)
                o['w'] = _mm(o['t'], ah_x).astype(BF16)

        def emit():
            for o in ops:
                c = o['c']
                qy = _mm(o['a_rb'], o['w'])
                mc = _mm(opb_sc[rd, OP_BBT, c], o['w'])
                diag = jnp.where(eye, el_sc[rd, c][0:1, :], 0.0)
                qm_sc[c] = jnp.concatenate([opf_sc[rd, OP_RH, c] + qy[:, 0:LANES], diag + mc[:, 0:LANES]],
                                           axis=0).astype(BF16)
                yin_sc[c] = qy[:, LANES:] + o['yv']
                cc_sc[c] = mc[:, LANES:] + o['cv']

        steps = [gram, values]
        for off in off_masks[1:]:
            steps += [level_a(off), level_b]
        return steps + [apply_inverse, emit]

    chain = dict(s=jnp.where(chain_tile == 0, 0.0, s_sc[...]), ys=[])

    def link(c):
        def run():
            res = _mm(qm_sc[c], chain['s'])
            y2 = res[0:LANES] + yin_sc[c]
            chain['ys'].append(y2[0:CHUNK] + y2[CHUNK:2 * CHUNK])
            chain['s'] = res[LANES:] + cc_sc[c]
        return run

    def finish_chain():
        s_sc[...] = chain['s']
        y = jnp.concatenate(chain['ys'], axis=0)
        mean = head_sum(y) * (1.0 / HEAD_DIM)
        dlt = y - mean
        var = head_sum(dlt * dlt) * (1.0 / HEAD_DIM)
        yn = dlt * lax.rsqrt(var + GROUPNORM_EPS) * gn_sc[0:1, :] + gn_sc[1:2, :]
        yr_ref[...] = (yn + bg_sc[0]) * bg_sc[1]

    def cast(j):
        def run():
            cast_dst[j][...] = cast_src[j][...].astype(BF16)
        return run

    steps = substages()
    fill = [prep_tile_wide]
    for c in chunks:
        fill += [link(c), prep(c)] + [cast(j) for j in range(n_cast)[c::len(chunks)]]
    fill += [finish_chain]
    gaps = len(steps) - 1
    for i, run in enumerate(steps):
        run()
        for item in fill[i * len(fill) // gaps:(i + 1) * len(fill) // gaps] if i < gaps else ():
            item()
    bg_sc[...] = bgp_sc[...]
    gn_sc[...] = gnp_sc[...]
    bgp_sc[0] = tile['bonus']
    bgp_sc[1] = tile['g']
    gnp_sc[0:1, :] = tile['gnw']
    gnp_sc[1:2, :] = tile['gnb']


RWKV_TILE = 512
RWKV_STAGES = 3


def _rwkv_work_steps(t, c):
    return (c // LANES) * (t // RWKV_TILE)


def _cast_job(src, lo, hi, steps):
    rows = hi - lo
    count = max(n for n in range(1, steps + 1) if rows % n == 0 and (rows // n) % 16 == 0)
    blk = rows // count
    assert lo % blk == 0
    return (src, blk, lo // blk, count)


def _rwkv(g_main, g_lr, v_first, w2s, pvec, has_vres, casts=()):
    t = g_main.shape[0]
    c = pvec.shape[1]
    tt = RWKV_TILE
    ntile = t // tt
    n_work = _rwkv_work_steps(t, c)
    nsteps = n_work + RWKV_STAGES - 1
    col0 = c // LANES
    nch = tt // CHUNK

    def prep_of(s):
        ss = jnp.minimum(s, n_work - 1)
        return ss // ntile, ss % ntile

    def chain_of(s):
        cs = jnp.maximum(s - (RWKV_STAGES - 1), 0)
        return cs // ntile, cs % ntile

    def in_map(col_mult):
        return lambda s: (prep_of(s)[1], col_mult * col0 + prep_of(s)[0])

    in_specs = [pl.BlockSpec((tt, LANES), in_map(1)),
                pl.BlockSpec((tt, LANES), in_map(2)),
                pl.BlockSpec((tt, LANES), in_map(3)),
                pl.BlockSpec((tt, 2 * LR_PAD), lambda s: (prep_of(s)[1], 0))]
    args = [g_main, g_main, g_main, g_lr]
    if has_vres:
        in_specs.append(pl.BlockSpec((tt, LANES), in_map(0)))
        args.append(v_first)
    in_specs += [pl.BlockSpec((768, LANES), lambda s: (0, prep_of(s)[0])),
                 pl.BlockSpec((PV_ROWS, LANES), lambda s: (0, prep_of(s)[0]))]
    args += [w2s, pvec]
    yr_spec = pl.BlockSpec((tt, LANES), lambda s: (chain_of(s)[1], chain_of(s)[0]))
    out_shape = jax.ShapeDtypeStruct((t, c), F32)
    if has_vres:
        out_specs, out_shapes = [yr_spec], [out_shape]
    else:
        out_specs, out_shapes = [yr_spec, pl.BlockSpec((tt, LANES), in_map(0))], [out_shape, out_shape]
    for src, rows, first, count in casts:
        blk = (rows, src.shape[1])
        in_specs.append(pl.BlockSpec(blk, lambda s, first=first, count=count: (first + jnp.minimum(s, count - 1), 0)))
        args.append(src)
        out_specs.append(pl.BlockSpec(blk, lambda s, count=count: (jnp.minimum(s, count - 1), 0)))
        out_shapes.append(jax.ShapeDtypeStruct((rows * count, src.shape[1]), BF16))
    return pl.pallas_call(
        functools.partial(_rwkv_kernel, tt=tt, ntile=ntile, nsteps=nsteps, has_vres=has_vres, n_cast=len(casts)),
        grid=(nsteps,),
        in_specs=in_specs,
        out_specs=out_specs,
        out_shape=out_shapes,
        scratch_shapes=[pltpu.VMEM((LANES, LANES), F32),
                        pltpu.VMEM((SUBLANES, LANES), F32),
                        pltpu.VMEM((SUBLANES, LR_PAD), F32),
                        pltpu.VMEM((nch, 2 * LANES, LANES), BF16),
                        pltpu.VMEM((nch, LANES, LANES), F32),
                        pltpu.VMEM((nch, LANES, LANES), F32),
                        pltpu.VMEM((2, tt, LANES), F32),
                        pltpu.VMEM((SUBLANES, LANES), F32),
                        pltpu.VMEM((2, 2, nch, LANES, LANES), F32),
                        pltpu.VMEM((2, 5, nch, LANES, LANES), BF16),
                        pltpu.VMEM((2, nch, SUBLANES, LANES), F32),
                        pltpu.VMEM((2, tt, LANES), F32),
                        pltpu.VMEM((SUBLANES, LANES), F32)],
        compiler_params=_cparams(("arbitrary",)),
        name="rwkv7_vres" if has_vres else "rwkv7",
    )(*args)


def _mix_kernel(*refs, tm, has_router):
    if has_router:
        (x_ref, pp_ref, halo_ref, glp_ref, glr_ref, yr_ref, pw_ref, ps_ref, po_ref, ro_ref, mo_ref,
         g1_ref, sc2_ref, sh2_ref, rt_ref, xo_ref, h2_ref, idx_ref, gt_ref, ext) = refs
    else:
        (x_ref, pp_ref, halo_ref, glp_ref, glr_ref, yr_ref, pw_ref, ps_ref, po_ref, ro_ref, mo_ref,
         g1_ref, sc2_ref, sh2_ref, xo_ref, h2_ref, ext) = refs
    i = pl.program_id(0)
    ext[0:POOL_HALO, :] = jnp.where(i == 0, 0.0, halo_ref[...].astype(F32))
    ext[POOL_HALO:POOL_HALO + tm, :] = pp_ref[...].astype(F32)
    pos = i * tm + lax.broadcasted_iota(jnp.int32, (tm, 1), 0)
    gw = ext.shape[1] // len(POOL_WINDOWS)
    parts = []
    for gi, win in enumerate(POOL_WINDOWS):
        cs = slice(gi * gw, (gi + 1) * gw)
        cur = ext[POOL_HALO:POOL_HALO + tm, cs]
        s = cur
        for j in range(1, win):
            s = s + ext[POOL_HALO - j:POOL_HALO - j + tm, cs]
        cnt = jnp.minimum(pos + 1, win).astype(F32)
        pooled = s / cnt - cur
        parts.append(_mm(pooled, pw_ref[gi]))
    mixed = jnp.concatenate(parts, axis=-1) * ps_ref[...]
    y_pool = _mm(mixed, po_ref[...])
    y_rwkv = _mm(yr_ref[...], ro_ref[...])
    merged = (jax.nn.sigmoid(glp_ref[...].astype(F32)) * y_pool
              + jax.nn.sigmoid(glr_ref[...].astype(F32)) * y_rwkv)
    xn = x_ref[...] + g1_ref[...] * _mm(merged, mo_ref[...])
    xo_ref[...] = xn
    ms = jnp.mean(xn * xn, axis=-1, keepdims=True)
    h2 = xn * lax.rsqrt(ms + NORM_EPS) * (1.0 + sc2_ref[...]) + sh2_ref[...]
    if has_router:
        slab = h2.shape[1] // LANES
        for a in range(slab):
            h2_ref[pl.ds(a, tm, stride=slab), :] = h2[:, a * LANES:(a + 1) * LANES]
        rt = rt_ref[...]
        hi, lo = _split(h2)
        rhi, rlo = _split(rt)
        logits = (jnp.dot(hi, rhi, preferred_element_type=F32) + jnp.dot(lo, rhi, preferred_element_type=F32)
                  + jnp.dot(hi, rlo, preferred_element_type=F32))
        n_exp = 8
        lane = lax.broadcasted_iota(jnp.int32, logits.shape, 1)
        lane_f = lane.astype(F32)
        big = float(LANES)
        lg = jnp.where(lane < n_exp, logits, -jnp.inf)
        m1 = jnp.max(lg, axis=-1, keepdims=True)
        i1 = jnp.min(jnp.where(lg == m1, lane_f, big), axis=-1, keepdims=True)
        lg2 = jnp.where(lane_f == i1, -jnp.inf, lg)
        m2 = jnp.max(lg2, axis=-1, keepdims=True)
        i2 = jnp.min(jnp.where(lg2 == m2, lane_f, big), axis=-1, keepdims=True)
        e2 = jnp.exp(m2 - m1)
        p1 = 1.0 / (1.0 + e2)
        p2 = e2 / (1.0 + e2)
        idx_ref[...] = jnp.where(lane == 0, i1, jnp.where(lane == 1, i2, 0.0)).astype(jnp.int32)
        gt_ref[...] = jnp.where(lane == 0, p1, jnp.where(lane == 1, p2, 0.0))
    else:
        h2_ref[...] = h2.astype(h2_ref.dtype)


def _mix(x, g_main, yr, pool_w, pool_scale, pool_out, rk_out, mix_out, gate1, scale2, shift2, router_pad):
    t, d = x.shape
    c = yr.shape[1]
    tm = 256
    has_router = router_pad is not None
    halo_blocks = tm // POOL_HALO
    cb_glp = (4 * c) // d
    in_specs = [pl.BlockSpec((tm, d), lambda i: (i, 0)),
                pl.BlockSpec((tm, c), lambda i: (i, 0)),
                pl.BlockSpec((POOL_HALO, c), lambda i: (jnp.maximum(i * halo_blocks - 1, 0), 0)),
                pl.BlockSpec((tm, d), lambda i: (i, cb_glp)),
                pl.BlockSpec((tm, d), lambda i: (i, cb_glp + 1)),
                pl.BlockSpec((tm, c), lambda i: (i, 0)),
                pl.BlockSpec(pool_w.shape, lambda i: (0, 0, 0)),
                pl.BlockSpec((1, c), lambda i: (0, 0)),
                pl.BlockSpec((c, d), lambda i: (0, 0)),
                pl.BlockSpec((c, d), lambda i: (0, 0)),
                pl.BlockSpec((d, d), lambda i: (0, 0)),
                pl.BlockSpec((1, d), lambda i: (0, 0)),
                pl.BlockSpec((1, d), lambda i: (0, 0)),
                pl.BlockSpec((1, d), lambda i: (0, 0))]
    args = [x, g_main, g_main, g_main, g_main, yr, pool_w, pool_scale, pool_out, rk_out, mix_out,
            gate1, scale2, shift2]
    out_specs = [pl.BlockSpec((tm, d), lambda i: (i, 0))]
    out_shape = [jax.ShapeDtypeStruct((t, d), F32)]
    if has_router:
        slab = d // LANES
        out_specs.append(pl.BlockSpec((tm * slab, LANES), lambda i: (i, 0)))
        out_shape.append(jax.ShapeDtypeStruct((t * slab, LANES), F32))
    else:
        out_specs.append(pl.BlockSpec((tm, d), lambda i: (i, 0)))
        out_shape.append(jax.ShapeDtypeStruct((t, d), BF16))
    if has_router:
        in_specs.append(pl.BlockSpec((d, LANES), lambda i: (0, 0)))
        args.append(router_pad)
        out_specs += [pl.BlockSpec((tm, LANES), lambda i: (i, 0)), pl.BlockSpec((tm, LANES), lambda i: (i, 0))]
        out_shape += [jax.ShapeDtypeStruct((t, LANES), jnp.int32), jax.ShapeDtypeStruct((t, LANES), F32)]
    return pl.pallas_call(
        functools.partial(_mix_kernel, tm=tm, has_router=has_router),
        grid=(t // tm,),
        in_specs=in_specs,
        out_specs=out_specs,
        out_shape=out_shape,
        scratch_shapes=[pltpu.VMEM((POOL_HALO + tm, c), F32)],
        compiler_params=_cparams(("arbitrary",)),
        name="mix_router" if has_router else "mix",
    )(*args)


def _ffn_kernel(h_ref, x_ref, g_ref, w1_ref, w3_ref, w2_ref, o_ref, acc):
    f = pl.program_id(1)

    @pl.when(f == 0)
    def _():
        acc[...] = jnp.zeros_like(acc)

    h = h_ref[...]
    a = jnp.dot(h, w1_ref[...].astype(BF16), preferred_element_type=F32)
    b = jnp.dot(h, w3_ref[...].astype(BF16), preferred_element_type=F32)
    u = (a * jax.nn.sigmoid(a) * b).astype(BF16)
    acc[...] += jnp.dot(u, w2_ref[...].astype(BF16), preferred_element_type=F32)

    @pl.when(f == pl.num_programs(1) - 1)
    def _():
        o_ref[...] = x_ref[...] + g_ref[...] * acc[...]


def _ffn(h2, x, gate2, w1, w3, w2):
    t, d = x.shape
    ff = w1.shape[1]
    tm, tf = 512, 512
    return pl.pallas_call(
        _ffn_kernel,
        grid=(t // tm, ff // tf),
        in_specs=[pl.BlockSpec((tm, d), lambda i, f: (i, 0)),
                  pl.BlockSpec((tm, d), lambda i, f: (i, 0)),
                  pl.BlockSpec((1, d), lambda i, f: (0, 0)),
                  pl.BlockSpec((d, tf), lambda i, f: (0, f)),
                  pl.BlockSpec((d, tf), lambda i, f: (0, f)),
                  pl.BlockSpec((tf, d), lambda i, f: (f, 0))],
        out_specs=pl.BlockSpec((tm, d), lambda i, f: (i, 0)),
        out_shape=jax.ShapeDtypeStruct((t, d), F32),
        scratch_shapes=[pltpu.VMEM((tm, d), F32)],
        compiler_params=_cparams(("arbitrary", "arbitrary")),
        name="ffn_dense",
    )(h2, x, gate2, w1, w3, w2)


def _slab_copy(src_hbm, src_row, dst, slot, dst_row, slab, sem):
    src = src_hbm.at[pl.ds(pl.multiple_of(src_row * slab, slab), slab), :]
    return pltpu.make_async_copy(src, dst.at[slot, pl.ds(pl.multiple_of(dst_row * slab, slab), slab), :],
                                 sem.at[slot])


def _moe_kernel(te_ref, tr_ref, na_ref, tok_ref, h_hbm, w1_ref, w3_ref, w2l_ref, w2h_ref,
                o_ref, xg, xb, acc, sem, *, tm, n_lo):
    b = pl.program_id(0)
    f = pl.program_id(1)
    nf = pl.num_programs(1)
    n_act = na_ref[0]
    slab = xb.shape[1] // LANES

    def start_gather(tile):
        def body(s, carry):
            _slab_copy(h_hbm, tok_ref[tile * tm + s], xg, 0, s, slab, sem).start()
            return carry
        lax.fori_loop(0, tm, body, 0, unroll=8)

    @pl.when(jnp.logical_and(f == 0, b < n_act))
    def _():
        @pl.when(b == 0)
        def _():
            start_gather(0)

        pltpu.make_async_copy(h_hbm.at[pl.ds(0, tm * slab), :], xg.at[0], sem.at[0]).wait()
        for a in range(slab):
            xb[:, a * LANES:(a + 1) * LANES] = xg[0, pl.ds(a, tm, stride=slab), :].astype(BF16)
        acc[...] = jnp.zeros_like(acc)

        @pl.when(b + 1 < n_act)
        def _():
            start_gather(b + 1)

    def swiglu_step(w2_ref, rows):
        x = xb[0:rows, :]
        a_ = jnp.dot(x, w1_ref[0].astype(BF16), preferred_element_type=F32)
        b_ = jnp.dot(x, w3_ref[0].astype(BF16), preferred_element_type=F32)
        u = (a_ * jax.nn.sigmoid(a_) * b_).astype(BF16)
        acc[0:rows, :] += jnp.dot(u, w2_ref[0].astype(BF16), preferred_element_type=F32)

    in_lo = te_ref[b] < n_lo
    for rows in range(MOE_ROW_STEP, tm + 1, MOE_ROW_STEP):
        has_rows = jnp.logical_and(b < n_act, tr_ref[b] == rows)

        @pl.when(jnp.logical_and(has_rows, in_lo))
        def _():
            swiglu_step(w2l_ref, rows)

        @pl.when(jnp.logical_and(has_rows, jnp.logical_not(in_lo)))
        def _():
            swiglu_step(w2h_ref, rows)

    @pl.when(f == nf - 1)
    def _():
        res = jnp.where(b < n_act, acc[...], 0.0)
        for a in range(slab):
            o_ref[pl.ds(a, tm, stride=slab), :] = res[:, a * LANES:(a + 1) * LANES]


MOE_ROW_STEP = 128


def _moe(h2_slab, tile_e, tile_rows, n_active, slot_tok, w1, w3, w2_lo, w2_hi, tm, n_tiles):
    n_exp, d, ff = w1.shape
    n_lo = w2_lo.shape[0]
    tf = 1024
    nf = ff // tf

    def w13_map(b, f, te, tr, na, tok):
        return (te[b], 0, jnp.where(b < na[0], f, nf - 1))

    def w2_map(lo):
        def index(b, f, te, tr, na, tok):
            e = te[b]
            mine = jnp.logical_and(b < na[0], (e < n_lo) if lo else (e >= n_lo))
            e_loc = jnp.minimum(e, n_lo - 1) if lo else jnp.maximum(e - n_lo, 0)
            return (e_loc, jnp.where(mine, f, (nf - 1) if lo else 0), 0)
        return index

    grid_spec = pltpu.PrefetchScalarGridSpec(
        num_scalar_prefetch=4,
        grid=(n_tiles, nf),
        in_specs=[pl.BlockSpec(memory_space=pl.ANY),
                  pl.BlockSpec((1, d, tf), w13_map), pl.BlockSpec((1, d, tf), w13_map),
                  pl.BlockSpec((1, tf, d), w2_map(True)), pl.BlockSpec((1, tf, d), w2_map(False))],
        out_specs=pl.BlockSpec((tm * (d // LANES), LANES), lambda b, f, te, tr, na, tok: (b, 0)),
        scratch_shapes=[pltpu.VMEM((1, tm * (d // LANES), LANES), F32),
                        pltpu.VMEM((tm, d), BF16),
                        pltpu.VMEM((tm, d), F32),
                        pltpu.SemaphoreType.DMA((1,))],
    )
    return pl.pallas_call(
        functools.partial(_moe_kernel, tm=tm, n_lo=n_lo),
        grid_spec=grid_spec,
        out_shape=jax.ShapeDtypeStruct((n_tiles * tm * (d // LANES), LANES), F32),
        compiler_params=_cparams(("arbitrary", "arbitrary")),
        name="moe_experts",
    )(tile_e, tile_rows, n_active, slot_tok, h2_slab, w1, w3, w2_lo, w2_hi)


def _comb_kernel(dest_ref, yb_hbm, x_ref, gt_ref, g2_ref, fg_ref, o_ref, buf, sem, *, tm):
    i = pl.program_id(0)
    n = pl.num_programs(0)
    slot = i % 2
    nrow = TOP_K * tm
    slab = o_ref.shape[1] // LANES

    def start_gather(tile, slt):
        def body(s, carry):
            _slab_copy(yb_hbm, dest_ref[tile * nrow + s], buf, slt, s, slab, sem).start()
            return carry
        lax.fori_loop(0, nrow, body, 0, unroll=8)

    @pl.when(i == 0)
    def _():
        start_gather(0, 0)

    @pl.when(i + 1 < n)
    def _():
        start_gather(i + 1, 1 - slot)

    pltpu.make_async_copy(yb_hbm.at[pl.ds(0, nrow * slab), :], buf.at[slot], sem.at[slot]).wait()
    gt = gt_ref[...]
    p0 = gt[:, 0:1]
    p1 = gt[:, 1:2]
    cols = []
    for a in range(slab):
        y0 = buf[slot, pl.ds(a, tm, stride=TOP_K * slab), :]
        y1 = buf[slot, pl.ds(slab + a, tm, stride=TOP_K * slab), :]
        cols.append(y0 * p0 + y1 * p1)
    fmoe = jnp.concatenate(cols, axis=-1)
    xn = x_ref[...] + g2_ref[...] * fmoe
    ms = jnp.mean(xn * xn, axis=-1, keepdims=True)
    o_ref[...] = xn * lax.rsqrt(ms + NORM_EPS) * fg_ref[...]


def _combine(dest, yb_slab, x, gates, gate2, final_gain):
    t, d = x.shape
    tm = 256
    grid_spec = pltpu.PrefetchScalarGridSpec(
        num_scalar_prefetch=1,
        grid=(t // tm,),
        in_specs=[pl.BlockSpec(memory_space=pl.ANY),
                  pl.BlockSpec((tm, d), lambda i, dr: (i, 0)),
                  pl.BlockSpec((tm, LANES), lambda i, dr: (i, 0)),
                  pl.BlockSpec((1, d), lambda i, dr: (0, 0)),
                  pl.BlockSpec((1, d), lambda i, dr: (0, 0))],
        out_specs=pl.BlockSpec((tm, d), lambda i, dr: (i, 0)),
        scratch_shapes=[pltpu.VMEM((2, TOP_K * tm * (d // LANES), LANES), F32),
                        pltpu.SemaphoreType.DMA((2,))],
    )
    return pl.pallas_call(
        functools.partial(_comb_kernel, tm=tm),
        grid_spec=grid_spec,
        out_shape=jax.ShapeDtypeStruct((t, d), F32),
        compiler_params=_cparams(("arbitrary",)),
        name="moe_combine",
    )(dest, yb_slab, x, gates, gate2, final_gain)


def _pad_rows(w, rows):
    return jnp.pad(w, ((0, rows - w.shape[0]), (0, 0)))


def _lowrank_weights(l, rk_lerp, rk_w1, rk_a1, rk_g1, rk_vmu, rk_v1):
    mus = [rk_lerp[l, 0], rk_lerp[l, 1], rk_lerp[l, 2]]
    ws = [rk_w1[l], rk_a1[l], rk_g1[l]]
    if l > 0:
        mus.append(rk_vmu[l - 1])
        ws.append(rk_v1[l - 1])
    wa = jnp.concatenate([(1.0 - m)[:, None] * w for m, w in zip(mus, ws)], axis=1)
    wb = jnp.concatenate([m[:, None] * w for m, w in zip(mus, ws)], axis=1)
    wa = jnp.pad(wa, ((0, 0), (0, LR_PAD - wa.shape[1])))
    wb = jnp.pad(wb, ((0, 0), (0, LR_PAD - wb.shape[1])))
    return jnp.concatenate([wa, wb], axis=1).astype(BF16)


def _second_stage_weights(l, rk_w2, rk_a2, rk_g2, rk_v2):
    c = rk_w2.shape[2]
    w2 = _pad_rows(rk_w2[l], 128)
    a2 = jnp.concatenate([jnp.zeros((64, c), F32), rk_a2[l]], axis=0)
    g2 = _pad_rows(rk_g2[l], 256)
    if l > 0:
        v2 = jnp.concatenate([jnp.zeros((160, c), F32), rk_v2[l - 1], jnp.zeros((64, c), F32)], axis=0)
    else:
        v2 = jnp.zeros((256, c), F32)
    return jnp.concatenate([w2, a2, g2, v2], axis=0).astype(BF16)


def _route(idx, tm, n_exp, n_tiles):
    n_assign = idx.size
    flat_e = idx.reshape(-1)
    onehot = (flat_e[:, None] == jnp.arange(n_exp, dtype=jnp.int32)[None, :]).astype(jnp.int32)
    csum = jnp.cumsum(onehot, axis=0)
    rank = jnp.take_along_axis(csum, flat_e[:, None], axis=1)[:, 0] - 1
    counts = csum[-1]
    padded = (counts + tm - 1) // tm * tm
    pad_end = jnp.cumsum(padded)
    pad_start = pad_end - padded
    dest = (pad_start[flat_e] + rank).astype(jnp.int32)
    slot_tok = jnp.zeros((n_tiles * tm,), jnp.int32).at[dest].set(
        jnp.arange(n_assign, dtype=jnp.int32) // TOP_K)
    n_active = (pad_end[-1] // tm).astype(jnp.int32)
    tile_ids = jnp.minimum(jnp.arange(n_tiles, dtype=jnp.int32), jnp.maximum(n_active - 1, 0))
    tile_e = jnp.minimum(jnp.searchsorted(pad_end, tile_ids * tm, side='right'), n_exp - 1).astype(jnp.int32)
    filled = counts[tile_e] - (tile_ids * tm - pad_start[tile_e])
    tile_rows = jnp.clip((filled + MOE_ROW_STEP - 1) // MOE_ROW_STEP * MOE_ROW_STEP, MOE_ROW_STEP, tm)
    return dest, slot_tok, tile_e, tile_rows.astype(jnp.int32), n_active.reshape(1)


def kernel(x, c, ada_w, ada_b, w_in, pool_w, pool_scale, pool_out, rk_shift, rk_lerp, rk_w0, rk_w1, rk_w2, rk_a0, rk_a1, rk_a2, rk_g1, rk_g2, rk_kk, rk_ka, rk_rk, rk_gn_w, rk_gn_b, rk_vmu, rk_v0, rk_v1, rk_v2, rk_out, mix_out, ffn_w1, ffn_w3, ffn_w2, router, moe_w1, moe_w3, moe_w2, final_gain):
    bsz, t, d = x.shape
    depth = ada_w.shape[0]
    cw = rk_w0.shape[1]
    n_exp = router.shape[2]
    assert bsz == 1 and depth == 2 and cw % LANES == 0
    xs = x.reshape(t, d)
    mod = _ada(c.reshape(d, 1), ada_w, ada_b)

    def mod_row(l, i):
        return mod[l, :, i * d:(i + 1) * d]

    v_first = None
    out = None
    for l in range(depth):
        w_lr = _lowrank_weights(l, rk_lerp, rk_w1, rk_a1, rk_g1, rk_vmu, rk_v1)
        if l == 0:
            g_main, g_lr = _in_proj(xs, mod_row(l, 1), mod_row(l, 0), w_in, l, w_lr)
        else:
            g_main, g_lr = _in_proj(xs, mod_row(l, 1), mod_row(l, 0), w_in_next.reshape(1, d, -1), 0, w_lr)
        w2s = _second_stage_weights(l, rk_w2, rk_a2, rk_g2, rk_v2)
        zeros_c = jnp.zeros((cw,), F32)
        rows = [rk_w0[l], rk_a0[l], rk_v0[l - 1] if l > 0 else zeros_c, rk_kk[l], rk_ka[l],
                rk_gn_w[l], rk_gn_b[l], rk_shift[l, 0], rk_shift[l, 1], rk_shift[l, 2],
                rk_rk[l].reshape(cw)]
        pvec = _pad_rows(jnp.stack(rows, axis=0), PV_ROWS)
        steps = _rwkv_work_steps(t, cw)
        mixer = [w.reshape(-1, w.shape[-1]) for w in (pool_w, pool_out, rk_out, mix_out)]
        mixer_casts = [_cast_job(w, l * (w.shape[0] // depth), (l + 1) * (w.shape[0] // depth), steps)
                       for w in mixer]
        if l == 0:
            dense = [ffn_w1[0], ffn_w3[0], ffn_w2[0]]
            experts = [moe_w1[0].reshape(-1, moe_w1.shape[-1]), moe_w3[0].reshape(-1, moe_w3.shape[-1]),
                       moe_w2[0].reshape(-1, moe_w2.shape[-1])]
            e1, e3, e2 = experts
            casts = ([_cast_job(w, 0, w.shape[0], steps) for w in dense]
                     + [_cast_job(e1, 0, e1.shape[0], steps), _cast_job(e2, 0, e2.shape[0] // 2, steps)]
                     + [_cast_job(w_in.reshape(-1, w_in.shape[-1]), d, 2 * d, steps)] + mixer_casts)
            yr, v_first, *cast_out = _rwkv(g_main, g_lr, None, w2s, pvec, has_vres=False, casts=casts)
            dense_bf, (e1_bf, e2_lo), w_in_next, mixer_bf = cast_out[:3], cast_out[3:5], cast_out[5], cast_out[6:]
        else:
            casts = [_cast_job(e3, 0, e3.shape[0], steps),
                     _cast_job(e2, e2.shape[0] // 2, e2.shape[0], steps)] + mixer_casts
            yr, e3_bf, e2_hi, *mixer_bf = _rwkv(g_main, g_lr, v_first, w2s, pvec, has_vres=True, casts=casts)
        is_moe = (l % 2 == 1)
        router_pad = None
        if is_moe:
            router_pad = jnp.pad(router[l // 2], ((0, 0), (0, LANES - n_exp)))
        res = _mix(xs, g_main, yr, mixer_bf[0].reshape(pool_w.shape[1:]), pool_scale[l].reshape(1, -1),
                   mixer_bf[1], mixer_bf[2], mixer_bf[3], mod_row(l, 2), mod_row(l, 4), mod_row(l, 3), router_pad)
        if not is_moe:
            x_new, h2 = res
            xs = _ffn(h2, x_new, mod_row(l, 5), *dense_bf)
        else:
            x_new, h2, idx_pad, gate_pad = res
            tm_moe = 512
            n_tiles = (t * TOP_K) // tm_moe + n_exp
            dest, slot_tok, tile_e, tile_rows, n_active = _route(idx_pad[:, :TOP_K], tm_moe, n_exp, n_tiles)
            n_lo = n_exp // 2
            yb_slab = _moe(h2, tile_e, tile_rows, n_active, slot_tok,
                           e1_bf.reshape(moe_w1.shape[1:]), e3_bf.reshape(moe_w3.shape[1:]),
                           e2_lo.reshape((n_lo,) + moe_w2.shape[2:]),
                           e2_hi.reshape((n_exp - n_lo,) + moe_w2.shape[2:]), tm_moe, n_tiles)
            out = _combine(dest, yb_slab, x_new, gate_pad, mod_row(l, 5), final_gain.reshape(1, d))
    return out.reshape(bsz, t, d)
```

```python
import functools

import jax
import jax.numpy as jnp
from jax import lax
from jax.experimental import pallas as pl
from jax.experimental.pallas import tpu as pltpu

F32 = jnp.float32
BF16 = jnp.bfloat16

NORM_EPS = 1e-6
GROUPNORM_EPS = 64e-5
L2_EPS = 1e-12
POOL_WINDOWS = (2, 4, 8, 16)
HEAD_DIM = 64
TOP_K = 2
LANES = 128
SUBLANES = 8
CHUNK = 64
LR_PAD = 384
POOL_HALO = 16
VMEM_LIMIT = 56 * 1024 * 1024
VMEM_LIMIT_MIX = 60 * 1024 * 1024


def _cparams(sem, vmem_limit=VMEM_LIMIT):
    return pltpu.CompilerParams(dimension_semantics=sem, vmem_limit_bytes=vmem_limit)


def _mm(a, b):
    return jnp.dot(a.astype(BF16), b.astype(BF16), preferred_element_type=F32)


def _mm_nt(a, b):
    return lax.dot_general(a.astype(BF16), b.astype(BF16), (((1,), (1,)), ((), ())),
                           preferred_element_type=F32)


def _split(x):
    hi = x.astype(BF16)
    lo = (x - hi.astype(F32)).astype(BF16)
    return hi, lo


def _ada_kernel(c_ref, w_ref, b_ref, o_ref):
    c = c_ref[...]
    act = c * jax.nn.sigmoid(c)
    o_ref[0] = jnp.sum(act * w_ref[0], axis=0, keepdims=True) + b_ref[0]


def _ada(c_col, ada_w, ada_b):
    depth, d, n = ada_w.shape
    tn = 1024
    return pl.pallas_call(
        _ada_kernel,
        grid=(depth, n // tn),
        in_specs=[pl.BlockSpec((d, 1), lambda l, j: (0, 0)),
                  pl.BlockSpec((1, d, tn), lambda l, j: (l, 0, j)),
                  pl.BlockSpec((1, 1, tn), lambda l, j: (l, 0, j))],
        out_specs=pl.BlockSpec((1, 1, tn), lambda l, j: (l, 0, j)),
        out_shape=jax.ShapeDtypeStruct((depth, 1, n), F32),
        compiler_params=_cparams(("arbitrary", "arbitrary")),
        name="ada_gemv",
    )(c_col, ada_w, ada_b.reshape(depth, 1, n))


def _in_kernel(x_ref, sc_ref, sh_ref, w_ref, wlr_ref, o_ref, olr_ref, h_sc):
    @pl.when(pl.program_id(1) == 0)
    def _():
        x = x_ref[...]
        ms = jnp.mean(x * x, axis=-1, keepdims=True)
        h = x * lax.rsqrt(ms + NORM_EPS) * (1.0 + sc_ref[...]) + sh_ref[...]
        hb = h.astype(BF16)
        h_sc[...] = hb
        olr_ref[...] = jnp.dot(hb, wlr_ref[...], preferred_element_type=F32)

    o_ref[...] = jnp.dot(h_sc[...], w_ref[0].astype(BF16), preferred_element_type=F32).astype(o_ref.dtype)


def _in_proj(x, scale, shift, w_in, layer, w_lr):
    t, d = x.shape
    n = w_in.shape[2]
    nlr = w_lr.shape[1]
    tm = 1024
    tn = 1024 if w_in.dtype == BF16 else 512
    return pl.pallas_call(
        _in_kernel,
        grid=(t // tm, n // tn),
        in_specs=[pl.BlockSpec((tm, d), lambda i, j: (i, 0)),
                  pl.BlockSpec((1, d), lambda i, j: (0, 0)),
                  pl.BlockSpec((1, d), lambda i, j: (0, 0)),
                  pl.BlockSpec((1, d, tn), lambda i, j: (layer, 0, j)),
                  pl.BlockSpec((d, nlr), lambda i, j: (0, 0))],
        out_specs=[pl.BlockSpec((tm, tn), lambda i, j: (i, j)),
                   pl.BlockSpec((tm, nlr), lambda i, j: (i, 0))],
        out_shape=[jax.ShapeDtypeStruct((t, n), BF16), jax.ShapeDtypeStruct((t, nlr), F32)],
        scratch_shapes=[pltpu.VMEM((tm, d), BF16)],
        compiler_params=_cparams(("arbitrary", "arbitrary")),
        name="in_proj",
    )(x, scale, shift, w_in, w_lr)


PV_W0, PV_A0, PV_V0, PV_KK, PV_KA, PV_GNW, PV_GNB, PV_MUR, PV_MUK, PV_MUV, PV_RK = range(11)
PV_ROWS = 16


def _rwkv_kernel(*refs, tt, ntile, nsteps, has_vres, n_cast):
    n_in = 7 if has_vres else 6
    n_out = 1 if has_vres else 2
    ins = refs[:n_in]
    cast_src = refs[n_in:n_in + n_cast]
    outs = refs[n_in + n_cast:n_in + n_cast + n_out]
    cast_dst = refs[n_in + n_cast + n_out:n_in + 2 * n_cast + n_out]
    (s_sc, c_rkv, c_lr, qm_sc, yin_sc, cc_sc, bg_sc, gn_sc,
     opf_sc, opb_sc, el_sc, bgp_sc, gnp_sc) = refs[n_in + 2 * n_cast + n_out:]
    if has_vres:
        (r_ref, k_ref, v_ref, lr_ref, vf_ref, w2_ref, pv_ref), (yr_ref,) = ins, outs
    else:
        (r_ref, k_ref, v_ref, lr_ref, w2_ref, pv_ref), (yr_ref, vfo_ref) = ins, outs
    step = pl.program_id(0)
    n_work = nsteps - 2
    is_drain = step >= n_work
    prep_tile = jnp.minimum(step, n_work - 1) % ntile
    chain_tile = jnp.maximum(step - 2, 0) % ntile
    wr = step % 2
    rd = 1 - wr
    chunks = range(tt // CHUNK)

    @pl.when(step == 0)
    def _():
        for ref in (s_sc, qm_sc, yin_sc, cc_sc, bg_sc, gn_sc, opf_sc, opb_sc, el_sc, bgp_sc, gnp_sc):
            ref[...] = jnp.zeros_like(ref)

    @pl.when(prep_tile == 0)
    def _():
        c_rkv[...] = jnp.zeros_like(c_rkv)
        c_lr[...] = jnp.zeros_like(c_lr)

    li = lax.broadcasted_iota(jnp.int32, (LANES, LANES), 0)
    lj = lax.broadcasted_iota(jnp.int32, (LANES, LANES), 1)
    same_head = ((li >> 6) == (lj >> 6))
    head_ones = jnp.where(same_head, 1.0, 0.0).astype(BF16)
    strict = (li & (CHUNK - 1)) > (lj & (CHUNK - 1))
    incl = (li & (CHUNK - 1)) >= (lj & (CHUNK - 1))
    eye = li == lj
    eye_f = jnp.where(eye, 1.0, 0.0)
    off_masks = [((li >> (s + 1)) == (lj >> (s + 1))) & (((li >> s) & 1) == 1) & (((lj >> s) & 1) == 0)
                 for s in range(CHUNK.bit_length() - 1)]

    def head_sum(x):
        return _mm(x, head_ones)

    OP_RH, OP_AH = 0, 1
    OP_BT, OP_KT, OP_VV, OP_BBT, OP_KBT = range(5)
    tile = {}

    def prep_tile_wide():
        row = lax.broadcasted_iota(jnp.int32, (tt, 1), 0)

        def shift(x, carry_row):
            return jnp.where(row == 0, carry_row, pltpu.roll(x, 1, axis=0))

        pv = pv_ref[...]

        def prow(i):
            return pv[i:i + 1, :]

        r_raw, k_raw, v_raw = (ref[...].astype(F32) for ref in (r_ref, k_ref, v_ref))
        zb = lr_ref[:, LR_PAD:2 * LR_PAD]
        c_rkv_used = jnp.where(is_drain, c_rkv[4:7, :], c_rkv[0:3, :])
        c_z = jnp.where(is_drain, c_lr[1:2, :], c_lr[0:1, :])
        r_prev = shift(r_raw, c_rkv_used[0:1, :])
        k_prev = shift(k_raw, c_rkv_used[1:2, :])
        v_prev = shift(v_raw, c_rkv_used[2:3, :])
        z = lr_ref[:, 0:LR_PAD] + shift(zb, c_z)
        c_rkv[4:7, :] = c_rkv_used
        c_lr[1:2, :] = c_z
        c_rkv[0:1, :] = r_raw[tt - 1:tt, :]
        c_rkv[1:2, :] = k_raw[tt - 1:tt, :]
        c_rkv[2:3, :] = v_raw[tt - 1:tt, :]
        c_lr[0:1, :] = zb[tt - 1:tt, :]

        r = r_raw + (r_prev - r_raw) * prow(PV_MUR)
        k = k_raw + (k_prev - k_raw) * prow(PV_MUK)
        v = v_raw + (v_prev - v_raw) * prow(PV_MUV)

        z_wa = z[:, 0:LANES]
        z_gv = z[:, LANES:LR_PAD]
        w_pre = prow(PV_W0) + _mm(jnp.tanh(z_wa), w2_ref[0:128, :])
        neg = -w_pre
        softplus = jnp.maximum(neg, 0.0) + jnp.log(1.0 + jnp.exp(-jnp.abs(neg)))
        logd = -jnp.exp(-softplus - 0.5)
        a_i = jax.nn.sigmoid(prow(PV_A0) + _mm(z_wa, w2_ref[128:256, :]))
        g = _mm(jax.nn.sigmoid(z_gv), w2_ref[256:512, :])
        if has_vres:
            v = v + (vf_ref[...] - v) * jax.nn.sigmoid(prow(PV_V0) + _mm(z_gv, w2_ref[512:768, :]))
        else:
            vfo_ref[...] = v

        kk = k * prow(PV_KK)
        kk = kk / jnp.maximum(jnp.sqrt(head_sum(kk * kk)), L2_EPS)
        k2 = k * (1.0 + (a_i - 1.0) * prow(PV_KA))
        tile.update(r=r, v=v, k2=k2, a_vec=-kk, b_vec=kk * a_i, logd=logd, g=g,
                    bonus=head_sum(r * k2 * prow(PV_RK)) * v, gnw=prow(PV_GNW), gnb=prow(PV_GNB))

    ci = lax.broadcasted_iota(jnp.int32, (CHUNK, CHUNK), 0)
    cj = lax.broadcasted_iota(jnp.int32, (CHUNK, CHUNK), 1)
    tri_incl = jnp.where(ci >= cj, 1.0, 0.0).astype(BF16)
    head0 = lax.broadcasted_iota(jnp.int32, (CHUNK, LANES), 1) < HEAD_DIM

    def stack(x):
        return jnp.concatenate([jnp.where(head0, x, 0.0), jnp.where(head0, 0.0, x)], axis=0)

    def prep(c):
        def run():
            sl = slice(c * CHUNK, (c + 1) * CHUNK)
            ld = tile['logd'][sl]
            ld_hi, ld_lo = _split(ld)
            lp = (jnp.dot(tri_incl, ld_hi, preferred_element_type=F32)
                  + jnp.dot(tri_incl, ld_lo, preferred_element_type=F32))
            le = lp[CHUNK - 1:CHUNK, :]
            e_p = jnp.exp(lp)
            e_n = jnp.exp(-lp)
            e_e = jnp.exp(le - lp)
            b_c, k_c = tile['b_vec'][sl], tile['k2'][sl]
            opf_sc[wr, OP_RH, c] = stack(tile['r'][sl] * e_p)
            opf_sc[wr, OP_AH, c] = stack(tile['a_vec'][sl] * jnp.exp(lp - ld))
            opb_sc[wr, OP_BT, c] = stack(b_c * e_n).astype(BF16)
            opb_sc[wr, OP_KT, c] = stack(k_c * e_n).astype(BF16)
            opb_sc[wr, OP_VV, c] = stack(tile['v'][sl]).astype(BF16)
            opb_sc[wr, OP_BBT, c] = stack(b_c * e_e).T.astype(BF16)
            opb_sc[wr, OP_KBT, c] = stack(k_c * e_e).T.astype(BF16)
            el_sc[wr, c] = jnp.broadcast_to(jnp.exp(le), (SUBLANES, LANES))
        return run

    def substages():
        ops = [dict(c=c) for c in chunks]

        def gram():
            for o in ops:
                c = o['c']
                ar = jnp.concatenate([opf_sc[rd, OP_AH, c], opf_sc[rd, OP_RH, c]], axis=0).astype(BF16)
                gb = _mm_nt(ar, opb_sc[rd, OP_BT, c])
                gk = _mm_nt(ar, opb_sc[rd, OP_KT, c])
                o['a_ab'] = jnp.where(strict, gb[0:LANES], 0.0)
                o['a_rb'] = jnp.where(incl, gb[LANES:], 0.0).astype(BF16)
                o['a_ak'] = jnp.where(strict, gk[0:LANES], 0.0).astype(BF16)
                o['a_rk'] = jnp.where(incl, gk[LANES:], 0.0).astype(BF16)

        def values():
            for o in ops:
                vv = opb_sc[rd, OP_VV, o['c']]
                o['x'] = _mm(o['a_ak'], vv)
                o['yv'] = _mm(o['a_rk'], vv)
                o['cv'] = _mm(opb_sc[rd, OP_KBT, o['c']], vv)
                o['t'] = eye_f + jnp.where(off_masks[0], o['a_ab'], 0.0)

        def level_a(off):
            def run():
                for o in ops:
                    o['ta'] = _mm(o['t'], jnp.where(off, o['a_ab'], 0.0))
            return run

        def level_b():
            for o in ops:
                o['t'] = o['t'] + _mm(o['ta'], o['t'])

        def apply_inverse():
            for o in ops:
                ah_x = jnp.concatenate([opf_sc[rd, OP_AH, o['c']], o['x']], axis=1)
                o['w'] = _mm(o['t'], ah_x).astype(BF16)

        def emit():
            for o in ops:
                c = o['c']
                qy = _mm(o['a_rb'], o['w'])
                mc = _mm(opb_sc[rd, OP_BBT, c], o['w'])
                diag = jnp.where(eye, el_sc[rd, c][0:1, :], 0.0)
                qm_sc[c] = jnp.concatenate([opf_sc[rd, OP_RH, c] + qy[:, 0:LANES], diag + mc[:, 0:LANES]],
                                           axis=0).astype(BF16)
                yin_sc[c] = qy[:, LANES:] + o['yv']
                cc_sc[c] = mc[:, LANES:] + o['cv']

        steps = [gram, values]
        for off in off_masks[1:]:
            steps += [level_a(off), level_b]
        return steps + [apply_inverse, emit]

    chain = dict(s=jnp.where(chain_tile == 0, 0.0, s_sc[...]), ys=[])

    def link(c):
        def run():
            res = _mm(qm_sc[c], chain['s'])
            y2 = res[0:LANES] + yin_sc[c]
            chain['ys'].append(y2[0:CHUNK] + y2[CHUNK:2 * CHUNK])
            chain['s'] = res[LANES:] + cc_sc[c]
        return run

    def finish_chain():
        s_sc[...] = chain['s']
        y = jnp.concatenate(chain['ys'], axis=0)
        mean = head_sum(y) * (1.0 / HEAD_DIM)
        dlt = y - mean
        var = head_sum(dlt * dlt) * (1.0 / HEAD_DIM)
        yn = dlt * lax.rsqrt(var + GROUPNORM_EPS) * gn_sc[0:1, :] + gn_sc[1:2, :]
        yr_ref[...] = (yn + bg_sc[0]) * bg_sc[1]

    def cast(j):
        def run():
            cast_dst[j][...] = cast_src[j][...].astype(BF16)
        return run

    steps = substages()
    fill = [prep_tile_wide]
    for c in chunks:
        fill += [link(c), prep(c)] + [cast(j) for j in range(n_cast)[c::len(chunks)]]
    fill += [finish_chain]
    gaps = len(steps) - 1
    for i, run in enumerate(steps):
        run()
        for item in fill[i * len(fill) // gaps:(i + 1) * len(fill) // gaps] if i < gaps else ():
            item()
    bg_sc[...] = bgp_sc[...]
    gn_sc[...] = gnp_sc[...]
    bgp_sc[0] = tile['bonus']
    bgp_sc[1] = tile['g']
    gnp_sc[0:1, :] = tile['gnw']
    gnp_sc[1:2, :] = tile['gnb']


RWKV_TILE = 512
RWKV_STAGES = 3


def _rwkv_work_steps(t, c):
    return (c // LANES) * (t // RWKV_TILE)


def _cast_job(src, lo, hi, steps):
    rows = hi - lo
    count = max(n for n in range(1, steps + 1) if rows % n == 0 and (rows // n) % 16 == 0)
    blk = rows // count
    assert lo % blk == 0
    return (src, blk, lo // blk, count)


def _rwkv(g_main, g_lr, v_first, w2s, pvec, has_vres, casts=()):
    t = g_main.shape[0]
    c = pvec.shape[1]
    tt = RWKV_TILE
    ntile = t // tt
    n_work = _rwkv_work_steps(t, c)
    nsteps = n_work + RWKV_STAGES - 1
    col0 = c // LANES
    nch = tt // CHUNK

    def prep_of(s):
        ss = jnp.minimum(s, n_work - 1)
        return ss // ntile, ss % ntile

    def chain_of(s):
        cs = jnp.maximum(s - (RWKV_STAGES - 1), 0)
        return cs // ntile, cs % ntile

    def in_map(col_mult):
        return lambda s: (prep_of(s)[1], col_mult * col0 + prep_of(s)[0])

    in_specs = [pl.BlockSpec((tt, LANES), in_map(1)),
                pl.BlockSpec((tt, LANES), in_map(2)),
                pl.BlockSpec((tt, LANES), in_map(3)),
                pl.BlockSpec((tt, 2 * LR_PAD), lambda s: (prep_of(s)[1], 0))]
    args = [g_main, g_main, g_main, g_lr]
    if has_vres:
        in_specs.append(pl.BlockSpec((tt, LANES), in_map(0)))
        args.append(v_first)
    in_specs += [pl.BlockSpec((768, LANES), lambda s: (0, prep_of(s)[0])),
                 pl.BlockSpec((PV_ROWS, LANES), lambda s: (0, prep_of(s)[0]))]
    args += [w2s, pvec]
    yr_spec = pl.BlockSpec((tt, LANES), lambda s: (chain_of(s)[1], chain_of(s)[0]))
    out_shape = jax.ShapeDtypeStruct((t, c), F32)
    if has_vres:
        out_specs, out_shapes = [yr_spec], [out_shape]
    else:
        out_specs, out_shapes = [yr_spec, pl.BlockSpec((tt, LANES), in_map(0))], [out_shape, out_shape]
    for src, rows, first, count in casts:
        blk = (rows, src.shape[1])
        in_specs.append(pl.BlockSpec(blk, lambda s, first=first, count=count: (first + jnp.minimum(s, count - 1), 0)))
        args.append(src)
        out_specs.append(pl.BlockSpec(blk, lambda s, count=count: (jnp.minimum(s, count - 1), 0)))
        out_shapes.append(jax.ShapeDtypeStruct((rows * count, src.shape[1]), BF16))
    return pl.pallas_call(
        functools.partial(_rwkv_kernel, tt=tt, ntile=ntile, nsteps=nsteps, has_vres=has_vres, n_cast=len(casts)),
        grid=(nsteps,),
        in_specs=in_specs,
        out_specs=out_specs,
        out_shape=out_shapes,
        scratch_shapes=[pltpu.VMEM((LANES, LANES), F32),
                        pltpu.VMEM((SUBLANES, LANES), F32),
                        pltpu.VMEM((SUBLANES, LR_PAD), F32),
                        pltpu.VMEM((nch, 2 * LANES, LANES), BF16),
                        pltpu.VMEM((nch, LANES, LANES), F32),
                        pltpu.VMEM((nch, LANES, LANES), F32),
                        pltpu.VMEM((2, tt, LANES), F32),
                        pltpu.VMEM((SUBLANES, LANES), F32),
                        pltpu.VMEM((2, 2, nch, LANES, LANES), F32),
                        pltpu.VMEM((2, 5, nch, LANES, LANES), BF16),
                        pltpu.VMEM((2, nch, SUBLANES, LANES), F32),
                        pltpu.VMEM((2, tt, LANES), F32),
                        pltpu.VMEM((SUBLANES, LANES), F32)],
        compiler_params=_cparams(("arbitrary",)),
        name="rwkv7_vres" if has_vres else "rwkv7",
    )(*args)


def _mix_kernel(*refs, tm, has_router):
    if has_router:
        (x_ref, pp_ref, halo_ref, glp_ref, glr_ref, yr_ref, pw_ref, ps_ref, po_ref, ro_ref, mo_ref,
         g1_ref, sc2_ref, sh2_ref, rt_ref, xo_ref, h2_ref, idx_ref, gt_ref, ext) = refs
    else:
        (x_ref, pp_ref, halo_ref, glp_ref, glr_ref, yr_ref, pw_ref, ps_ref, po_ref, ro_ref, mo_ref,
         g1_ref, sc2_ref, sh2_ref, xo_ref, h2_ref, ext) = refs
    i = pl.program_id(0)
    ext[0:POOL_HALO, :] = jnp.where(i == 0, 0.0, halo_ref[...].astype(F32))
    ext[POOL_HALO:POOL_HALO + tm, :] = pp_ref[...].astype(F32)
    pos = i * tm + lax.broadcasted_iota(jnp.int32, (tm, 1), 0)
    gw = ext.shape[1] // len(POOL_WINDOWS)
    parts = []
    for gi, win in enumerate(POOL_WINDOWS):
        cs = slice(gi * gw, (gi + 1) * gw)
        cur = ext[POOL_HALO:POOL_HALO + tm, cs]
        s = cur
        for j in range(1, win):
            s = s + ext[POOL_HALO - j:POOL_HALO - j + tm, cs]
        cnt = jnp.minimum(pos + 1, win).astype(F32)
        pooled = s / cnt - cur
        parts.append(_mm(pooled, pw_ref[gi]))
    mixed = jnp.concatenate(parts, axis=-1) * ps_ref[...]
    y_pool = _mm(mixed, po_ref[...])
    y_rwkv = _mm(yr_ref[...], ro_ref[...])
    merged = (jax.nn.sigmoid(glp_ref[...].astype(F32)) * y_pool
              + jax.nn.sigmoid(glr_ref[...].astype(F32)) * y_rwkv)
    xn = x_ref[...] + g1_ref[...] * _mm(merged, mo_ref[...])
    xo_ref[...] = xn
    ms = jnp.mean(xn * xn, axis=-1, keepdims=True)
    h2 = xn * lax.rsqrt(ms + NORM_EPS) * (1.0 + sc2_ref[...]) + sh2_ref[...]
    if has_router:
        slab = h2.shape[1] // LANES
        for a in range(slab):
            h2_ref[pl.ds(a, tm, stride=slab), :] = h2[:, a * LANES:(a + 1) * LANES]
        rt = rt_ref[...]
        hi, lo = _split(h2)
        rhi, rlo = _split(rt)
        logits = (jnp.dot(hi, rhi, preferred_element_type=F32) + jnp.dot(lo, rhi, preferred_element_type=F32)
                  + jnp.dot(hi, rlo, preferred_element_type=F32))
        n_exp = 8
        lane = lax.broadcasted_iota(jnp.int32, logits.shape, 1)
        lane_f = lane.astype(F32)
        big = float(LANES)
        lg = jnp.where(lane < n_exp, logits, -jnp.inf)
        m1 = jnp.max(lg, axis=-1, keepdims=True)
        i1 = jnp.min(jnp.where(lg == m1, lane_f, big), axis=-1, keepdims=True)
        lg2 = jnp.where(lane_f == i1, -jnp.inf, lg)
        m2 = jnp.max(lg2, axis=-1, keepdims=True)
        i2 = jnp.min(jnp.where(lg2 == m2, lane_f, big), axis=-1, keepdims=True)
        e2 = jnp.exp(m2 - m1)
        p1 = 1.0 / (1.0 + e2)
        p2 = e2 / (1.0 + e2)
        idx_ref[...] = jnp.where(lane == 0, i1, jnp.where(lane == 1, i2, 0.0)).astype(jnp.int32)
        gt_ref[...] = jnp.where(lane == 0, p1, jnp.where(lane == 1, p2, 0.0))
    else:
        h2_ref[...] = h2.astype(h2_ref.dtype)


def _mix(x, g_main, yr, pool_w, pool_scale, pool_out, rk_out, mix_out, gate1, scale2, shift2, router_pad):
    t, d = x.shape
    c = yr.shape[1]
    has_router = router_pad is not None
    tm = 256 if has_router else 512
    once = dict(pipeline_mode=pl.Buffered(1))
    halo_blocks = tm // POOL_HALO
    cb_glp = (4 * c) // d
    in_specs = [pl.BlockSpec((tm, d), lambda i: (i, 0)),
                pl.BlockSpec((tm, c), lambda i: (i, 0)),
                pl.BlockSpec((POOL_HALO, c), lambda i: (jnp.maximum(i * halo_blocks - 1, 0), 0)),
                pl.BlockSpec((tm, d), lambda i: (i, cb_glp)),
                pl.BlockSpec((tm, d), lambda i: (i, cb_glp + 1)),
                pl.BlockSpec((tm, c), lambda i: (i, 0)),
                pl.BlockSpec(pool_w.shape, lambda i: (0, 0, 0), **once),
                pl.BlockSpec((1, c), lambda i: (0, 0)),
                pl.BlockSpec((c, d), lambda i: (0, 0), **once),
                pl.BlockSpec((c, d), lambda i: (0, 0), **once),
                pl.BlockSpec((d, d), lambda i: (0, 0), **once),
                pl.BlockSpec((1, d), lambda i: (0, 0)),
                pl.BlockSpec((1, d), lambda i: (0, 0)),
                pl.BlockSpec((1, d), lambda i: (0, 0))]
    args = [x, g_main, g_main, g_main, g_main, yr, pool_w, pool_scale, pool_out, rk_out, mix_out,
            gate1, scale2, shift2]
    out_specs = [pl.BlockSpec((tm, d), lambda i: (i, 0))]
    out_shape = [jax.ShapeDtypeStruct((t, d), F32)]
    if has_router:
        slab = d // LANES
        out_specs.append(pl.BlockSpec((tm * slab, LANES), lambda i: (i, 0)))
        out_shape.append(jax.ShapeDtypeStruct((t * slab, LANES), F32))
    else:
        out_specs.append(pl.BlockSpec((tm, d), lambda i: (i, 0)))
        out_shape.append(jax.ShapeDtypeStruct((t, d), BF16))
    if has_router:
        in_specs.append(pl.BlockSpec((d, LANES), lambda i: (0, 0)))
        args.append(router_pad)
        out_specs += [pl.BlockSpec((tm, LANES), lambda i: (i, 0)), pl.BlockSpec((tm, LANES), lambda i: (i, 0))]
        out_shape += [jax.ShapeDtypeStruct((t, LANES), jnp.int32), jax.ShapeDtypeStruct((t, LANES), F32)]
    return pl.pallas_call(
        functools.partial(_mix_kernel, tm=tm, has_router=has_router),
        grid=(t // tm,),
        in_specs=in_specs,
        out_specs=out_specs,
        out_shape=out_shape,
        scratch_shapes=[pltpu.VMEM((POOL_HALO + tm, c), F32)],
        compiler_params=_cparams(("arbitrary",), VMEM_LIMIT_MIX),
        name="mix_router" if has_router else "mix",
    )(*args)


def _ffn_kernel(h_ref, x_ref, g_ref, w1_ref, w3_ref, w2_ref, o_ref, acc):
    f = pl.program_id(1)

    @pl.when(f == 0)
    def _():
        acc[...] = jnp.zeros_like(acc)

    h = h_ref[...]
    a = jnp.dot(h, w1_ref[...].astype(BF16), preferred_element_type=F32)
    b = jnp.dot(h, w3_ref[...].astype(BF16), preferred_element_type=F32)
    u = (a * jax.nn.sigmoid(a) * b).astype(BF16)
    acc[...] += jnp.dot(u, w2_ref[...].astype(BF16), preferred_element_type=F32)

    @pl.when(f == pl.num_programs(1) - 1)
    def _():
        o_ref[...] = x_ref[...] + g_ref[...] * acc[...]


def _ffn(h2, x, gate2, w1, w3, w2):
    t, d = x.shape
    ff = w1.shape[1]
    tm, tf = 512, 512
    return pl.pallas_call(
        _ffn_kernel,
        grid=(t // tm, ff // tf),
        in_specs=[pl.BlockSpec((tm, d), lambda i, f: (i, 0)),
                  pl.BlockSpec((tm, d), lambda i, f: (i, 0)),
                  pl.BlockSpec((1, d), lambda i, f: (0, 0)),
                  pl.BlockSpec((d, tf), lambda i, f: (0, f)),
                  pl.BlockSpec((d, tf), lambda i, f: (0, f)),
                  pl.BlockSpec((tf, d), lambda i, f: (f, 0))],
        out_specs=pl.BlockSpec((tm, d), lambda i, f: (i, 0)),
        out_shape=jax.ShapeDtypeStruct((t, d), F32),
        scratch_shapes=[pltpu.VMEM((tm, d), F32)],
        compiler_params=_cparams(("arbitrary", "arbitrary")),
        name="ffn_dense",
    )(h2, x, gate2, w1, w3, w2)


def _slab_copy(src_hbm, src_row, dst, slot, dst_row, slab, sem):
    src = src_hbm.at[pl.ds(pl.multiple_of(src_row * slab, slab), slab), :]
    return pltpu.make_async_copy(src, dst.at[slot, pl.ds(pl.multiple_of(dst_row * slab, slab), slab), :],
                                 sem.at[slot])


def _moe_kernel(te_ref, tr_ref, na_ref, tok_ref, h_hbm, w1_ref, w3_ref, w2l_ref, w2h_ref,
                o_ref, xg, xb, acc, sem, *, tm, n_lo):
    b = pl.program_id(0)
    f = pl.program_id(1)
    nf = pl.num_programs(1)
    n_act = na_ref[0]
    slab = xb.shape[1] // LANES

    def start_gather(tile):
        def body(s, carry):
            _slab_copy(h_hbm, tok_ref[tile * tm + s], xg, 0, s, slab, sem).start()
            return carry
        lax.fori_loop(0, tm, body, 0, unroll=8)

    @pl.when(jnp.logical_and(f == 0, b < n_act))
    def _():
        @pl.when(b == 0)
        def _():
            start_gather(0)

        pltpu.make_async_copy(h_hbm.at[pl.ds(0, tm * slab), :], xg.at[0], sem.at[0]).wait()
        for a in range(slab):
            xb[:, a * LANES:(a + 1) * LANES] = xg[0, pl.ds(a, tm, stride=slab), :].astype(BF16)
        acc[...] = jnp.zeros_like(acc)

        @pl.when(b + 1 < n_act)
        def _():
            start_gather(b + 1)

    def swiglu_step(w2_ref, rows):
        x = xb[0:rows, :]
        a_ = jnp.dot(x, w1_ref[0].astype(BF16), preferred_element_type=F32)
        b_ = jnp.dot(x, w3_ref[0].astype(BF16), preferred_element_type=F32)
        u = (a_ * jax.nn.sigmoid(a_) * b_).astype(BF16)
        acc[0:rows, :] += jnp.dot(u, w2_ref[0].astype(BF16), preferred_element_type=F32)

    in_lo = te_ref[b] < n_lo
    for rows in range(MOE_ROW_STEP, tm + 1, MOE_ROW_STEP):
        has_rows = jnp.logical_and(b < n_act, tr_ref[b] == rows)

        @pl.when(jnp.logical_and(has_rows, in_lo))
        def _():
            swiglu_step(w2l_ref, rows)

        @pl.when(jnp.logical_and(has_rows, jnp.logical_not(in_lo)))
        def _():
            swiglu_step(w2h_ref, rows)

    @pl.when(f == nf - 1)
    def _():
        res = jnp.where(b < n_act, acc[...], 0.0)
        for a in range(slab):
            o_ref[pl.ds(a, tm, stride=slab), :] = res[:, a * LANES:(a + 1) * LANES]


MOE_ROW_STEP = 128


def _moe(h2_slab, tile_e, tile_rows, n_active, slot_tok, w1, w3, w2_lo, w2_hi, tm, n_tiles):
    n_exp, d, ff = w1.shape
    n_lo = w2_lo.shape[0]
    tf = 1024
    nf = ff // tf

    def w13_map(b, f, te, tr, na, tok):
        return (te[b], 0, jnp.where(b < na[0], f, nf - 1))

    def w2_map(lo):
        def index(b, f, te, tr, na, tok):
            e = te[b]
            mine = jnp.logical_and(b < na[0], (e < n_lo) if lo else (e >= n_lo))
            e_loc = jnp.minimum(e, n_lo - 1) if lo else jnp.maximum(e - n_lo, 0)
            return (e_loc, jnp.where(mine, f, (nf - 1) if lo else 0), 0)
        return index

    grid_spec = pltpu.PrefetchScalarGridSpec(
        num_scalar_prefetch=4,
        grid=(n_tiles, nf),
        in_specs=[pl.BlockSpec(memory_space=pl.ANY),
                  pl.BlockSpec((1, d, tf), w13_map), pl.BlockSpec((1, d, tf), w13_map),
                  pl.BlockSpec((1, tf, d), w2_map(True)), pl.BlockSpec((1, tf, d), w2_map(False))],
        out_specs=pl.BlockSpec((tm * (d // LANES), LANES), lambda b, f, te, tr, na, tok: (b, 0)),
        scratch_shapes=[pltpu.VMEM((1, tm * (d // LANES), LANES), F32),
                        pltpu.VMEM((tm, d), BF16),
                        pltpu.VMEM((tm, d), F32),
                        pltpu.SemaphoreType.DMA((1,))],
    )
    return pl.pallas_call(
        functools.partial(_moe_kernel, tm=tm, n_lo=n_lo),
        grid_spec=grid_spec,
        out_shape=jax.ShapeDtypeStruct((n_tiles * tm * (d // LANES), LANES), F32),
        compiler_params=_cparams(("arbitrary", "arbitrary")),
        name="moe_experts",
    )(tile_e, tile_rows, n_active, slot_tok, h2_slab, w1, w3, w2_lo, w2_hi)


def _comb_kernel(dest_ref, yb_hbm, x_ref, gt_ref, g2_ref, fg_ref, o_ref, buf, sem, *, tm):
    i = pl.program_id(0)
    n = pl.num_programs(0)
    slot = i % 2
    nrow = TOP_K * tm
    slab = o_ref.shape[1] // LANES

    def start_gather(tile, slt):
        def body(s, carry):
            _slab_copy(yb_hbm, dest_ref[tile * nrow + s], buf, slt, s, slab, sem).start()
            return carry
        lax.fori_loop(0, nrow, body, 0, unroll=8)

    @pl.when(i == 0)
    def _():
        start_gather(0, 0)

    @pl.when(i + 1 < n)
    def _():
        start_gather(i + 1, 1 - slot)

    pltpu.make_async_copy(yb_hbm.at[pl.ds(0, nrow * slab), :], buf.at[slot], sem.at[slot]).wait()
    gt = gt_ref[...]
    p0 = gt[:, 0:1]
    p1 = gt[:, 1:2]
    cols = []
    for a in range(slab):
        y0 = buf[slot, pl.ds(a, tm, stride=TOP_K * slab), :]
        y1 = buf[slot, pl.ds(slab + a, tm, stride=TOP_K * slab), :]
        cols.append(y0 * p0 + y1 * p1)
    fmoe = jnp.concatenate(cols, axis=-1)
    xn = x_ref[...] + g2_ref[...] * fmoe
    ms = jnp.mean(xn * xn, axis=-1, keepdims=True)
    o_ref[...] = xn * lax.rsqrt(ms + NORM_EPS) * fg_ref[...]


def _combine(dest, yb_slab, x, gates, gate2, final_gain):
    t, d = x.shape
    tm = 256
    grid_spec = pltpu.PrefetchScalarGridSpec(
        num_scalar_prefetch=1,
        grid=(t // tm,),
        in_specs=[pl.BlockSpec(memory_space=pl.ANY),
                  pl.BlockSpec((tm, d), lambda i, dr: (i, 0)),
                  pl.BlockSpec((tm, LANES), lambda i, dr: (i, 0)),
                  pl.BlockSpec((1, d), lambda i, dr: (0, 0)),
                  pl.BlockSpec((1, d), lambda i, dr: (0, 0))],
        out_specs=pl.BlockSpec((tm, d), lambda i, dr: (i, 0)),
        scratch_shapes=[pltpu.VMEM((2, TOP_K * tm * (d // LANES), LANES), F32),
                        pltpu.SemaphoreType.DMA((2,))],
    )
    return pl.pallas_call(
        functools.partial(_comb_kernel, tm=tm),
        grid_spec=grid_spec,
        out_shape=jax.ShapeDtypeStruct((t, d), F32),
        compiler_params=_cparams(("arbitrary",)),
        name="moe_combine",
    )(dest, yb_slab, x, gates, gate2, final_gain)


def _pad_rows(w, rows):
    return jnp.pad(w, ((0, rows - w.shape[0]), (0, 0)))


def _lowrank_weights(l, rk_lerp, rk_w1, rk_a1, rk_g1, rk_vmu, rk_v1):
    mus = [rk_lerp[l, 0], rk_lerp[l, 1], rk_lerp[l, 2]]
    ws = [rk_w1[l], rk_a1[l], rk_g1[l]]
    if l > 0:
        mus.append(rk_vmu[l - 1])
        ws.append(rk_v1[l - 1])
    wa = jnp.concatenate([(1.0 - m)[:, None] * w for m, w in zip(mus, ws)], axis=1)
    wb = jnp.concatenate([m[:, None] * w for m, w in zip(mus, ws)], axis=1)
    wa = jnp.pad(wa, ((0, 0), (0, LR_PAD - wa.shape[1])))
    wb = jnp.pad(wb, ((0, 0), (0, LR_PAD - wb.shape[1])))
    return jnp.concatenate([wa, wb], axis=1).astype(BF16)


def _second_stage_weights(l, rk_w2, rk_a2, rk_g2, rk_v2):
    c = rk_w2.shape[2]
    w2 = _pad_rows(rk_w2[l], 128)
    a2 = jnp.concatenate([jnp.zeros((64, c), F32), rk_a2[l]], axis=0)
    g2 = _pad_rows(rk_g2[l], 256)
    if l > 0:
        v2 = jnp.concatenate([jnp.zeros((160, c), F32), rk_v2[l - 1], jnp.zeros((64, c), F32)], axis=0)
    else:
        v2 = jnp.zeros((256, c), F32)
    return jnp.concatenate([w2, a2, g2, v2], axis=0).astype(BF16)


def _route(idx, tm, n_exp, n_tiles):
    n_assign = idx.size
    flat_e = idx.reshape(-1)
    onehot = (flat_e[:, None] == jnp.arange(n_exp, dtype=jnp.int32)[None, :]).astype(jnp.int32)
    csum = jnp.cumsum(onehot, axis=0)
    rank = jnp.take_along_axis(csum, flat_e[:, None], axis=1)[:, 0] - 1
    counts = csum[-1]
    padded = (counts + tm - 1) // tm * tm
    pad_end = jnp.cumsum(padded)
    pad_start = pad_end - padded
    dest = (pad_start[flat_e] + rank).astype(jnp.int32)
    slot_tok = jnp.zeros((n_tiles * tm,), jnp.int32).at[dest].set(
        jnp.arange(n_assign, dtype=jnp.int32) // TOP_K)
    n_active = (pad_end[-1] // tm).astype(jnp.int32)
    tile_ids = jnp.minimum(jnp.arange(n_tiles, dtype=jnp.int32), jnp.maximum(n_active - 1, 0))
    tile_e = jnp.minimum(jnp.searchsorted(pad_end, tile_ids * tm, side='right'), n_exp - 1).astype(jnp.int32)
    filled = counts[tile_e] - (tile_ids * tm - pad_start[tile_e])
    tile_rows = jnp.clip((filled + MOE_ROW_STEP - 1) // MOE_ROW_STEP * MOE_ROW_STEP, MOE_ROW_STEP, tm)
    return dest, slot_tok, tile_e, tile_rows.astype(jnp.int32), n_active.reshape(1)


def kernel(x, c, ada_w, ada_b, w_in, pool_w, pool_scale, pool_out, rk_shift, rk_lerp, rk_w0, rk_w1, rk_w2, rk_a0, rk_a1, rk_a2, rk_g1, rk_g2, rk_kk, rk_ka, rk_rk, rk_gn_w, rk_gn_b, rk_vmu, rk_v0, rk_v1, rk_v2, rk_out, mix_out, ffn_w1, ffn_w3, ffn_w2, router, moe_w1, moe_w3, moe_w2, final_gain):
    bsz, t, d = x.shape
    depth = ada_w.shape[0]
    cw = rk_w0.shape[1]
    n_exp = router.shape[2]
    assert bsz == 1 and depth == 2 and cw % LANES == 0
    xs = x.reshape(t, d)
    mod = _ada(c.reshape(d, 1), ada_w, ada_b)

    def mod_row(l, i):
        return mod[l, :, i * d:(i + 1) * d]

    v_first = None
    out = None
    for l in range(depth):
        w_lr = _lowrank_weights(l, rk_lerp, rk_w1, rk_a1, rk_g1, rk_vmu, rk_v1)
        if l == 0:
            g_main, g_lr = _in_proj(xs, mod_row(l, 1), mod_row(l, 0), w_in, l, w_lr)
        else:
            g_main, g_lr = _in_proj(xs, mod_row(l, 1), mod_row(l, 0), w_in_next.reshape(1, d, -1), 0, w_lr)
        w2s = _second_stage_weights(l, rk_w2, rk_a2, rk_g2, rk_v2)
        zeros_c = jnp.zeros((cw,), F32)
        rows = [rk_w0[l], rk_a0[l], rk_v0[l - 1] if l > 0 else zeros_c, rk_kk[l], rk_ka[l],
                rk_gn_w[l], rk_gn_b[l], rk_shift[l, 0], rk_shift[l, 1], rk_shift[l, 2],
                rk_rk[l].reshape(cw)]
        pvec = _pad_rows(jnp.stack(rows, axis=0), PV_ROWS)
        steps = _rwkv_work_steps(t, cw)
        mixer = [w.reshape(-1, w.shape[-1]) for w in (pool_w, pool_out, rk_out, mix_out)]
        mixer_casts = [_cast_job(w, l * (w.shape[0] // depth), (l + 1) * (w.shape[0] // depth), steps)
                       for w in mixer]
        if l == 0:
            dense = [ffn_w1[0], ffn_w3[0], ffn_w2[0]]
            experts = [moe_w1[0].reshape(-1, moe_w1.shape[-1]), moe_w3[0].reshape(-1, moe_w3.shape[-1]),
                       moe_w2[0].reshape(-1, moe_w2.shape[-1])]
            e1, e3, e2 = experts
            casts = ([_cast_job(w, 0, w.shape[0], steps) for w in dense]
                     + [_cast_job(e1, 0, e1.shape[0], steps), _cast_job(e2, 0, e2.shape[0] // 2, steps)]
                     + [_cast_job(w_in.reshape(-1, w_in.shape[-1]), d, 2 * d, steps)] + mixer_casts)
            yr, v_first, *cast_out = _rwkv(g_main, g_lr, None, w2s, pvec, has_vres=False, casts=casts)
            dense_bf, (e1_bf, e2_lo), w_in_next, mixer_bf = cast_out[:3], cast_out[3:5], cast_out[5], cast_out[6:]
        else:
            casts = [_cast_job(e3, 0, e3.shape[0], steps),
                     _cast_job(e2, e2.shape[0] // 2, e2.shape[0], steps)] + mixer_casts
            yr, e3_bf, e2_hi, *mixer_bf = _rwkv(g_main, g_lr, v_first, w2s, pvec, has_vres=True, casts=casts)
        is_moe = (l % 2 == 1)
        router_pad = None
        if is_moe:
            router_pad = jnp.pad(router[l // 2], ((0, 0), (0, LANES - n_exp)))
        res = _mix(xs, g_main, yr, mixer_bf[0].reshape(pool_w.shape[1:]), pool_scale[l].reshape(1, -1),
                   mixer_bf[1], mixer_bf[2], mixer_bf[3], mod_row(l, 2), mod_row(l, 4), mod_row(l, 3), router_pad)
        if not is_moe:
            x_new, h2 = res
            xs = _ffn(h2, x_new, mod_row(l, 5), *dense_bf)
        else:
            x_new, h2, idx_pad, gate_pad = res
            tm_moe = 512
            n_tiles = (t * TOP_K) // tm_moe + n_exp
            dest, slot_tok, tile_e, tile_rows, n_active = _route(idx_pad[:, :TOP_K], tm_moe, n_exp, n_tiles)
            n_lo = n_exp // 2
            yb_slab = _moe(h2, tile_e, tile_rows, n_active, slot_tok,
                           e1_bf.reshape(moe_w1.shape[1:]), e3_bf.reshape(moe_w3.shape[1:]),
                           e2_lo.reshape((n_lo,) + moe_w2.shape[2:]),
                           e2_hi.reshape((n_exp - n_lo,) + moe_w2.shape[2:]), tm_moe, n_tiles)
            out = _combine(dest, yb_slab, x_new, gate_pad, mod_row(l, 5), final_gain.reshape(1, d))
    return out.reshape(bsz, t, d)
```

```python
import functools

import jax
import jax.numpy as jnp
from jax import lax
from jax.experimental import pallas as pl
from jax.experimental.pallas import tpu as pltpu

F32 = jnp.float32
BF16 = jnp.bfloat16

NORM_EPS = 1e-6
GROUPNORM_EPS = 64e-5
L2_EPS = 1e-12
POOL_WINDOWS = (2, 4, 8, 16)
HEAD_DIM = 64
TOP_K = 2
LANES = 128
SUBLANES = 8
CHUNK = 64
LR_PAD = 384
POOL_HALO = 16
VMEM_LIMIT = 56 * 1024 * 1024
VMEM_LIMIT_HIGH = 60 * 1024 * 1024


def _cparams(sem, vmem_limit=VMEM_LIMIT):
    return pltpu.CompilerParams(dimension_semantics=sem, vmem_limit_bytes=vmem_limit)


def _mm(a, b):
    return jnp.dot(a.astype(BF16), b.astype(BF16), preferred_element_type=F32)


def _mm_nt(a, b):
    return lax.dot_general(a.astype(BF16), b.astype(BF16), (((1,), (1,)), ((), ())),
                           preferred_element_type=F32)


def _split(x):
    hi = x.astype(BF16)
    lo = (x - hi.astype(F32)).astype(BF16)
    return hi, lo


def _ada_kernel(c_ref, w_ref, b_ref, o_ref):
    c = c_ref[...]
    act = c * jax.nn.sigmoid(c)
    o_ref[0] = jnp.sum(act * w_ref[0], axis=0, keepdims=True) + b_ref[0]


def _ada(c_col, ada_w, ada_b):
    depth, d, n = ada_w.shape
    tn = 1024
    return pl.pallas_call(
        _ada_kernel,
        grid=(depth, n // tn),
        in_specs=[pl.BlockSpec((d, 1), lambda l, j: (0, 0)),
                  pl.BlockSpec((1, d, tn), lambda l, j: (l, 0, j)),
                  pl.BlockSpec((1, 1, tn), lambda l, j: (l, 0, j))],
        out_specs=pl.BlockSpec((1, 1, tn), lambda l, j: (l, 0, j)),
        out_shape=jax.ShapeDtypeStruct((depth, 1, n), F32),
        compiler_params=_cparams(("arbitrary", "arbitrary")),
        name="ada_gemv",
    )(c_col, ada_w, ada_b.reshape(depth, 1, n))


def _in_kernel(x_ref, sc_ref, sh_ref, w_ref, wlr_ref, o_ref, olr_ref, h_sc):
    @pl.when(pl.program_id(1) == 0)
    def _():
        x = x_ref[...]
        ms = jnp.mean(x * x, axis=-1, keepdims=True)
        h = x * lax.rsqrt(ms + NORM_EPS) * (1.0 + sc_ref[...]) + sh_ref[...]
        hb = h.astype(BF16)
        h_sc[...] = hb
        olr_ref[...] = jnp.dot(hb, wlr_ref[...], preferred_element_type=F32)

    o_ref[...] = jnp.dot(h_sc[...], w_ref[0].astype(BF16), preferred_element_type=F32).astype(o_ref.dtype)


def _in_proj(x, scale, shift, w_in, layer, w_lr):
    t, d = x.shape
    n = w_in.shape[2]
    nlr = w_lr.shape[1]
    tm, tn = 1024, 1024
    return pl.pallas_call(
        _in_kernel,
        grid=(t // tm, n // tn),
        in_specs=[pl.BlockSpec((tm, d), lambda i, j: (i, 0)),
                  pl.BlockSpec((1, d), lambda i, j: (0, 0)),
                  pl.BlockSpec((1, d), lambda i, j: (0, 0)),
                  pl.BlockSpec((1, d, tn), lambda i, j: (layer, 0, j)),
                  pl.BlockSpec((d, nlr), lambda i, j: (0, 0))],
        out_specs=[pl.BlockSpec((tm, tn), lambda i, j: (i, j)),
                   pl.BlockSpec((tm, nlr), lambda i, j: (i, 0))],
        out_shape=[jax.ShapeDtypeStruct((t, n), BF16), jax.ShapeDtypeStruct((t, nlr), F32)],
        scratch_shapes=[pltpu.VMEM((tm, d), BF16)],
        compiler_params=_cparams(("arbitrary", "arbitrary"), VMEM_LIMIT_HIGH),
        name="in_proj",
    )(x, scale, shift, w_in, w_lr)


PV_W0, PV_A0, PV_V0, PV_KK, PV_KA, PV_GNW, PV_GNB, PV_MUR, PV_MUK, PV_MUV, PV_RK = range(11)
PV_ROWS = 16


def _rwkv_kernel(*refs, tt, ntile, nsteps, has_vres, n_cast):
    n_in = 7 if has_vres else 6
    n_out = 1 if has_vres else 2
    ins = refs[:n_in]
    cast_src = refs[n_in:n_in + n_cast]
    outs = refs[n_in + n_cast:n_in + n_cast + n_out]
    cast_dst = refs[n_in + n_cast + n_out:n_in + 2 * n_cast + n_out]
    (s_sc, c_rkv, c_lr, qm_sc, yin_sc, cc_sc, bg_sc, gn_sc,
     opf_sc, opb_sc, el_sc, bgp_sc, gnp_sc) = refs[n_in + 2 * n_cast + n_out:]
    if has_vres:
        (r_ref, k_ref, v_ref, lr_ref, vf_ref, w2_ref, pv_ref), (yr_ref,) = ins, outs
    else:
        (r_ref, k_ref, v_ref, lr_ref, w2_ref, pv_ref), (yr_ref, vfo_ref) = ins, outs
    step = pl.program_id(0)
    n_work = nsteps - 2
    is_drain = step >= n_work
    prep_tile = jnp.minimum(step, n_work - 1) % ntile
    chain_tile = jnp.maximum(step - 2, 0) % ntile
    wr = step % 2
    rd = 1 - wr
    chunks = range(tt // CHUNK)

    @pl.when(step == 0)
    def _():
        for ref in (s_sc, qm_sc, yin_sc, cc_sc, bg_sc, gn_sc, opf_sc, opb_sc, el_sc, bgp_sc, gnp_sc):
            ref[...] = jnp.zeros_like(ref)

    @pl.when(prep_tile == 0)
    def _():
        c_rkv[...] = jnp.zeros_like(c_rkv)
        c_lr[...] = jnp.zeros_like(c_lr)

    li = lax.broadcasted_iota(jnp.int32, (LANES, LANES), 0)
    lj = lax.broadcasted_iota(jnp.int32, (LANES, LANES), 1)
    same_head = ((li >> 6) == (lj >> 6))
    head_ones = jnp.where(same_head, 1.0, 0.0).astype(BF16)
    strict = (li & (CHUNK - 1)) > (lj & (CHUNK - 1))
    incl = (li & (CHUNK - 1)) >= (lj & (CHUNK - 1))
    eye = li == lj
    eye_f = jnp.where(eye, 1.0, 0.0)
    off_masks = [((li >> (s + 1)) == (lj >> (s + 1))) & (((li >> s) & 1) == 1) & (((lj >> s) & 1) == 0)
                 for s in range(CHUNK.bit_length() - 1)]

    def head_sum(x):
        return _mm(x, head_ones)

    OP_RH, OP_AH = 0, 1
    OP_BT, OP_KT, OP_VV, OP_BBT, OP_KBT = range(5)
    tile = {}

    def prep_tile_wide():
        row = lax.broadcasted_iota(jnp.int32, (tt, 1), 0)

        def shift(x, carry_row):
            return jnp.where(row == 0, carry_row, pltpu.roll(x, 1, axis=0))

        pv = pv_ref[...]

        def prow(i):
            return pv[i:i + 1, :]

        r_raw, k_raw, v_raw = (ref[...].astype(F32) for ref in (r_ref, k_ref, v_ref))
        zb = lr_ref[:, LR_PAD:2 * LR_PAD]
        c_rkv_used = jnp.where(is_drain, c_rkv[4:7, :], c_rkv[0:3, :])
        c_z = jnp.where(is_drain, c_lr[1:2, :], c_lr[0:1, :])
        r_prev = shift(r_raw, c_rkv_used[0:1, :])
        k_prev = shift(k_raw, c_rkv_used[1:2, :])
        v_prev = shift(v_raw, c_rkv_used[2:3, :])
        z = lr_ref[:, 0:LR_PAD] + shift(zb, c_z)
        c_rkv[4:7, :] = c_rkv_used
        c_lr[1:2, :] = c_z
        c_rkv[0:1, :] = r_raw[tt - 1:tt, :]
        c_rkv[1:2, :] = k_raw[tt - 1:tt, :]
        c_rkv[2:3, :] = v_raw[tt - 1:tt, :]
        c_lr[0:1, :] = zb[tt - 1:tt, :]

        r = r_raw + (r_prev - r_raw) * prow(PV_MUR)
        k = k_raw + (k_prev - k_raw) * prow(PV_MUK)
        v = v_raw + (v_prev - v_raw) * prow(PV_MUV)

        z_wa = z[:, 0:LANES]
        z_gv = z[:, LANES:LR_PAD]
        w_pre = prow(PV_W0) + _mm(jnp.tanh(z_wa), w2_ref[0:128, :])
        neg = -w_pre
        softplus = jnp.maximum(neg, 0.0) + jnp.log(1.0 + jnp.exp(-jnp.abs(neg)))
        logd = -jnp.exp(-softplus - 0.5)
        a_i = jax.nn.sigmoid(prow(PV_A0) + _mm(z_wa, w2_ref[128:256, :]))
        g = _mm(jax.nn.sigmoid(z_gv), w2_ref[256:512, :])
        if has_vres:
            v = v + (vf_ref[...] - v) * jax.nn.sigmoid(prow(PV_V0) + _mm(z_gv, w2_ref[512:768, :]))
        else:
            vfo_ref[...] = v

        kk = k * prow(PV_KK)
        kk = kk / jnp.maximum(jnp.sqrt(head_sum(kk * kk)), L2_EPS)
        k2 = k * (1.0 + (a_i - 1.0) * prow(PV_KA))
        tile.update(r=r, v=v, k2=k2, a_vec=-kk, b_vec=kk * a_i, logd=logd, g=g,
                    bonus=head_sum(r * k2 * prow(PV_RK)) * v, gnw=prow(PV_GNW), gnb=prow(PV_GNB))

    ci = lax.broadcasted_iota(jnp.int32, (CHUNK, CHUNK), 0)
    cj = lax.broadcasted_iota(jnp.int32, (CHUNK, CHUNK), 1)
    tri_incl = jnp.where(ci >= cj, 1.0, 0.0).astype(BF16)
    head0 = lax.broadcasted_iota(jnp.int32, (CHUNK, LANES), 1) < HEAD_DIM

    def stack(x):
        return jnp.concatenate([jnp.where(head0, x, 0.0), jnp.where(head0, 0.0, x)], axis=0)

    def prep(c):
        def run():
            sl = slice(c * CHUNK, (c + 1) * CHUNK)
            ld = tile['logd'][sl]
            ld_hi, ld_lo = _split(ld)
            lp = (jnp.dot(tri_incl, ld_hi, preferred_element_type=F32)
                  + jnp.dot(tri_incl, ld_lo, preferred_element_type=F32))
            le = lp[CHUNK - 1:CHUNK, :]
            e_p = jnp.exp(lp)
            e_n = jnp.exp(-lp)
            e_e = jnp.exp(le - lp)
            b_c, k_c = tile['b_vec'][sl], tile['k2'][sl]
            opf_sc[wr, OP_RH, c] = stack(tile['r'][sl] * e_p)
            opf_sc[wr, OP_AH, c] = stack(tile['a_vec'][sl] * jnp.exp(lp - ld))
            opb_sc[wr, OP_BT, c] = stack(b_c * e_n).astype(BF16)
            opb_sc[wr, OP_KT, c] = stack(k_c * e_n).astype(BF16)
            opb_sc[wr, OP_VV, c] = stack(tile['v'][sl]).astype(BF16)
            opb_sc[wr, OP_BBT, c] = stack(b_c * e_e).T.astype(BF16)
            opb_sc[wr, OP_KBT, c] = stack(k_c * e_e).T.astype(BF16)
            el_sc[wr, c] = jnp.broadcast_to(jnp.exp(le), (SUBLANES, LANES))
        return run

    def substages():
        ops = [dict(c=c) for c in chunks]

        def gram():
            for o in ops:
                c = o['c']
                ar = jnp.concatenate([opf_sc[rd, OP_AH, c], opf_sc[rd, OP_RH, c]], axis=0).astype(BF16)
                gb = _mm_nt(ar, opb_sc[rd, OP_BT, c])
                gk = _mm_nt(ar, opb_sc[rd, OP_KT, c])
                o['a_ab'] = jnp.where(strict, gb[0:LANES], 0.0)
                o['a_rb'] = jnp.where(incl, gb[LANES:], 0.0).astype(BF16)
                o['a_ak'] = jnp.where(strict, gk[0:LANES], 0.0).astype(BF16)
                o['a_rk'] = jnp.where(incl, gk[LANES:], 0.0).astype(BF16)

        def values():
            for o in ops:
                vv = opb_sc[rd, OP_VV, o['c']]
                o['x'] = _mm(o['a_ak'], vv)
                o['yv'] = _mm(o['a_rk'], vv)
                o['cv'] = _mm(opb_sc[rd, OP_KBT, o['c']], vv)
                o['t'] = eye_f + jnp.where(off_masks[0], o['a_ab'], 0.0)

        def level_a(off):
            def run():
                for o in ops:
                    o['ta'] = _mm(o['t'], jnp.where(off, o['a_ab'], 0.0))
            return run

        def level_b():
            for o in ops:
                o['t'] = o['t'] + _mm(o['ta'], o['t'])

        def apply_inverse():
            for o in ops:
                ah_x = jnp.concatenate([opf_sc[rd, OP_AH, o['c']], o['x']], axis=1)
                o['w'] = _mm(o['t'], ah_x).astype(BF16)

        def emit():
            for o in ops:
                c = o['c']
                qy = _mm(o['a_rb'], o['w'])
                mc = _mm(opb_sc[rd, OP_BBT, c], o['w'])
                diag = jnp.where(eye, el_sc[rd, c][0:1, :], 0.0)
                qm_sc[c] = jnp.concatenate([opf_sc[rd, OP_RH, c] + qy[:, 0:LANES], diag + mc[:, 0:LANES]],
                                           axis=0).astype(BF16)
                yin_sc[c] = qy[:, LANES:] + o['yv']
                cc_sc[c] = mc[:, LANES:] + o['cv']

        steps = [gram, values]
        for off in off_masks[1:]:
            steps += [level_a(off), level_b]
        return steps + [apply_inverse, emit]

    chain = dict(s=jnp.where(chain_tile == 0, 0.0, s_sc[...]), ys=[])

    def link(c):
        def run():
            res = _mm(qm_sc[c], chain['s'])
            y2 = res[0:LANES] + yin_sc[c]
            chain['ys'].append(y2[0:CHUNK] + y2[CHUNK:2 * CHUNK])
            chain['s'] = res[LANES:] + cc_sc[c]
        return run

    def finish_chain():
        s_sc[...] = chain['s']
        y = jnp.concatenate(chain['ys'], axis=0)
        mean = head_sum(y) * (1.0 / HEAD_DIM)
        dlt = y - mean
        var = head_sum(dlt * dlt) * (1.0 / HEAD_DIM)
        yn = dlt * lax.rsqrt(var + GROUPNORM_EPS) * gn_sc[0:1, :] + gn_sc[1:2, :]
        yr_ref[...] = (yn + bg_sc[0]) * bg_sc[1]

    def cast(j):
        def run():
            cast_dst[j][...] = cast_src[j][...].astype(BF16)
        return run

    steps = substages()
    fill = [prep_tile_wide]
    for c in chunks:
        fill += [link(c), prep(c)] + [cast(j) for j in range(n_cast)[c::len(chunks)]]
    fill += [finish_chain]
    gaps = len(steps) - 1
    for i, run in enumerate(steps):
        run()
        for item in fill[i * len(fill) // gaps:(i + 1) * len(fill) // gaps] if i < gaps else ():
            item()
    bg_sc[...] = bgp_sc[...]
    gn_sc[...] = gnp_sc[...]
    bgp_sc[0] = tile['bonus']
    bgp_sc[1] = tile['g']
    gnp_sc[0:1, :] = tile['gnw']
    gnp_sc[1:2, :] = tile['gnb']


RWKV_TILE = 512
RWKV_STAGES = 3


def _rwkv_work_steps(t, c):
    return (c // LANES) * (t // RWKV_TILE)


def _cast_job(src, lo, hi, steps):
    rows = hi - lo
    count = max(n for n in range(1, steps + 1) if rows % n == 0 and (rows // n) % 16 == 0)
    blk = rows // count
    assert lo % blk == 0
    return (src, blk, lo // blk, count)


def _rwkv(g_main, g_lr, v_first, w2s, pvec, has_vres, casts=()):
    t = g_main.shape[0]
    c = pvec.shape[1]
    tt = RWKV_TILE
    ntile = t // tt
    n_work = _rwkv_work_steps(t, c)
    nsteps = n_work + RWKV_STAGES - 1
    col0 = c // LANES
    nch = tt // CHUNK

    def prep_of(s):
        ss = jnp.minimum(s, n_work - 1)
        return ss // ntile, ss % ntile

    def chain_of(s):
        cs = jnp.maximum(s - (RWKV_STAGES - 1), 0)
        return cs // ntile, cs % ntile

    def in_map(col_mult):
        return lambda s: (prep_of(s)[1], col_mult * col0 + prep_of(s)[0])

    in_specs = [pl.BlockSpec((tt, LANES), in_map(1)),
                pl.BlockSpec((tt, LANES), in_map(2)),
                pl.BlockSpec((tt, LANES), in_map(3)),
                pl.BlockSpec((tt, 2 * LR_PAD), lambda s: (prep_of(s)[1], 0))]
    args = [g_main, g_main, g_main, g_lr]
    if has_vres:
        in_specs.append(pl.BlockSpec((tt, LANES), in_map(0)))
        args.append(v_first)
    in_specs += [pl.BlockSpec((768, LANES), lambda s: (0, prep_of(s)[0])),
                 pl.BlockSpec((PV_ROWS, LANES), lambda s: (0, prep_of(s)[0]))]
    args += [w2s, pvec]
    yr_spec = pl.BlockSpec((tt, LANES), lambda s: (chain_of(s)[1], chain_of(s)[0]))
    out_shape = jax.ShapeDtypeStruct((t, c), F32)
    if has_vres:
        out_specs, out_shapes = [yr_spec], [out_shape]
    else:
        out_specs, out_shapes = [yr_spec, pl.BlockSpec((tt, LANES), in_map(0))], [out_shape, out_shape]
    for src, rows, first, count in casts:
        blk = (rows, src.shape[1])
        in_specs.append(pl.BlockSpec(blk, lambda s, first=first, count=count: (first + jnp.minimum(s, count - 1), 0)))
        args.append(src)
        out_specs.append(pl.BlockSpec(blk, lambda s, count=count: (jnp.minimum(s, count - 1), 0)))
        out_shapes.append(jax.ShapeDtypeStruct((rows * count, src.shape[1]), BF16))
    return pl.pallas_call(
        functools.partial(_rwkv_kernel, tt=tt, ntile=ntile, nsteps=nsteps, has_vres=has_vres, n_cast=len(casts)),
        grid=(nsteps,),
        in_specs=in_specs,
        out_specs=out_specs,
        out_shape=out_shapes,
        scratch_shapes=[pltpu.VMEM((LANES, LANES), F32),
                        pltpu.VMEM((SUBLANES, LANES), F32),
                        pltpu.VMEM((SUBLANES, LR_PAD), F32),
                        pltpu.VMEM((nch, 2 * LANES, LANES), BF16),
                        pltpu.VMEM((nch, LANES, LANES), F32),
                        pltpu.VMEM((nch, LANES, LANES), F32),
                        pltpu.VMEM((2, tt, LANES), F32),
                        pltpu.VMEM((SUBLANES, LANES), F32),
                        pltpu.VMEM((2, 2, nch, LANES, LANES), F32),
                        pltpu.VMEM((2, 5, nch, LANES, LANES), BF16),
                        pltpu.VMEM((2, nch, SUBLANES, LANES), F32),
                        pltpu.VMEM((2, tt, LANES), F32),
                        pltpu.VMEM((SUBLANES, LANES), F32)],
        compiler_params=_cparams(("arbitrary",)),
        name="rwkv7_vres" if has_vres else "rwkv7",
    )(*args)


def _mix_kernel(*refs, tm, has_router):
    if has_router:
        (x_ref, pp_ref, halo_ref, glp_ref, glr_ref, yr_ref, pw_ref, ps_ref, po_ref, ro_ref, mo_ref,
         g1_ref, sc2_ref, sh2_ref, rt_ref, xo_ref, h2_ref, idx_ref, gt_ref, ext) = refs
    else:
        (x_ref, pp_ref, halo_ref, glp_ref, glr_ref, yr_ref, pw_ref, ps_ref, po_ref, ro_ref, mo_ref,
         g1_ref, sc2_ref, sh2_ref, xo_ref, h2_ref, ext) = refs
    i = pl.program_id(0)
    ext[0:POOL_HALO, :] = jnp.where(i == 0, 0.0, halo_ref[...].astype(F32))
    ext[POOL_HALO:POOL_HALO + tm, :] = pp_ref[...].astype(F32)
    pos = i * tm + lax.broadcasted_iota(jnp.int32, (tm, 1), 0)
    gw = ext.shape[1] // len(POOL_WINDOWS)
    parts = []
    for gi, win in enumerate(POOL_WINDOWS):
        cs = slice(gi * gw, (gi + 1) * gw)
        cur = ext[POOL_HALO:POOL_HALO + tm, cs]
        s = cur
        for j in range(1, win):
            s = s + ext[POOL_HALO - j:POOL_HALO - j + tm, cs]
        cnt = jnp.minimum(pos + 1, win).astype(F32)
        pooled = s / cnt - cur
        parts.append(_mm(pooled, pw_ref[gi]))
    mixed = jnp.concatenate(parts, axis=-1) * ps_ref[...]
    y_pool = _mm(mixed, po_ref[...])
    y_rwkv = _mm(yr_ref[...], ro_ref[...])
    merged = (jax.nn.sigmoid(glp_ref[...].astype(F32)) * y_pool
              + jax.nn.sigmoid(glr_ref[...].astype(F32)) * y_rwkv)
    xn = x_ref[...] + g1_ref[...] * _mm(merged, mo_ref[...])
    xo_ref[...] = xn
    ms = jnp.mean(xn * xn, axis=-1, keepdims=True)
    h2 = xn * lax.rsqrt(ms + NORM_EPS) * (1.0 + sc2_ref[...]) + sh2_ref[...]
    if has_router:
        slab = h2.shape[1] // LANES
        for a in range(slab):
            h2_ref[pl.ds(a, tm, stride=slab), :] = h2[:, a * LANES:(a + 1) * LANES]
        rt = rt_ref[...]
        hi, lo = _split(h2)
        rhi, rlo = _split(rt)
        logits = (jnp.dot(hi, rhi, preferred_element_type=F32) + jnp.dot(lo, rhi, preferred_element_type=F32)
                  + jnp.dot(hi, rlo, preferred_element_type=F32))
        n_exp = 8
        lane = lax.broadcasted_iota(jnp.int32, logits.shape, 1)
        lane_f = lane.astype(F32)
        big = float(LANES)
        lg = jnp.where(lane < n_exp, logits, -jnp.inf)
        m1 = jnp.max(lg, axis=-1, keepdims=True)
        i1 = jnp.min(jnp.where(lg == m1, lane_f, big), axis=-1, keepdims=True)
        lg2 = jnp.where(lane_f == i1, -jnp.inf, lg)
        m2 = jnp.max(lg2, axis=-1, keepdims=True)
        i2 = jnp.min(jnp.where(lg2 == m2, lane_f, big), axis=-1, keepdims=True)
        e2 = jnp.exp(m2 - m1)
        p1 = 1.0 / (1.0 + e2)
        p2 = e2 / (1.0 + e2)
        idx_ref[...] = jnp.where(lane == 0, i1, jnp.where(lane == 1, i2, 0.0)).astype(jnp.int32)
        gt_ref[...] = jnp.where(lane == 0, p1, jnp.where(lane == 1, p2, 0.0))
    else:
        h2_ref[...] = h2.astype(h2_ref.dtype)


def _mix(x, g_main, yr, pool_w, pool_scale, pool_out, rk_out, mix_out, gate1, scale2, shift2, router_pad):
    t, d = x.shape
    c = yr.shape[1]
    has_router = router_pad is not None
    tm = 256 if has_router else 512
    once = dict(pipeline_mode=pl.Buffered(1))
    halo_blocks = tm // POOL_HALO
    cb_glp = (4 * c) // d
    in_specs = [pl.BlockSpec((tm, d), lambda i: (i, 0)),
                pl.BlockSpec((tm, c), lambda i: (i, 0)),
                pl.BlockSpec((POOL_HALO, c), lambda i: (jnp.maximum(i * halo_blocks - 1, 0), 0)),
                pl.BlockSpec((tm, d), lambda i: (i, cb_glp)),
                pl.BlockSpec((tm, d), lambda i: (i, cb_glp + 1)),
                pl.BlockSpec((tm, c), lambda i: (i, 0)),
                pl.BlockSpec(pool_w.shape, lambda i: (0, 0, 0), **once),
                pl.BlockSpec((1, c), lambda i: (0, 0)),
                pl.BlockSpec((c, d), lambda i: (0, 0), **once),
                pl.BlockSpec((c, d), lambda i: (0, 0), **once),
                pl.BlockSpec((d, d), lambda i: (0, 0), **once),
                pl.BlockSpec((1, d), lambda i: (0, 0)),
                pl.BlockSpec((1, d), lambda i: (0, 0)),
                pl.BlockSpec((1, d), lambda i: (0, 0))]
    args = [x, g_main, g_main, g_main, g_main, yr, pool_w, pool_scale, pool_out, rk_out, mix_out,
            gate1, scale2, shift2]
    out_specs = [pl.BlockSpec((tm, d), lambda i: (i, 0))]
    out_shape = [jax.ShapeDtypeStruct((t, d), F32)]
    if has_router:
        slab = d // LANES
        out_specs.append(pl.BlockSpec((tm * slab, LANES), lambda i: (i, 0)))
        out_shape.append(jax.ShapeDtypeStruct((t * slab, LANES), F32))
    else:
        out_specs.append(pl.BlockSpec((tm, d), lambda i: (i, 0)))
        out_shape.append(jax.ShapeDtypeStruct((t, d), BF16))
    if has_router:
        in_specs.append(pl.BlockSpec((d, LANES), lambda i: (0, 0)))
        args.append(router_pad)
        out_specs += [pl.BlockSpec((tm, LANES), lambda i: (i, 0)), pl.BlockSpec((tm, LANES), lambda i: (i, 0))]
        out_shape += [jax.ShapeDtypeStruct((t, LANES), jnp.int32), jax.ShapeDtypeStruct((t, LANES), F32)]
    return pl.pallas_call(
        functools.partial(_mix_kernel, tm=tm, has_router=has_router),
        grid=(t // tm,),
        in_specs=in_specs,
        out_specs=out_specs,
        out_shape=out_shape,
        scratch_shapes=[pltpu.VMEM((POOL_HALO + tm, c), F32)],
        compiler_params=_cparams(("arbitrary",), VMEM_LIMIT_HIGH),
        name="mix_router" if has_router else "mix",
    )(*args)


def _ffn_kernel(h_ref, x_ref, g_ref, w1_ref, w3_ref, w2_ref, o_ref, acc):
    f = pl.program_id(1)

    @pl.when(f == 0)
    def _():
        acc[...] = jnp.zeros_like(acc)

    h = h_ref[...]
    a = jnp.dot(h, w1_ref[...].astype(BF16), preferred_element_type=F32)
    b = jnp.dot(h, w3_ref[...].astype(BF16), preferred_element_type=F32)
    u = (a * jax.nn.sigmoid(a) * b).astype(BF16)
    acc[...] += jnp.dot(u, w2_ref[...].astype(BF16), preferred_element_type=F32)

    @pl.when(f == pl.num_programs(1) - 1)
    def _():
        o_ref[...] = x_ref[...] + g_ref[...] * acc[...]


def _ffn(h2, x, gate2, w1, w3, w2):
    t, d = x.shape
    ff = w1.shape[1]
    tm, tf = 512, 512
    return pl.pallas_call(
        _ffn_kernel,
        grid=(t // tm, ff // tf),
        in_specs=[pl.BlockSpec((tm, d), lambda i, f: (i, 0)),
                  pl.BlockSpec((tm, d), lambda i, f: (i, 0)),
                  pl.BlockSpec((1, d), lambda i, f: (0, 0)),
                  pl.BlockSpec((d, tf), lambda i, f: (0, f)),
                  pl.BlockSpec((d, tf), lambda i, f: (0, f)),
                  pl.BlockSpec((tf, d), lambda i, f: (f, 0))],
        out_specs=pl.BlockSpec((tm, d), lambda i, f: (i, 0)),
        out_shape=jax.ShapeDtypeStruct((t, d), F32),
        scratch_shapes=[pltpu.VMEM((tm, d), F32)],
        compiler_params=_cparams(("arbitrary", "arbitrary")),
        name="ffn_dense",
    )(h2, x, gate2, w1, w3, w2)


def _slab_copy(src_hbm, src_row, dst, slot, dst_row, slab, sem):
    src = src_hbm.at[pl.ds(pl.multiple_of(src_row * slab, slab), slab), :]
    return pltpu.make_async_copy(src, dst.at[slot, pl.ds(pl.multiple_of(dst_row * slab, slab), slab), :],
                                 sem.at[slot])


def _moe_kernel(te_ref, tr_ref, na_ref, tok_ref, h_hbm, w1_ref, w3_ref, w2l_ref, w2h_ref,
                o_ref, xg, xb, acc, sem, *, tm, n_lo):
    b = pl.program_id(0)
    f = pl.program_id(1)
    nf = pl.num_programs(1)
    n_act = na_ref[0]
    slab = xb.shape[1] // LANES

    def start_gather(tile):
        def body(s, carry):
            _slab_copy(h_hbm, tok_ref[tile * tm + s], xg, 0, s, slab, sem).start()
            return carry
        lax.fori_loop(0, tm, body, 0, unroll=8)

    @pl.when(jnp.logical_and(f == 0, b < n_act))
    def _():
        @pl.when(b == 0)
        def _():
            start_gather(0)

        pltpu.make_async_copy(h_hbm.at[pl.ds(0, tm * slab), :], xg.at[0], sem.at[0]).wait()
        for a in range(slab):
            xb[:, a * LANES:(a + 1) * LANES] = xg[0, pl.ds(a, tm, stride=slab), :].astype(BF16)
        acc[...] = jnp.zeros_like(acc)

        @pl.when(b + 1 < n_act)
        def _():
            start_gather(b + 1)

    def swiglu_step(w2_ref, rows):
        x = xb[0:rows, :]
        a_ = jnp.dot(x, w1_ref[0].astype(BF16), preferred_element_type=F32)
        b_ = jnp.dot(x, w3_ref[0].astype(BF16), preferred_element_type=F32)
        u = (a_ * jax.nn.sigmoid(a_) * b_).astype(BF16)
        acc[0:rows, :] += jnp.dot(u, w2_ref[0].astype(BF16), preferred_element_type=F32)

    in_lo = te_ref[b] < n_lo
    for rows in range(MOE_ROW_STEP, tm + 1, MOE_ROW_STEP):
        has_rows = jnp.logical_and(b < n_act, tr_ref[b] == rows)

        @pl.when(jnp.logical_and(has_rows, in_lo))
        def _():
            swiglu_step(w2l_ref, rows)

        @pl.when(jnp.logical_and(has_rows, jnp.logical_not(in_lo)))
        def _():
            swiglu_step(w2h_ref, rows)

    @pl.when(f == nf - 1)
    def _():
        res = jnp.where(b < n_act, acc[...], 0.0)
        for a in range(slab):
            o_ref[pl.ds(a, tm, stride=slab), :] = res[:, a * LANES:(a + 1) * LANES]


MOE_ROW_STEP = 128


def _moe(h2_slab, tile_e, tile_rows, n_active, slot_tok, w1, w3, w2_lo, w2_hi, tm, n_tiles):
    n_exp, d, ff = w1.shape
    n_lo = w2_lo.shape[0]
    tf = 1024
    nf = ff // tf

    def w13_map(b, f, te, tr, na, tok):
        return (te[b], 0, jnp.where(b < na[0], f, nf - 1))

    def w2_map(lo):
        def index(b, f, te, tr, na, tok):
            e = te[b]
            mine = jnp.logical_and(b < na[0], (e < n_lo) if lo else (e >= n_lo))
            e_loc = jnp.minimum(e, n_lo - 1) if lo else jnp.maximum(e - n_lo, 0)
            return (e_loc, jnp.where(mine, f, (nf - 1) if lo else 0), 0)
        return index

    grid_spec = pltpu.PrefetchScalarGridSpec(
        num_scalar_prefetch=4,
        grid=(n_tiles, nf),
        in_specs=[pl.BlockSpec(memory_space=pl.ANY),
                  pl.BlockSpec((1, d, tf), w13_map), pl.BlockSpec((1, d, tf), w13_map),
                  pl.BlockSpec((1, tf, d), w2_map(True)), pl.BlockSpec((1, tf, d), w2_map(False))],
        out_specs=pl.BlockSpec((tm * (d // LANES), LANES), lambda b, f, te, tr, na, tok: (b, 0)),
        scratch_shapes=[pltpu.VMEM((1, tm * (d // LANES), LANES), F32),
                        pltpu.VMEM((tm, d), BF16),
                        pltpu.VMEM((tm, d), F32),
                        pltpu.SemaphoreType.DMA((1,))],
    )
    return pl.pallas_call(
        functools.partial(_moe_kernel, tm=tm, n_lo=n_lo),
        grid_spec=grid_spec,
        out_shape=jax.ShapeDtypeStruct((n_tiles * tm * (d // LANES), LANES), F32),
        compiler_params=_cparams(("arbitrary", "arbitrary")),
        name="moe_experts",
    )(tile_e, tile_rows, n_active, slot_tok, h2_slab, w1, w3, w2_lo, w2_hi)


def _comb_kernel(dest_ref, yb_hbm, x_ref, gt_ref, g2_ref, fg_ref, o_ref, buf, sem, *, tm):
    i = pl.program_id(0)
    n = pl.num_programs(0)
    slot = i % 2
    nrow = TOP_K * tm
    slab = o_ref.shape[1] // LANES

    def start_gather(tile, slt):
        def body(s, carry):
            _slab_copy(yb_hbm, dest_ref[tile * nrow + s], buf, slt, s, slab, sem).start()
            return carry
        lax.fori_loop(0, nrow, body, 0, unroll=8)

    @pl.when(i == 0)
    def _():
        start_gather(0, 0)

    @pl.when(i + 1 < n)
    def _():
        start_gather(i + 1, 1 - slot)

    pltpu.make_async_copy(yb_hbm.at[pl.ds(0, nrow * slab), :], buf.at[slot], sem.at[slot]).wait()
    gt = gt_ref[...]
    p0 = gt[:, 0:1]
    p1 = gt[:, 1:2]
    cols = []
    for a in range(slab):
        y0 = buf[slot, pl.ds(a, tm, stride=TOP_K * slab), :]
        y1 = buf[slot, pl.ds(slab + a, tm, stride=TOP_K * slab), :]
        cols.append(y0 * p0 + y1 * p1)
    fmoe = jnp.concatenate(cols, axis=-1)
    xn = x_ref[...] + g2_ref[...] * fmoe
    ms = jnp.mean(xn * xn, axis=-1, keepdims=True)
    o_ref[...] = xn * lax.rsqrt(ms + NORM_EPS) * fg_ref[...]


def _combine(dest, yb_slab, x, gates, gate2, final_gain):
    t, d = x.shape
    tm = 256
    grid_spec = pltpu.PrefetchScalarGridSpec(
        num_scalar_prefetch=1,
        grid=(t // tm,),
        in_specs=[pl.BlockSpec(memory_space=pl.ANY),
                  pl.BlockSpec((tm, d), lambda i, dr: (i, 0)),
                  pl.BlockSpec((tm, LANES), lambda i, dr: (i, 0)),
                  pl.BlockSpec((1, d), lambda i, dr: (0, 0)),
                  pl.BlockSpec((1, d), lambda i, dr: (0, 0))],
        out_specs=pl.BlockSpec((tm, d), lambda i, dr: (i, 0)),
        scratch_shapes=[pltpu.VMEM((2, TOP_K * tm * (d // LANES), LANES), F32),
                        pltpu.SemaphoreType.DMA((2,))],
    )
    return pl.pallas_call(
        functools.partial(_comb_kernel, tm=tm),
        grid_spec=grid_spec,
        out_shape=jax.ShapeDtypeStruct((t, d), F32),
        compiler_params=_cparams(("arbitrary",)),
        name="moe_combine",
    )(dest, yb_slab, x, gates, gate2, final_gain)


def _pad_rows(w, rows):
    return jnp.pad(w, ((0, rows - w.shape[0]), (0, 0)))


def _lowrank_weights(l, rk_lerp, rk_w1, rk_a1, rk_g1, rk_vmu, rk_v1):
    mus = [rk_lerp[l, 0], rk_lerp[l, 1], rk_lerp[l, 2]]
    ws = [rk_w1[l], rk_a1[l], rk_g1[l]]
    if l > 0:
        mus.append(rk_vmu[l - 1])
        ws.append(rk_v1[l - 1])
    wa = jnp.concatenate([(1.0 - m)[:, None] * w for m, w in zip(mus, ws)], axis=1)
    wb = jnp.concatenate([m[:, None] * w for m, w in zip(mus, ws)], axis=1)
    wa = jnp.pad(wa, ((0, 0), (0, LR_PAD - wa.shape[1])))
    wb = jnp.pad(wb, ((0, 0), (0, LR_PAD - wb.shape[1])))
    return jnp.concatenate([wa, wb], axis=1).astype(BF16)


def _second_stage_weights(l, rk_w2, rk_a2, rk_g2, rk_v2):
    c = rk_w2.shape[2]
    w2 = _pad_rows(rk_w2[l], 128)
    a2 = jnp.concatenate([jnp.zeros((64, c), F32), rk_a2[l]], axis=0)
    g2 = _pad_rows(rk_g2[l], 256)
    if l > 0:
        v2 = jnp.concatenate([jnp.zeros((160, c), F32), rk_v2[l - 1], jnp.zeros((64, c), F32)], axis=0)
    else:
        v2 = jnp.zeros((256, c), F32)
    return jnp.concatenate([w2, a2, g2, v2], axis=0).astype(BF16)


def _route(idx, tm, n_exp, n_tiles):
    n_assign = idx.size
    flat_e = idx.reshape(-1)
    onehot = (flat_e[:, None] == jnp.arange(n_exp, dtype=jnp.int32)[None, :]).astype(jnp.int32)
    csum = jnp.cumsum(onehot, axis=0)
    rank = jnp.take_along_axis(csum, flat_e[:, None], axis=1)[:, 0] - 1
    counts = csum[-1]
    padded = (counts + tm - 1) // tm * tm
    pad_end = jnp.cumsum(padded)
    pad_start = pad_end - padded
    dest = (pad_start[flat_e] + rank).astype(jnp.int32)
    slot_tok = jnp.zeros((n_tiles * tm,), jnp.int32).at[dest].set(
        jnp.arange(n_assign, dtype=jnp.int32) // TOP_K)
    n_active = (pad_end[-1] // tm).astype(jnp.int32)
    tile_ids = jnp.minimum(jnp.arange(n_tiles, dtype=jnp.int32), jnp.maximum(n_active - 1, 0))
    tile_e = jnp.minimum(jnp.searchsorted(pad_end, tile_ids * tm, side='right'), n_exp - 1).astype(jnp.int32)
    filled = counts[tile_e] - (tile_ids * tm - pad_start[tile_e])
    tile_rows = jnp.clip((filled + MOE_ROW_STEP - 1) // MOE_ROW_STEP * MOE_ROW_STEP, MOE_ROW_STEP, tm)
    return dest, slot_tok, tile_e, tile_rows.astype(jnp.int32), n_active.reshape(1)


def kernel(x, c, ada_w, ada_b, w_in, pool_w, pool_scale, pool_out, rk_shift, rk_lerp, rk_w0, rk_w1, rk_w2, rk_a0, rk_a1, rk_a2, rk_g1, rk_g2, rk_kk, rk_ka, rk_rk, rk_gn_w, rk_gn_b, rk_vmu, rk_v0, rk_v1, rk_v2, rk_out, mix_out, ffn_w1, ffn_w3, ffn_w2, router, moe_w1, moe_w3, moe_w2, final_gain):
    bsz, t, d = x.shape
    depth = ada_w.shape[0]
    cw = rk_w0.shape[1]
    n_exp = router.shape[2]
    assert bsz == 1 and depth == 2 and cw % LANES == 0
    xs = x.reshape(t, d)
    mod = _ada(c.reshape(d, 1), ada_w, ada_b)

    def mod_row(l, i):
        return mod[l, :, i * d:(i + 1) * d]

    v_first = None
    out = None
    for l in range(depth):
        w_lr = _lowrank_weights(l, rk_lerp, rk_w1, rk_a1, rk_g1, rk_vmu, rk_v1)
        if l == 0:
            g_main, g_lr = _in_proj(xs, mod_row(l, 1), mod_row(l, 0), w_in, l, w_lr)
        else:
            g_main, g_lr = _in_proj(xs, mod_row(l, 1), mod_row(l, 0), w_in_next.reshape(1, d, -1), 0, w_lr)
        w2s = _second_stage_weights(l, rk_w2, rk_a2, rk_g2, rk_v2)
        zeros_c = jnp.zeros((cw,), F32)
        rows = [rk_w0[l], rk_a0[l], rk_v0[l - 1] if l > 0 else zeros_c, rk_kk[l], rk_ka[l],
                rk_gn_w[l], rk_gn_b[l], rk_shift[l, 0], rk_shift[l, 1], rk_shift[l, 2],
                rk_rk[l].reshape(cw)]
        pvec = _pad_rows(jnp.stack(rows, axis=0), PV_ROWS)
        steps = _rwkv_work_steps(t, cw)
        mixer = [w.reshape(-1, w.shape[-1]) for w in (pool_w, pool_out, rk_out, mix_out)]
        mixer_casts = [_cast_job(w, l * (w.shape[0] // depth), (l + 1) * (w.shape[0] // depth), steps)
                       for w in mixer]
        if l == 0:
            dense = [ffn_w1[0], ffn_w3[0], ffn_w2[0]]
            experts = [moe_w1[0].reshape(-1, moe_w1.shape[-1]), moe_w3[0].reshape(-1, moe_w3.shape[-1]),
                       moe_w2[0].reshape(-1, moe_w2.shape[-1])]
            e1, e3, e2 = experts
            casts = ([_cast_job(w, 0, w.shape[0], steps) for w in dense]
                     + [_cast_job(e1, 0, e1.shape[0], steps), _cast_job(e2, 0, e2.shape[0] // 2, steps)]
                     + [_cast_job(w_in.reshape(-1, w_in.shape[-1]), d, 2 * d, steps)] + mixer_casts)
            yr, v_first, *cast_out = _rwkv(g_main, g_lr, None, w2s, pvec, has_vres=False, casts=casts)
            dense_bf, (e1_bf, e2_lo), w_in_next, mixer_bf = cast_out[:3], cast_out[3:5], cast_out[5], cast_out[6:]
        else:
            casts = [_cast_job(e3, 0, e3.shape[0], steps),
                     _cast_job(e2, e2.shape[0] // 2, e2.shape[0], steps)] + mixer_casts
            yr, e3_bf, e2_hi, *mixer_bf = _rwkv(g_main, g_lr, v_first, w2s, pvec, has_vres=True, casts=casts)
        is_moe = (l % 2 == 1)
        router_pad = None
        if is_moe:
            router_pad = jnp.pad(router[l // 2], ((0, 0), (0, LANES - n_exp)))
        res = _mix(xs, g_main, yr, mixer_bf[0].reshape(pool_w.shape[1:]), pool_scale[l].reshape(1, -1),
                   mixer_bf[1], mixer_bf[2], mixer_bf[3], mod_row(l, 2), mod_row(l, 4), mod_row(l, 3), router_pad)
        if not is_moe:
            x_new, h2 = res
            xs = _ffn(h2, x_new, mod_row(l, 5), *dense_bf)
        else:
            x_new, h2, idx_pad, gate_pad = res
            tm_moe = 512
            n_tiles = (t * TOP_K) // tm_moe + n_exp
            dest, slot_tok, tile_e, tile_rows, n_active = _route(idx_pad[:, :TOP_K], tm_moe, n_exp, n_tiles)
            n_lo = n_exp // 2
            yb_slab = _moe(h2, tile_e, tile_rows, n_active, slot_tok,
                           e1_bf.reshape(moe_w1.shape[1:]), e3_bf.reshape(moe_w3.shape[1:]),
                           e2_lo.reshape((n_lo,) + moe_w2.shape[2:]),
                           e2_hi.reshape((n_exp - n_lo,) + moe_w2.shape[2:]), tm_moe, n_tiles)
            out = _combine(dest, yb_slab, x_new, gate_pad, mod_row(l, 5), final_gain.reshape(1, d))
    return out.reshape(bsz, t, d)
```

```python
import functools

import jax
import jax.numpy as jnp
from jax import lax
from jax.experimental import pallas as pl
from jax.experimental.pallas import tpu as pltpu

F32 = jnp.float32
BF16 = jnp.bfloat16

NORM_EPS = 1e-6
GROUPNORM_EPS = 64e-5
L2_EPS = 1e-12
POOL_WINDOWS = (2, 4, 8, 16)
HEAD_DIM = 64
TOP_K = 2
LANES = 128
SUBLANES = 8
CHUNK = 64
LR_PAD = 384
POOL_HALO = 16
VMEM_LIMIT = 56 * 1024 * 1024
VMEM_LIMIT_HIGH = 60 * 1024 * 1024


def _cparams(sem, vmem_limit=VMEM_LIMIT):
    return pltpu.CompilerParams(dimension_semantics=sem, vmem_limit_bytes=vmem_limit)


def _mm(a, b):
    return jnp.dot(a.astype(BF16), b.astype(BF16), preferred_element_type=F32)


def _mm_nt(a, b):
    return lax.dot_general(a.astype(BF16), b.astype(BF16), (((1,), (1,)), ((), ())),
                           preferred_element_type=F32)


def _split(x):
    hi = x.astype(BF16)
    lo = (x - hi.astype(F32)).astype(BF16)
    return hi, lo


def _ada_kernel(c_ref, w_ref, b_ref, o_ref):
    c = c_ref[...]
    act = c * jax.nn.sigmoid(c)
    o_ref[0] = jnp.sum(act * w_ref[0], axis=0, keepdims=True) + b_ref[0]


def _ada(c_col, ada_w, ada_b):
    depth, d, n = ada_w.shape
    tn = 1024
    return pl.pallas_call(
        _ada_kernel,
        grid=(depth, n // tn),
        in_specs=[pl.BlockSpec((d, 1), lambda l, j: (0, 0)),
                  pl.BlockSpec((1, d, tn), lambda l, j: (l, 0, j)),
                  pl.BlockSpec((1, 1, tn), lambda l, j: (l, 0, j))],
        out_specs=pl.BlockSpec((1, 1, tn), lambda l, j: (l, 0, j)),
        out_shape=jax.ShapeDtypeStruct((depth, 1, n), F32),
        compiler_params=_cparams(("arbitrary", "arbitrary")),
        name="ada_gemv",
    )(c_col, ada_w, ada_b.reshape(depth, 1, n))


def _in_kernel(x_ref, sc_ref, sh_ref, w_ref, wlr_ref, o_ref, olr_ref, h_sc):
    @pl.when(pl.program_id(1) == 0)
    def _():
        x = x_ref[...]
        ms = jnp.mean(x * x, axis=-1, keepdims=True)
        h = x * lax.rsqrt(ms + NORM_EPS) * (1.0 + sc_ref[...]) + sh_ref[...]
        hb = h.astype(BF16)
        h_sc[...] = hb
        olr_ref[...] = jnp.dot(hb, wlr_ref[...], preferred_element_type=F32)

    o_ref[...] = jnp.dot(h_sc[...], w_ref[0].astype(BF16), preferred_element_type=F32).astype(o_ref.dtype)


def _in_proj(x, scale, shift, w_in, layer, w_lr):
    t, d = x.shape
    n = w_in.shape[2]
    nlr = w_lr.shape[1]
    tm, tn = 1024, 1024
    return pl.pallas_call(
        _in_kernel,
        grid=(t // tm, n // tn),
        in_specs=[pl.BlockSpec((tm, d), lambda i, j: (i, 0)),
                  pl.BlockSpec((1, d), lambda i, j: (0, 0)),
                  pl.BlockSpec((1, d), lambda i, j: (0, 0)),
                  pl.BlockSpec((1, d, tn), lambda i, j: (layer, 0, j)),
                  pl.BlockSpec((d, nlr), lambda i, j: (0, 0))],
        out_specs=[pl.BlockSpec((tm, tn), lambda i, j: (i, j)),
                   pl.BlockSpec((tm, nlr), lambda i, j: (i, 0))],
        out_shape=[jax.ShapeDtypeStruct((t, n), BF16), jax.ShapeDtypeStruct((t, nlr), F32)],
        scratch_shapes=[pltpu.VMEM((tm, d), BF16)],
        compiler_params=_cparams(("arbitrary", "arbitrary"), VMEM_LIMIT_HIGH),
        name="in_proj",
    )(x, scale, shift, w_in, w_lr)


PV_W0, PV_A0, PV_V0, PV_KK, PV_KA, PV_GNW, PV_GNB, PV_MUR, PV_MUK, PV_MUV, PV_RK = range(11)
PV_ROWS = 16


def _rwkv_kernel(*refs, tt, ntile, nsteps, has_vres, n_cast):
    n_in = 7 if has_vres else 6
    n_out = 1 if has_vres else 2
    ins = refs[:n_in]
    cast_src = refs[n_in:n_in + n_cast]
    outs = refs[n_in + n_cast:n_in + n_cast + n_out]
    cast_dst = refs[n_in + n_cast + n_out:n_in + 2 * n_cast + n_out]
    (s_sc, c_rkv, c_lr, qm_sc, yin_sc, cc_sc, bg_sc, gn_sc,
     opf_sc, opb_sc, el_sc, bgp_sc, gnp_sc) = refs[n_in + 2 * n_cast + n_out:]
    if has_vres:
        (r_ref, k_ref, v_ref, lr_ref, vf_ref, w2_ref, pv_ref), (yr_ref,) = ins, outs
    else:
        (r_ref, k_ref, v_ref, lr_ref, w2_ref, pv_ref), (yr_ref, vfo_ref) = ins, outs
    step = pl.program_id(0)
    n_work = nsteps - 2
    is_drain = step >= n_work
    prep_tile = jnp.minimum(step, n_work - 1) % ntile
    chain_tile = jnp.maximum(step - 2, 0) % ntile
    wr = step % 2
    rd = 1 - wr
    chunks = range(tt // CHUNK)

    @pl.when(step == 0)
    def _():
        for ref in (s_sc, qm_sc, yin_sc, cc_sc, bg_sc, gn_sc, opf_sc, opb_sc, el_sc, bgp_sc, gnp_sc):
            ref[...] = jnp.zeros_like(ref)

    @pl.when(prep_tile == 0)
    def _():
        c_rkv[...] = jnp.zeros_like(c_rkv)
        c_lr[...] = jnp.zeros_like(c_lr)

    li = lax.broadcasted_iota(jnp.int32, (LANES, LANES), 0)
    lj = lax.broadcasted_iota(jnp.int32, (LANES, LANES), 1)
    same_head = ((li >> 6) == (lj >> 6))
    head_ones = jnp.where(same_head, 1.0, 0.0).astype(BF16)
    strict = (li & (CHUNK - 1)) > (lj & (CHUNK - 1))
    incl = (li & (CHUNK - 1)) >= (lj & (CHUNK - 1))
    eye = li == lj
    eye_f = jnp.where(eye, 1.0, 0.0)
    off_masks = [((li >> (s + 1)) == (lj >> (s + 1))) & (((li >> s) & 1) == 1) & (((lj >> s) & 1) == 0)
                 for s in range(CHUNK.bit_length() - 1)]

    def head_sum(x):
        return _mm(x, head_ones)

    OP_RH, OP_AH = 0, 1
    OP_BT, OP_KT, OP_VV, OP_BBT, OP_KBT = range(5)
    tile = {}

    def prep_tile_wide():
        row = lax.broadcasted_iota(jnp.int32, (tt, 1), 0)

        def shift(x, carry_row):
            return jnp.where(row == 0, carry_row, pltpu.roll(x, 1, axis=0))

        pv = pv_ref[...]

        def prow(i):
            return pv[i:i + 1, :]

        r_raw, k_raw, v_raw = (ref[...].astype(F32) for ref in (r_ref, k_ref, v_ref))
        zb = lr_ref[:, LR_PAD:2 * LR_PAD]
        c_rkv_used = jnp.where(is_drain, c_rkv[4:7, :], c_rkv[0:3, :])
        c_z = jnp.where(is_drain, c_lr[1:2, :], c_lr[0:1, :])
        r_prev = shift(r_raw, c_rkv_used[0:1, :])
        k_prev = shift(k_raw, c_rkv_used[1:2, :])
        v_prev = shift(v_raw, c_rkv_used[2:3, :])
        z = lr_ref[:, 0:LR_PAD] + shift(zb, c_z)
        c_rkv[4:7, :] = c_rkv_used
        c_lr[1:2, :] = c_z
        c_rkv[0:1, :] = r_raw[tt - 1:tt, :]
        c_rkv[1:2, :] = k_raw[tt - 1:tt, :]
        c_rkv[2:3, :] = v_raw[tt - 1:tt, :]
        c_lr[0:1, :] = zb[tt - 1:tt, :]

        r = r_raw + (r_prev - r_raw) * prow(PV_MUR)
        k = k_raw + (k_prev - k_raw) * prow(PV_MUK)
        v = v_raw + (v_prev - v_raw) * prow(PV_MUV)

        z_wa = z[:, 0:LANES]
        z_gv = z[:, LANES:LR_PAD]
        w_pre = prow(PV_W0) + _mm(jnp.tanh(z_wa), w2_ref[0:128, :])
        neg = -w_pre
        softplus = jnp.maximum(neg, 0.0) + jnp.log(1.0 + jnp.exp(-jnp.abs(neg)))
        logd = -jnp.exp(-softplus - 0.5)
        a_i = jax.nn.sigmoid(prow(PV_A0) + _mm(z_wa, w2_ref[128:256, :]))
        g = _mm(jax.nn.sigmoid(z_gv), w2_ref[256:512, :])
        if has_vres:
            v = v + (vf_ref[...] - v) * jax.nn.sigmoid(prow(PV_V0) + _mm(z_gv, w2_ref[512:768, :]))
        else:
            vfo_ref[...] = v

        kk = k * prow(PV_KK)
        kk = kk / jnp.maximum(jnp.sqrt(head_sum(kk * kk)), L2_EPS)
        k2 = k * (1.0 + (a_i - 1.0) * prow(PV_KA))
        tile.update(r=r, v=v, k2=k2, a_vec=-kk, b_vec=kk * a_i, logd=logd, g=g,
                    bonus=head_sum(r * k2 * prow(PV_RK)) * v, gnw=prow(PV_GNW), gnb=prow(PV_GNB))

    ci = lax.broadcasted_iota(jnp.int32, (CHUNK, CHUNK), 0)
    cj = lax.broadcasted_iota(jnp.int32, (CHUNK, CHUNK), 1)
    tri_incl = jnp.where(ci >= cj, 1.0, 0.0).astype(BF16)
    head0 = lax.broadcasted_iota(jnp.int32, (CHUNK, LANES), 1) < HEAD_DIM

    def stack(x):
        return jnp.concatenate([jnp.where(head0, x, 0.0), jnp.where(head0, 0.0, x)], axis=0)

    def prep(c):
        def run():
            sl = slice(c * CHUNK, (c + 1) * CHUNK)
            ld = tile['logd'][sl]
            ld_hi, ld_lo = _split(ld)
            lp = (jnp.dot(tri_incl, ld_hi, preferred_element_type=F32)
                  + jnp.dot(tri_incl, ld_lo, preferred_element_type=F32))
            le = lp[CHUNK - 1:CHUNK, :]
            e_p = jnp.exp(lp)
            e_n = jnp.exp(-lp)
            e_e = jnp.exp(le - lp)
            b_c, k_c = tile['b_vec'][sl], tile['k2'][sl]
            opf_sc[wr, OP_RH, c] = stack(tile['r'][sl] * e_p)
            opf_sc[wr, OP_AH, c] = stack(tile['a_vec'][sl] * jnp.exp(lp - ld))
            opb_sc[wr, OP_BT, c] = stack(b_c * e_n).astype(BF16)
            opb_sc[wr, OP_KT, c] = stack(k_c * e_n).astype(BF16)
            opb_sc[wr, OP_VV, c] = stack(tile['v'][sl]).astype(BF16)
            opb_sc[wr, OP_BBT, c] = stack(b_c * e_e).T.astype(BF16)
            opb_sc[wr, OP_KBT, c] = stack(k_c * e_e).T.astype(BF16)
            el_sc[wr, c] = jnp.broadcast_to(jnp.exp(le), (SUBLANES, LANES))
        return run

    def substages():
        ops = [dict(c=c) for c in chunks]

        def gram():
            for o in ops:
                c = o['c']
                ar = jnp.concatenate([opf_sc[rd, OP_AH, c], opf_sc[rd, OP_RH, c]], axis=0).astype(BF16)
                gb = _mm_nt(ar, opb_sc[rd, OP_BT, c])
                gk = _mm_nt(ar, opb_sc[rd, OP_KT, c])
                o['a_ab'] = jnp.where(strict, gb[0:LANES], 0.0)
                o['a_rb'] = jnp.where(incl, gb[LANES:], 0.0).astype(BF16)
                o['a_ak'] = jnp.where(strict, gk[0:LANES], 0.0).astype(BF16)
                o['a_rk'] = jnp.where(incl, gk[LANES:], 0.0).astype(BF16)

        def values():
            for o in ops:
                vv = opb_sc[rd, OP_VV, o['c']]
                o['x'] = _mm(o['a_ak'], vv)
                o['yv'] = _mm(o['a_rk'], vv)
                o['cv'] = _mm(opb_sc[rd, OP_KBT, o['c']], vv)
                o['t'] = eye_f + jnp.where(off_masks[0], o['a_ab'], 0.0)

        def level_a(off):
            def run():
                for o in ops:
                    o['ta'] = _mm(o['t'], jnp.where(off, o['a_ab'], 0.0))
            return run

        def level_b():
            for o in ops:
                o['t'] = o['t'] + _mm(o['ta'], o['t'])

        def apply_inverse():
            for o in ops:
                ah_x = jnp.concatenate([opf_sc[rd, OP_AH, o['c']], o['x']], axis=1)
                o['w'] = _mm(o['t'], ah_x).astype(BF16)

        def emit():
            for o in ops:
                c = o['c']
                qy = _mm(o['a_rb'], o['w'])
                mc = _mm(opb_sc[rd, OP_BBT, c], o['w'])
                diag = jnp.where(eye, el_sc[rd, c][0:1, :], 0.0)
                qm_sc[c] = jnp.concatenate([opf_sc[rd, OP_RH, c] + qy[:, 0:LANES], diag + mc[:, 0:LANES]],
                                           axis=0).astype(BF16)
                yin_sc[c] = qy[:, LANES:] + o['yv']
                cc_sc[c] = mc[:, LANES:] + o['cv']

        steps = [gram, values]
        for off in off_masks[1:]:
            steps += [level_a(off), level_b]
        return steps + [apply_inverse, emit]

    chain = dict(s=jnp.where(chain_tile == 0, 0.0, s_sc[...]), ys=[])

    def link(c):
        def run():
            res = _mm(qm_sc[c], chain['s'])
            y2 = res[0:LANES] + yin_sc[c]
            chain['ys'].append(y2[0:CHUNK] + y2[CHUNK:2 * CHUNK])
            chain['s'] = res[LANES:] + cc_sc[c]
        return run

    def finish_chain():
        s_sc[...] = chain['s']
        y = jnp.concatenate(chain['ys'], axis=0)
        mean = head_sum(y) * (1.0 / HEAD_DIM)
        dlt = y - mean
        var = head_sum(dlt * dlt) * (1.0 / HEAD_DIM)
        yn = dlt * lax.rsqrt(var + GROUPNORM_EPS) * gn_sc[0:1, :] + gn_sc[1:2, :]
        yr_ref[...] = (yn + bg_sc[0]) * bg_sc[1]

    def cast(j):
        def run():
            cast_dst[j][...] = cast_src[j][...].astype(BF16)
        return run

    steps = substages()
    fill = [prep_tile_wide]
    for c in chunks:
        fill += [link(c), prep(c)] + [cast(j) for j in range(n_cast)[c::len(chunks)]]
    fill += [finish_chain]
    gaps = len(steps) - 1
    for i, run in enumerate(steps):
        run()
        for item in fill[i * len(fill) // gaps:(i + 1) * len(fill) // gaps] if i < gaps else ():
            item()
    bg_sc[...] = bgp_sc[...]
    gn_sc[...] = gnp_sc[...]
    bgp_sc[0] = tile['bonus']
    bgp_sc[1] = tile['g']
    gnp_sc[0:1, :] = tile['gnw']
    gnp_sc[1:2, :] = tile['gnb']


RWKV_TILE = 512
RWKV_STAGES = 3


def _rwkv_work_steps(t, c):
    return (c // LANES) * (t // RWKV_TILE)


def _cast_job(src, lo, hi, steps):
    rows = hi - lo
    count = max(n for n in range(1, steps + 1) if rows % n == 0 and (rows // n) % 16 == 0)
    blk = rows // count
    assert lo % blk == 0
    return (src, blk, lo // blk, count)


def _rwkv(g_main, g_lr, v_first, w2s, pvec, has_vres, casts=()):
    t = g_main.shape[0]
    c = pvec.shape[1]
    tt = RWKV_TILE
    ntile = t // tt
    n_work = _rwkv_work_steps(t, c)
    nsteps = n_work + RWKV_STAGES - 1
    col0 = c // LANES
    nch = tt // CHUNK

    def prep_of(s):
        ss = jnp.minimum(s, n_work - 1)
        return ss // ntile, ss % ntile

    def chain_of(s):
        cs = jnp.maximum(s - (RWKV_STAGES - 1), 0)
        return cs // ntile, cs % ntile

    def in_map(col_mult):
        return lambda s: (prep_of(s)[1], col_mult * col0 + prep_of(s)[0])

    in_specs = [pl.BlockSpec((tt, LANES), in_map(1)),
                pl.BlockSpec((tt, LANES), in_map(2)),
                pl.BlockSpec((tt, LANES), in_map(3)),
                pl.BlockSpec((tt, 2 * LR_PAD), lambda s: (prep_of(s)[1], 0))]
    args = [g_main, g_main, g_main, g_lr]
    if has_vres:
        in_specs.append(pl.BlockSpec((tt, LANES), in_map(0)))
        args.append(v_first)
    in_specs += [pl.BlockSpec((768, LANES), lambda s: (0, prep_of(s)[0])),
                 pl.BlockSpec((PV_ROWS, LANES), lambda s: (0, prep_of(s)[0]))]
    args += [w2s, pvec]
    yr_spec = pl.BlockSpec((tt, LANES), lambda s: (chain_of(s)[1], chain_of(s)[0]))
    out_shape = jax.ShapeDtypeStruct((t, c), F32)
    if has_vres:
        out_specs, out_shapes = [yr_spec], [out_shape]
    else:
        out_specs, out_shapes = [yr_spec, pl.BlockSpec((tt, LANES), in_map(0))], [out_shape, out_shape]
    for src, rows, first, count in casts:
        blk = (rows, src.shape[1])
        in_specs.append(pl.BlockSpec(blk, lambda s, first=first, count=count: (first + jnp.minimum(s, count - 1), 0)))
        args.append(src)
        out_specs.append(pl.BlockSpec(blk, lambda s, count=count: (jnp.minimum(s, count - 1), 0)))
        out_shapes.append(jax.ShapeDtypeStruct((rows * count, src.shape[1]), BF16))
    return pl.pallas_call(
        functools.partial(_rwkv_kernel, tt=tt, ntile=ntile, nsteps=nsteps, has_vres=has_vres, n_cast=len(casts)),
        grid=(nsteps,),
        in_specs=in_specs,
        out_specs=out_specs,
        out_shape=out_shapes,
        scratch_shapes=[pltpu.VMEM((LANES, LANES), F32),
                        pltpu.VMEM((SUBLANES, LANES), F32),
                        pltpu.VMEM((SUBLANES, LR_PAD), F32),
                        pltpu.VMEM((nch, 2 * LANES, LANES), BF16),
                        pltpu.VMEM((nch, LANES, LANES), F32),
                        pltpu.VMEM((nch, LANES, LANES), F32),
                        pltpu.VMEM((2, tt, LANES), F32),
                        pltpu.VMEM((SUBLANES, LANES), F32),
                        pltpu.VMEM((2, 2, nch, LANES, LANES), F32),
                        pltpu.VMEM((2, 5, nch, LANES, LANES), BF16),
                        pltpu.VMEM((2, nch, SUBLANES, LANES), F32),
                        pltpu.VMEM((2, tt, LANES), F32),
                        pltpu.VMEM((SUBLANES, LANES), F32)],
        compiler_params=_cparams(("arbitrary",)),
        name="rwkv7_vres" if has_vres else "rwkv7",
    )(*args)


def _mix_kernel(*refs, tm, n_exp):
    has_router = n_exp is not None
    if has_router:
        (x_ref, pp_ref, halo_ref, glp_ref, glr_ref, yr_ref, pw_ref, ps_ref, po_ref, ro_ref, mo_ref,
         g1_ref, sc2_ref, sh2_ref, rt_ref, xo_ref, h2_ref, idx_ref, gt_ref, ext) = refs
    else:
        (x_ref, pp_ref, halo_ref, glp_ref, glr_ref, yr_ref, pw_ref, ps_ref, po_ref, ro_ref, mo_ref,
         g1_ref, sc2_ref, sh2_ref, xo_ref, h2_ref, ext) = refs
    i = pl.program_id(0)
    ext[0:POOL_HALO, :] = jnp.where(i == 0, 0.0, halo_ref[...].astype(F32))
    ext[POOL_HALO:POOL_HALO + tm, :] = pp_ref[...].astype(F32)
    pos = i * tm + lax.broadcasted_iota(jnp.int32, (tm, 1), 0)
    gw = ext.shape[1] // len(POOL_WINDOWS)
    parts = []
    for gi, win in enumerate(POOL_WINDOWS):
        cs = slice(gi * gw, (gi + 1) * gw)
        cur = ext[POOL_HALO:POOL_HALO + tm, cs]
        s = cur
        for j in range(1, win):
            s = s + ext[POOL_HALO - j:POOL_HALO - j + tm, cs]
        cnt = jnp.minimum(pos + 1, win).astype(F32)
        pooled = s / cnt - cur
        parts.append(_mm(pooled, pw_ref[gi]))
    mixed = jnp.concatenate(parts, axis=-1) * ps_ref[...]
    y_pool = _mm(mixed, po_ref[...])
    y_rwkv = _mm(yr_ref[...], ro_ref[...])
    merged = (jax.nn.sigmoid(glp_ref[...].astype(F32)) * y_pool
              + jax.nn.sigmoid(glr_ref[...].astype(F32)) * y_rwkv)
    xn = x_ref[...] + g1_ref[...] * _mm(merged, mo_ref[...])
    xo_ref[...] = xn
    ms = jnp.mean(xn * xn, axis=-1, keepdims=True)
    h2 = xn * lax.rsqrt(ms + NORM_EPS) * (1.0 + sc2_ref[...]) + sh2_ref[...]
    if has_router:
        slab = h2.shape[1] // LANES
        for a in range(slab):
            h2_ref[pl.ds(a, tm, stride=slab), :] = h2[:, a * LANES:(a + 1) * LANES]
        hi, lo = _split(h2)
        rhi, rlo = _split(rt_ref[...])
        hi_r = jnp.dot(hi, jnp.concatenate([rhi, rlo], axis=1), preferred_element_type=F32)
        logits = hi_r[:, 0:LANES] + hi_r[:, LANES:] + jnp.dot(lo, rhi, preferred_element_type=F32)
        lane = lax.broadcasted_iota(jnp.int32, logits.shape, 1)
        lane_f = lane.astype(F32)
        big = float(LANES)
        lg = jnp.where(lane < n_exp, logits, -jnp.inf)
        m1 = jnp.max(lg, axis=-1, keepdims=True)
        i1 = jnp.min(jnp.where(lg == m1, lane_f, big), axis=-1, keepdims=True)
        lg2 = jnp.where(lane_f == i1, -jnp.inf, lg)
        m2 = jnp.max(lg2, axis=-1, keepdims=True)
        i2 = jnp.min(jnp.where(lg2 == m2, lane_f, big), axis=-1, keepdims=True)
        e2 = jnp.exp(m2 - m1)
        p1 = 1.0 / (1.0 + e2)
        p2 = e2 / (1.0 + e2)
        idx_ref[...] = jnp.where(lane == 0, i1, jnp.where(lane == 1, i2, 0.0)).astype(jnp.int32)
        gt_ref[...] = jnp.where(lane == 0, p1, jnp.where(lane == 1, p2, 0.0))
    else:
        h2_ref[...] = h2.astype(h2_ref.dtype)


def _mix(x, g_main, yr, pool_w, pool_scale, pool_out, rk_out, mix_out, gate1, scale2, shift2, router_pad, n_exp):
    t, d = x.shape
    c = yr.shape[1]
    has_router = router_pad is not None
    tm = 256 if has_router else 512
    once = dict(pipeline_mode=pl.Buffered(1))
    halo_blocks = tm // POOL_HALO
    cb_glp = (4 * c) // d
    in_specs = [pl.BlockSpec((tm, d), lambda i: (i, 0)),
                pl.BlockSpec((tm, c), lambda i: (i, 0)),
                pl.BlockSpec((POOL_HALO, c), lambda i: (jnp.maximum(i * halo_blocks - 1, 0), 0)),
                pl.BlockSpec((tm, d), lambda i: (i, cb_glp)),
                pl.BlockSpec((tm, d), lambda i: (i, cb_glp + 1)),
                pl.BlockSpec((tm, c), lambda i: (i, 0)),
                pl.BlockSpec(pool_w.shape, lambda i: (0, 0, 0), **once),
                pl.BlockSpec((1, c), lambda i: (0, 0)),
                pl.BlockSpec((c, d), lambda i: (0, 0), **once),
                pl.BlockSpec((c, d), lambda i: (0, 0), **once),
                pl.BlockSpec((d, d), lambda i: (0, 0), **once),
                pl.BlockSpec((1, d), lambda i: (0, 0)),
                pl.BlockSpec((1, d), lambda i: (0, 0)),
                pl.BlockSpec((1, d), lambda i: (0, 0))]
    args = [x, g_main, g_main, g_main, g_main, yr, pool_w, pool_scale, pool_out, rk_out, mix_out,
            gate1, scale2, shift2]
    out_specs = [pl.BlockSpec((tm, d), lambda i: (i, 0))]
    out_shape = [jax.ShapeDtypeStruct((t, d), F32)]
    if has_router:
        slab = d // LANES
        out_specs.append(pl.BlockSpec((tm * slab, LANES), lambda i: (i, 0)))
        out_shape.append(jax.ShapeDtypeStruct((t * slab, LANES), F32))
    else:
        out_specs.append(pl.BlockSpec((tm, d), lambda i: (i, 0)))
        out_shape.append(jax.ShapeDtypeStruct((t, d), BF16))
    if has_router:
        in_specs.append(pl.BlockSpec((d, LANES), lambda i: (0, 0)))
        args.append(router_pad)
        out_specs += [pl.BlockSpec((tm, LANES), lambda i: (i, 0)), pl.BlockSpec((tm, LANES), lambda i: (i, 0))]
        out_shape += [jax.ShapeDtypeStruct((t, LANES), jnp.int32), jax.ShapeDtypeStruct((t, LANES), F32)]
    return pl.pallas_call(
        functools.partial(_mix_kernel, tm=tm, n_exp=n_exp if has_router else None),
        grid=(t // tm,),
        in_specs=in_specs,
        out_specs=out_specs,
        out_shape=out_shape,
        scratch_shapes=[pltpu.VMEM((POOL_HALO + tm, c), F32)],
        compiler_params=_cparams(("arbitrary",), VMEM_LIMIT_HIGH),
        name="mix_router" if has_router else "mix",
    )(*args)


def _ffn_kernel(h_ref, x_ref, g_ref, w1_ref, w3_ref, w2_ref, o_ref, acc):
    f = pl.program_id(1)

    @pl.when(f == 0)
    def _():
        acc[...] = jnp.zeros_like(acc)

    h = h_ref[...]
    a = jnp.dot(h, w1_ref[...].astype(BF16), preferred_element_type=F32)
    b = jnp.dot(h, w3_ref[...].astype(BF16), preferred_element_type=F32)
    u = (a * jax.nn.sigmoid(a) * b).astype(BF16)
    acc[...] += jnp.dot(u, w2_ref[...].astype(BF16), preferred_element_type=F32)

    @pl.when(f == pl.num_programs(1) - 1)
    def _():
        o_ref[...] = x_ref[...] + g_ref[...] * acc[...]


def _ffn(h2, x, gate2, w1, w3, w2):
    t, d = x.shape
    ff = w1.shape[1]
    tm, tf = 512, 512
    return pl.pallas_call(
        _ffn_kernel,
        grid=(t // tm, ff // tf),
        in_specs=[pl.BlockSpec((tm, d), lambda i, f: (i, 0)),
                  pl.BlockSpec((tm, d), lambda i, f: (i, 0)),
                  pl.BlockSpec((1, d), lambda i, f: (0, 0)),
                  pl.BlockSpec((d, tf), lambda i, f: (0, f)),
                  pl.BlockSpec((d, tf), lambda i, f: (0, f)),
                  pl.BlockSpec((tf, d), lambda i, f: (f, 0))],
        out_specs=pl.BlockSpec((tm, d), lambda i, f: (i, 0)),
        out_shape=jax.ShapeDtypeStruct((t, d), F32),
        scratch_shapes=[pltpu.VMEM((tm, d), F32)],
        compiler_params=_cparams(("arbitrary", "arbitrary")),
        name="ffn_dense",
    )(h2, x, gate2, w1, w3, w2)


def _slab_copy(src_hbm, src_row, dst, slot, dst_row, slab, sem):
    src = src_hbm.at[pl.ds(pl.multiple_of(src_row * slab, slab), slab), :]
    return pltpu.make_async_copy(src, dst.at[slot, pl.ds(pl.multiple_of(dst_row * slab, slab), slab), :],
                                 sem.at[slot])


def _moe_kernel(te_ref, tr_ref, na_ref, tok_ref, h_hbm, w1_ref, w3_ref, w2l_ref, w2h_ref,
                o_ref, xg, xb, acc, sem, *, tm, n_lo):
    b = pl.program_id(0)
    f = pl.program_id(1)
    nf = pl.num_programs(1)
    n_act = na_ref[0]
    slab = xb.shape[1] // LANES

    def start_gather(tile):
        def body(s, carry):
            _slab_copy(h_hbm, tok_ref[tile * tm + s], xg, 0, s, slab, sem).start()
            return carry
        lax.fori_loop(0, tm, body, 0, unroll=8)

    @pl.when(jnp.logical_and(f == 0, b < n_act))
    def _():
        @pl.when(b == 0)
        def _():
            start_gather(0)

        pltpu.make_async_copy(h_hbm.at[pl.ds(0, tm * slab), :], xg.at[0], sem.at[0]).wait()
        for a in range(slab):
            xb[:, a * LANES:(a + 1) * LANES] = xg[0, pl.ds(a, tm, stride=slab), :].astype(BF16)
        acc[...] = jnp.zeros_like(acc)

        @pl.when(b + 1 < n_act)
        def _():
            start_gather(b + 1)

    def swiglu_step(w2_ref, rows):
        x = xb[0:rows, :]
        a_ = jnp.dot(x, w1_ref[0].astype(BF16), preferred_element_type=F32)
        b_ = jnp.dot(x, w3_ref[0].astype(BF16), preferred_element_type=F32)
        u = (a_ * jax.nn.sigmoid(a_) * b_).astype(BF16)
        acc[0:rows, :] += jnp.dot(u, w2_ref[0].astype(BF16), preferred_element_type=F32)

    in_lo = te_ref[b] < n_lo
    for rows in range(MOE_ROW_STEP, tm + 1, MOE_ROW_STEP):
        has_rows = jnp.logical_and(b < n_act, tr_ref[b] == rows)

        @pl.when(jnp.logical_and(has_rows, in_lo))
        def _():
            swiglu_step(w2l_ref, rows)

        @pl.when(jnp.logical_and(has_rows, jnp.logical_not(in_lo)))
        def _():
            swiglu_step(w2h_ref, rows)

    @pl.when(f == nf - 1)
    def _():
        res = jnp.where(b < n_act, acc[...], 0.0)
        for a in range(slab):
            o_ref[pl.ds(a, tm, stride=slab), :] = res[:, a * LANES:(a + 1) * LANES]


MOE_ROW_STEP = 128


def _moe(h2_slab, tile_e, tile_rows, n_active, slot_tok, w1, w3, w2_lo, w2_hi, tm, n_tiles):
    n_exp, d, ff = w1.shape
    n_lo = w2_lo.shape[0]
    tf = 1024
    nf = ff // tf

    def w13_map(b, f, te, tr, na, tok):
        return (te[b], 0, jnp.where(b < na[0], f, nf - 1))

    def w2_map(lo):
        def index(b, f, te, tr, na, tok):
            e = te[b]
            mine = jnp.logical_and(b < na[0], (e < n_lo) if lo else (e >= n_lo))
            e_loc = jnp.minimum(e, n_lo - 1) if lo else jnp.maximum(e - n_lo, 0)
            return (e_loc, jnp.where(mine, f, (nf - 1) if lo else 0), 0)
        return index

    grid_spec = pltpu.PrefetchScalarGridSpec(
        num_scalar_prefetch=4,
        grid=(n_tiles, nf),
        in_specs=[pl.BlockSpec(memory_space=pl.ANY),
                  pl.BlockSpec((1, d, tf), w13_map), pl.BlockSpec((1, d, tf), w13_map),
                  pl.BlockSpec((1, tf, d), w2_map(True)), pl.BlockSpec((1, tf, d), w2_map(False))],
        out_specs=pl.BlockSpec((tm * (d // LANES), LANES), lambda b, f, te, tr, na, tok: (b, 0)),
        scratch_shapes=[pltpu.VMEM((1, tm * (d // LANES), LANES), F32),
                        pltpu.VMEM((tm, d), BF16),
                        pltpu.VMEM((tm, d), F32),
                        pltpu.SemaphoreType.DMA((1,))],
    )
    return pl.pallas_call(
        functools.partial(_moe_kernel, tm=tm, n_lo=n_lo),
        grid_spec=grid_spec,
        out_shape=jax.ShapeDtypeStruct((n_tiles * tm * (d // LANES), LANES), F32),
        compiler_params=_cparams(("arbitrary", "arbitrary")),
        name="moe_experts",
    )(tile_e, tile_rows, n_active, slot_tok, h2_slab, w1, w3, w2_lo, w2_hi)


def _comb_kernel(dest_ref, yb_hbm, x_ref, gt_ref, g2_ref, fg_ref, o_ref, buf, sem, *, tm):
    i = pl.program_id(0)
    n = pl.num_programs(0)
    slot = i % 2
    nrow = TOP_K * tm
    slab = o_ref.shape[1] // LANES

    def start_gather(tile, slt):
        def body(s, carry):
            _slab_copy(yb_hbm, dest_ref[tile * nrow + s], buf, slt, s, slab, sem).start()
            return carry
        lax.fori_loop(0, nrow, body, 0, unroll=8)

    @pl.when(i == 0)
    def _():
        start_gather(0, 0)

    @pl.when(i + 1 < n)
    def _():
        start_gather(i + 1, 1 - slot)

    pltpu.make_async_copy(yb_hbm.at[pl.ds(0, nrow * slab), :], buf.at[slot], sem.at[slot]).wait()
    gt = gt_ref[...]
    p0 = gt[:, 0:1]
    p1 = gt[:, 1:2]
    cols = []
    for a in range(slab):
        y0 = buf[slot, pl.ds(a, tm, stride=TOP_K * slab), :]
        y1 = buf[slot, pl.ds(slab + a, tm, stride=TOP_K * slab), :]
        cols.append(y0 * p0 + y1 * p1)
    fmoe = jnp.concatenate(cols, axis=-1)
    xn = x_ref[...] + g2_ref[...] * fmoe
    ms = jnp.mean(xn * xn, axis=-1, keepdims=True)
    o_ref[...] = xn * lax.rsqrt(ms + NORM_EPS) * fg_ref[...]


def _combine(dest, yb_slab, x, gates, gate2, final_gain):
    t, d = x.shape
    tm = 256
    grid_spec = pltpu.PrefetchScalarGridSpec(
        num_scalar_prefetch=1,
        grid=(t // tm,),
        in_specs=[pl.BlockSpec(memory_space=pl.ANY),
                  pl.BlockSpec((tm, d), lambda i, dr: (i, 0)),
                  pl.BlockSpec((tm, LANES), lambda i, dr: (i, 0)),
                  pl.BlockSpec((1, d), lambda i, dr: (0, 0)),
                  pl.BlockSpec((1, d), lambda i, dr: (0, 0))],
        out_specs=pl.BlockSpec((tm, d), lambda i, dr: (i, 0)),
        scratch_shapes=[pltpu.VMEM((2, TOP_K * tm * (d // LANES), LANES), F32),
                        pltpu.SemaphoreType.DMA((2,))],
    )
    return pl.pallas_call(
        functools.partial(_comb_kernel, tm=tm),
        grid_spec=grid_spec,
        out_shape=jax.ShapeDtypeStruct((t, d), F32),
        compiler_params=_cparams(("arbitrary",)),
        name="moe_combine",
    )(dest, yb_slab, x, gates, gate2, final_gain)


def _pad_rows(w, rows):
    return jnp.pad(w, ((0, rows - w.shape[0]), (0, 0)))


def _lowrank_weights(l, rk_lerp, rk_w1, rk_a1, rk_g1, rk_vmu, rk_v1):
    mus = [rk_lerp[l, 0], rk_lerp[l, 1], rk_lerp[l, 2]]
    ws = [rk_w1[l], rk_a1[l], rk_g1[l]]
    if l > 0:
        mus.append(rk_vmu[l - 1])
        ws.append(rk_v1[l - 1])
    wa = jnp.concatenate([(1.0 - m)[:, None] * w for m, w in zip(mus, ws)], axis=1)
    wb = jnp.concatenate([m[:, None] * w for m, w in zip(mus, ws)], axis=1)
    wa = jnp.pad(wa, ((0, 0), (0, LR_PAD - wa.shape[1])))
    wb = jnp.pad(wb, ((0, 0), (0, LR_PAD - wb.shape[1])))
    return jnp.concatenate([wa, wb], axis=1).astype(BF16)


def _second_stage_weights(l, rk_w2, rk_a2, rk_g2, rk_v2):
    c = rk_w2.shape[2]
    w2 = _pad_rows(rk_w2[l], 128)
    a2 = jnp.concatenate([jnp.zeros((64, c), F32), rk_a2[l]], axis=0)
    g2 = _pad_rows(rk_g2[l], 256)
    if l > 0:
        v2 = jnp.concatenate([jnp.zeros((160, c), F32), rk_v2[l - 1], jnp.zeros((64, c), F32)], axis=0)
    else:
        v2 = jnp.zeros((256, c), F32)
    return jnp.concatenate([w2, a2, g2, v2], axis=0).astype(BF16)


def _route(idx, tm, n_exp, n_tiles):
    n_assign = idx.size
    flat_e = idx.reshape(-1)
    onehot = (flat_e[:, None] == jnp.arange(n_exp, dtype=jnp.int32)[None, :]).astype(jnp.int32)
    csum = jnp.cumsum(onehot, axis=0)
    rank = jnp.take_along_axis(csum, flat_e[:, None], axis=1)[:, 0] - 1
    counts = csum[-1]
    padded = (counts + tm - 1) // tm * tm
    pad_end = jnp.cumsum(padded)
    pad_start = pad_end - padded
    dest = (pad_start[flat_e] + rank).astype(jnp.int32)
    slot_tok = jnp.zeros((n_tiles * tm,), jnp.int32).at[dest].set(
        jnp.arange(n_assign, dtype=jnp.int32) // TOP_K)
    n_active = (pad_end[-1] // tm).astype(jnp.int32)
    tile_ids = jnp.minimum(jnp.arange(n_tiles, dtype=jnp.int32), jnp.maximum(n_active - 1, 0))
    tile_e = jnp.minimum(jnp.searchsorted(pad_end, tile_ids * tm, side='right'), n_exp - 1).astype(jnp.int32)
    filled = counts[tile_e] - (tile_ids * tm - pad_start[tile_e])
    tile_rows = jnp.clip((filled + MOE_ROW_STEP - 1) // MOE_ROW_STEP * MOE_ROW_STEP, MOE_ROW_STEP, tm)
    return dest, slot_tok, tile_e, tile_rows.astype(jnp.int32), n_active.reshape(1)


def kernel(x, c, ada_w, ada_b, w_in, pool_w, pool_scale, pool_out, rk_shift, rk_lerp, rk_w0, rk_w1, rk_w2, rk_a0, rk_a1, rk_a2, rk_g1, rk_g2, rk_kk, rk_ka, rk_rk, rk_gn_w, rk_gn_b, rk_vmu, rk_v0, rk_v1, rk_v2, rk_out, mix_out, ffn_w1, ffn_w3, ffn_w2, router, moe_w1, moe_w3, moe_w2, final_gain):
    bsz, t, d = x.shape
    depth = ada_w.shape[0]
    cw = rk_w0.shape[1]
    n_exp = router.shape[2]
    assert bsz == 1 and depth == 2 and cw % LANES == 0
    xs = x.reshape(t, d)
    mod = _ada(c.reshape(d, 1), ada_w, ada_b)

    def mod_row(l, i):
        return mod[l, :, i * d:(i + 1) * d]

    v_first = None
    out = None
    for l in range(depth):
        w_lr = _lowrank_weights(l, rk_lerp, rk_w1, rk_a1, rk_g1, rk_vmu, rk_v1)
        if l == 0:
            g_main, g_lr = _in_proj(xs, mod_row(l, 1), mod_row(l, 0), w_in, l, w_lr)
        else:
            g_main, g_lr = _in_proj(xs, mod_row(l, 1), mod_row(l, 0), w_in_next.reshape(1, d, -1), 0, w_lr)
        w2s = _second_stage_weights(l, rk_w2, rk_a2, rk_g2, rk_v2)
        zeros_c = jnp.zeros((cw,), F32)
        rows = [rk_w0[l], rk_a0[l], rk_v0[l - 1] if l > 0 else zeros_c, rk_kk[l], rk_ka[l],
                rk_gn_w[l], rk_gn_b[l], rk_shift[l, 0], rk_shift[l, 1], rk_shift[l, 2],
                rk_rk[l].reshape(cw)]
        pvec = _pad_rows(jnp.stack(rows, axis=0), PV_ROWS)
        steps = _rwkv_work_steps(t, cw)
        mixer = [w.reshape(-1, w.shape[-1]) for w in (pool_w, pool_out, rk_out, mix_out)]
        mixer_casts = [_cast_job(w, l * (w.shape[0] // depth), (l + 1) * (w.shape[0] // depth), steps)
                       for w in mixer]
        if l == 0:
            dense = [ffn_w1[0], ffn_w3[0], ffn_w2[0]]
            experts = [moe_w1[0].reshape(-1, moe_w1.shape[-1]), moe_w3[0].reshape(-1, moe_w3.shape[-1]),
                       moe_w2[0].reshape(-1, moe_w2.shape[-1])]
            e1, e3, e2 = experts
            casts = ([_cast_job(w, 0, w.shape[0], steps) for w in dense]
                     + [_cast_job(e1, 0, e1.shape[0], steps), _cast_job(e2, 0, e2.shape[0] // 2, steps)]
                     + [_cast_job(w_in.reshape(-1, w_in.shape[-1]), d, 2 * d, steps)] + mixer_casts)
            yr, v_first, *cast_out = _rwkv(g_main, g_lr, None, w2s, pvec, has_vres=False, casts=casts)
            dense_bf, (e1_bf, e2_lo), w_in_next, mixer_bf = cast_out[:3], cast_out[3:5], cast_out[5], cast_out[6:]
        else:
            casts = [_cast_job(e3, 0, e3.shape[0], steps),
                     _cast_job(e2, e2.shape[0] // 2, e2.shape[0], steps)] + mixer_casts
            yr, e3_bf, e2_hi, *mixer_bf = _rwkv(g_main, g_lr, v_first, w2s, pvec, has_vres=True, casts=casts)
        is_moe = (l % 2 == 1)
        router_pad = None
        if is_moe:
            router_pad = jnp.pad(router[l // 2], ((0, 0), (0, LANES - n_exp)))
        res = _mix(xs, g_main, yr, mixer_bf[0].reshape(pool_w.shape[1:]), pool_scale[l].reshape(1, -1),
                   mixer_bf[1], mixer_bf[2], mixer_bf[3], mod_row(l, 2), mod_row(l, 4), mod_row(l, 3), router_pad,
                   n_exp)
        if not is_moe:
            x_new, h2 = res
            xs = _ffn(h2, x_new, mod_row(l, 5), *dense_bf)
        else:
            x_new, h2, idx_pad, gate_pad = res
            tm_moe = 512
            n_tiles = (t * TOP_K) // tm_moe + n_exp
            dest, slot_tok, tile_e, tile_rows, n_active = _route(idx_pad[:, :TOP_K], tm_moe, n_exp, n_tiles)
            n_lo = n_exp // 2
            yb_slab = _moe(h2, tile_e, tile_rows, n_active, slot_tok,
                           e1_bf.reshape(moe_w1.shape[1:]), e3_bf.reshape(moe_w3.shape[1:]),
                           e2_lo.reshape((n_lo,) + moe_w2.shape[2:]),
                           e2_hi.reshape((n_exp - n_lo,) + moe_w2.shape[2:]), tm_moe, n_tiles)
            out = _combine(dest, yb_slab, x_new, gate_pad, mod_row(l, 5), final_gain.reshape(1, d))
    return out.reshape(bsz, t, d)
```

```python
import functools

import jax
import jax.numpy as jnp
from jax import lax
from jax.experimental import pallas as pl
from jax.experimental.pallas import tpu as pltpu

F32 = jnp.float32
BF16 = jnp.bfloat16

NORM_EPS = 1e-6
GROUPNORM_EPS = 64e-5
L2_EPS = 1e-12
POOL_WINDOWS = (2, 4, 8, 16)
HEAD_DIM = 64
TOP_K = 2
LANES = 128
SUBLANES = 8
CHUNK = 64
LR_PAD = 384
POOL_HALO = 16
VMEM_LIMIT = 56 * 1024 * 1024
VMEM_LIMIT_HIGH = 60 * 1024 * 1024


def _cparams(sem, vmem_limit=VMEM_LIMIT):
    return pltpu.CompilerParams(dimension_semantics=sem, vmem_limit_bytes=vmem_limit)


def _mm(a, b):
    return jnp.dot(a.astype(BF16), b.astype(BF16), preferred_element_type=F32)


def _mm_nt(a, b):
    return lax.dot_general(a.astype(BF16), b.astype(BF16), (((1,), (1,)), ((), ())),
                           preferred_element_type=F32)


def _split(x):
    hi = x.astype(BF16)
    lo = (x - hi.astype(F32)).astype(BF16)
    return hi, lo


def _ada_kernel(c_ref, w_ref, b_ref, o_ref):
    c = c_ref[...]
    act = c * jax.nn.sigmoid(c)
    o_ref[0] = jnp.sum(act * w_ref[0], axis=0, keepdims=True) + b_ref[0]


def _ada(c_col, ada_w, ada_b):
    depth, d, n = ada_w.shape
    tn = 2048
    return pl.pallas_call(
        _ada_kernel,
        grid=(depth, n // tn),
        in_specs=[pl.BlockSpec((d, 1), lambda l, j: (0, 0)),
                  pl.BlockSpec((1, d, tn), lambda l, j: (l, 0, j)),
                  pl.BlockSpec((1, 1, tn), lambda l, j: (l, 0, j))],
        out_specs=pl.BlockSpec((1, 1, tn), lambda l, j: (l, 0, j)),
        out_shape=jax.ShapeDtypeStruct((depth, 1, n), F32),
        compiler_params=_cparams(("arbitrary", "arbitrary")),
        name="ada_gemv",
    )(c_col, ada_w, ada_b.reshape(depth, 1, n))


def _in_kernel(x_ref, sc_ref, sh_ref, w_ref, wlr_ref, o_ref, olr_ref, h_sc):
    @pl.when(pl.program_id(1) == 0)
    def _():
        x = x_ref[...]
        ms = jnp.mean(x * x, axis=-1, keepdims=True)
        h = x * lax.rsqrt(ms + NORM_EPS) * (1.0 + sc_ref[...]) + sh_ref[...]
        hb = h.astype(BF16)
        h_sc[...] = hb
        olr_ref[...] = jnp.dot(hb, wlr_ref[...], preferred_element_type=F32)

    o_ref[...] = jnp.dot(h_sc[...], w_ref[0].astype(BF16), preferred_element_type=F32).astype(o_ref.dtype)


def _in_proj(x, scale, shift, w_in, layer, w_lr):
    t, d = x.shape
    n = w_in.shape[2]
    nlr = w_lr.shape[1]
    tm, tn = 1024, 1024
    return pl.pallas_call(
        _in_kernel,
        grid=(t // tm, n // tn),
        in_specs=[pl.BlockSpec((tm, d), lambda i, j: (i, 0)),
                  pl.BlockSpec((1, d), lambda i, j: (0, 0)),
                  pl.BlockSpec((1, d), lambda i, j: (0, 0)),
                  pl.BlockSpec((1, d, tn), lambda i, j: (layer, 0, j)),
                  pl.BlockSpec((d, nlr), lambda i, j: (0, 0))],
        out_specs=[pl.BlockSpec((tm, tn), lambda i, j: (i, j)),
                   pl.BlockSpec((tm, nlr), lambda i, j: (i, 0))],
        out_shape=[jax.ShapeDtypeStruct((t, n), BF16), jax.ShapeDtypeStruct((t, nlr), F32)],
        scratch_shapes=[pltpu.VMEM((tm, d), BF16)],
        compiler_params=_cparams(("arbitrary", "arbitrary"), VMEM_LIMIT_HIGH),
        name="in_proj",
    )(x, scale, shift, w_in, w_lr)


PV_W0, PV_A0, PV_V0, PV_KK, PV_KA, PV_GNW, PV_GNB, PV_MUR, PV_MUK, PV_MUV, PV_RK = range(11)
PV_ROWS = 16


def _rwkv_kernel(*refs, tt, ntile, nsteps, has_vres, n_cast):
    n_in = 7 if has_vres else 6
    n_out = 1 if has_vres else 2
    ins = refs[:n_in]
    cast_src = refs[n_in:n_in + n_cast]
    outs = refs[n_in + n_cast:n_in + n_cast + n_out]
    cast_dst = refs[n_in + n_cast + n_out:n_in + 2 * n_cast + n_out]
    (s_sc, c_rkv, c_lr, qm_sc, yin_sc, cc_sc, bg_sc, gn_sc,
     opf_sc, opb_sc, el_sc, bgp_sc, gnp_sc) = refs[n_in + 2 * n_cast + n_out:]
    if has_vres:
        (r_ref, k_ref, v_ref, lr_ref, vf_ref, w2_ref, pv_ref), (yr_ref,) = ins, outs
    else:
        (r_ref, k_ref, v_ref, lr_ref, w2_ref, pv_ref), (yr_ref, vfo_ref) = ins, outs
    step = pl.program_id(0)
    n_work = nsteps - 2
    is_drain = step >= n_work
    prep_tile = jnp.minimum(step, n_work - 1) % ntile
    chain_tile = jnp.maximum(step - 2, 0) % ntile
    wr = step % 2
    rd = 1 - wr
    chunks = range(tt // CHUNK)

    @pl.when(step == 0)
    def _():
        for ref in (s_sc, qm_sc, yin_sc, cc_sc, bg_sc, gn_sc, opf_sc, opb_sc, el_sc, bgp_sc, gnp_sc):
            ref[...] = jnp.zeros_like(ref)

    @pl.when(prep_tile == 0)
    def _():
        c_rkv[...] = jnp.zeros_like(c_rkv)
        c_lr[...] = jnp.zeros_like(c_lr)

    li = lax.broadcasted_iota(jnp.int32, (LANES, LANES), 0)
    lj = lax.broadcasted_iota(jnp.int32, (LANES, LANES), 1)
    same_head = ((li >> 6) == (lj >> 6))
    head_ones = jnp.where(same_head, 1.0, 0.0).astype(BF16)
    strict = (li & (CHUNK - 1)) > (lj & (CHUNK - 1))
    incl = (li & (CHUNK - 1)) >= (lj & (CHUNK - 1))
    eye = li == lj
    eye_f = jnp.where(eye, 1.0, 0.0)
    off_masks = [((li >> (s + 1)) == (lj >> (s + 1))) & (((li >> s) & 1) == 1) & (((lj >> s) & 1) == 0)
                 for s in range(CHUNK.bit_length() - 1)]

    def head_sum(x):
        return _mm(x, head_ones)

    OP_RH, OP_AH = 0, 1
    OP_BT, OP_KT, OP_VV, OP_BBT, OP_KBT = range(5)
    tile = {}

    def prep_tile_wide():
        row = lax.broadcasted_iota(jnp.int32, (tt, 1), 0)

        def shift(x, carry_row):
            return jnp.where(row == 0, carry_row, pltpu.roll(x, 1, axis=0))

        pv = pv_ref[...]

        def prow(i):
            return pv[i:i + 1, :]

        r_raw, k_raw, v_raw = (ref[...].astype(F32) for ref in (r_ref, k_ref, v_ref))
        zb = lr_ref[:, LR_PAD:2 * LR_PAD]
        c_rkv_used = jnp.where(is_drain, c_rkv[4:7, :], c_rkv[0:3, :])
        c_z = jnp.where(is_drain, c_lr[1:2, :], c_lr[0:1, :])
        r_prev = shift(r_raw, c_rkv_used[0:1, :])
        k_prev = shift(k_raw, c_rkv_used[1:2, :])
        v_prev = shift(v_raw, c_rkv_used[2:3, :])
        z = lr_ref[:, 0:LR_PAD] + shift(zb, c_z)
        c_rkv[4:7, :] = c_rkv_used
        c_lr[1:2, :] = c_z
        c_rkv[0:1, :] = r_raw[tt - 1:tt, :]
        c_rkv[1:2, :] = k_raw[tt - 1:tt, :]
        c_rkv[2:3, :] = v_raw[tt - 1:tt, :]
        c_lr[0:1, :] = zb[tt - 1:tt, :]

        r = r_raw + (r_prev - r_raw) * prow(PV_MUR)
        k = k_raw + (k_prev - k_raw) * prow(PV_MUK)
        v = v_raw + (v_prev - v_raw) * prow(PV_MUV)

        z_wa = z[:, 0:LANES]
        z_gv = z[:, LANES:LR_PAD]
        w_pre = prow(PV_W0) + _mm(jnp.tanh(z_wa), w2_ref[0:128, :])
        neg = -w_pre
        softplus = jnp.maximum(neg, 0.0) + jnp.log(1.0 + jnp.exp(-jnp.abs(neg)))
        logd = -jnp.exp(-softplus - 0.5)
        a_i = jax.nn.sigmoid(prow(PV_A0) + _mm(z_wa, w2_ref[128:256, :]))
        g = _mm(jax.nn.sigmoid(z_gv), w2_ref[256:512, :])
        if has_vres:
            v = v + (vf_ref[...] - v) * jax.nn.sigmoid(prow(PV_V0) + _mm(z_gv, w2_ref[512:768, :]))
        else:
            vfo_ref[...] = v

        kk = k * prow(PV_KK)
        kk = kk / jnp.maximum(jnp.sqrt(head_sum(kk * kk)), L2_EPS)
        k2 = k * (1.0 + (a_i - 1.0) * prow(PV_KA))
        tile.update(r=r, v=v, k2=k2, a_vec=-kk, b_vec=kk * a_i, logd=logd, g=g,
                    bonus=head_sum(r * k2 * prow(PV_RK)) * v, gnw=prow(PV_GNW), gnb=prow(PV_GNB))

    ci = lax.broadcasted_iota(jnp.int32, (CHUNK, CHUNK), 0)
    cj = lax.broadcasted_iota(jnp.int32, (CHUNK, CHUNK), 1)
    tri_incl = jnp.where(ci >= cj, 1.0, 0.0).astype(BF16)
    head0 = lax.broadcasted_iota(jnp.int32, (CHUNK, LANES), 1) < HEAD_DIM

    def stack(x):
        return jnp.concatenate([jnp.where(head0, x, 0.0), jnp.where(head0, 0.0, x)], axis=0)

    def prep(c):
        def run():
            sl = slice(c * CHUNK, (c + 1) * CHUNK)
            ld = tile['logd'][sl]
            ld_hi, ld_lo = _split(ld)
            lp = (jnp.dot(tri_incl, ld_hi, preferred_element_type=F32)
                  + jnp.dot(tri_incl, ld_lo, preferred_element_type=F32))
            le = lp[CHUNK - 1:CHUNK, :]
            e_p = jnp.exp(lp)
            e_n = jnp.exp(-lp)
            e_e = jnp.exp(le - lp)
            b_c, k_c = tile['b_vec'][sl], tile['k2'][sl]
            opf_sc[wr, OP_RH, c] = stack(tile['r'][sl] * e_p)
            opf_sc[wr, OP_AH, c] = stack(tile['a_vec'][sl] * jnp.exp(lp - ld))
            opb_sc[wr, OP_BT, c] = stack(b_c * e_n).astype(BF16)
            opb_sc[wr, OP_KT, c] = stack(k_c * e_n).astype(BF16)
            opb_sc[wr, OP_VV, c] = stack(tile['v'][sl]).astype(BF16)
            opb_sc[wr, OP_BBT, c] = stack(b_c * e_e).T.astype(BF16)
            opb_sc[wr, OP_KBT, c] = stack(k_c * e_e).T.astype(BF16)
            el_sc[wr, c] = jnp.broadcast_to(jnp.exp(le), (SUBLANES, LANES))
        return run

    def substages():
        ops = [dict(c=c) for c in chunks]

        def gram():
            for o in ops:
                c = o['c']
                ar = jnp.concatenate([opf_sc[rd, OP_AH, c], opf_sc[rd, OP_RH, c]], axis=0).astype(BF16)
                gb = _mm_nt(ar, opb_sc[rd, OP_BT, c])
                gk = _mm_nt(ar, opb_sc[rd, OP_KT, c])
                o['a_ab'] = jnp.where(strict, gb[0:LANES], 0.0)
                o['a_rb'] = jnp.where(incl, gb[LANES:], 0.0).astype(BF16)
                o['a_ak'] = jnp.where(strict, gk[0:LANES], 0.0).astype(BF16)
                o['a_rk'] = jnp.where(incl, gk[LANES:], 0.0).astype(BF16)

        def values():
            for o in ops:
                vv = opb_sc[rd, OP_VV, o['c']]
                o['x'] = _mm(o['a_ak'], vv)
                o['yv'] = _mm(o['a_rk'], vv)
                o['cv'] = _mm(opb_sc[rd, OP_KBT, o['c']], vv)
                o['t'] = eye_f + jnp.where(off_masks[0], o['a_ab'], 0.0)

        def level_a(off):
            def run():
                for o in ops:
                    o['ta'] = _mm(o['t'], jnp.where(off, o['a_ab'], 0.0))
            return run

        def level_b():
            for o in ops:
                o['t'] = o['t'] + _mm(o['ta'], o['t'])

        def apply_inverse():
            for o in ops:
                ah_x = jnp.concatenate([opf_sc[rd, OP_AH, o['c']], o['x']], axis=1)
                o['w'] = _mm(o['t'], ah_x).astype(BF16)

        def emit():
            for o in ops:
                c = o['c']
                qy = _mm(o['a_rb'], o['w'])
                mc = _mm(opb_sc[rd, OP_BBT, c], o['w'])
                diag = jnp.where(eye, el_sc[rd, c][0:1, :], 0.0)
                qm_sc[c] = jnp.concatenate([opf_sc[rd, OP_RH, c] + qy[:, 0:LANES], diag + mc[:, 0:LANES]],
                                           axis=0).astype(BF16)
                yin_sc[c] = qy[:, LANES:] + o['yv']
                cc_sc[c] = mc[:, LANES:] + o['cv']

        steps = [gram, values]
        for off in off_masks[1:]:
            steps += [level_a(off), level_b]
        return steps + [apply_inverse, emit]

    chain = dict(s=jnp.where(chain_tile == 0, 0.0, s_sc[...]), ys=[])

    def link(c):
        def run():
            res = _mm(qm_sc[c], chain['s'])
            y2 = res[0:LANES] + yin_sc[c]
            chain['ys'].append(y2[0:CHUNK] + y2[CHUNK:2 * CHUNK])
            chain['s'] = res[LANES:] + cc_sc[c]
        return run

    def finish_chain():
        s_sc[...] = chain['s']
        y = jnp.concatenate(chain['ys'], axis=0)
        mean = head_sum(y) * (1.0 / HEAD_DIM)
        dlt = y - mean
        var = head_sum(dlt * dlt) * (1.0 / HEAD_DIM)
        yn = dlt * lax.rsqrt(var + GROUPNORM_EPS) * gn_sc[0:1, :] + gn_sc[1:2, :]
        yr_ref[...] = (yn + bg_sc[0]) * bg_sc[1]

    def cast(j):
        def run():
            cast_dst[j][...] = cast_src[j][...].astype(BF16)
        return run

    steps = substages()
    fill = [prep_tile_wide]
    for c in chunks:
        fill += [link(c), prep(c)] + [cast(j) for j in range(n_cast)[c::len(chunks)]]
    fill += [finish_chain]
    gaps = len(steps) - 1
    for i, run in enumerate(steps):
        run()
        for item in fill[i * len(fill) // gaps:(i + 1) * len(fill) // gaps] if i < gaps else ():
            item()
    bg_sc[...] = bgp_sc[...]
    gn_sc[...] = gnp_sc[...]
    bgp_sc[0] = tile['bonus']
    bgp_sc[1] = tile['g']
    gnp_sc[0:1, :] = tile['gnw']
    gnp_sc[1:2, :] = tile['gnb']


RWKV_TILE = 512
RWKV_STAGES = 3


def _rwkv_work_steps(t, c):
    return (c // LANES) * (t // RWKV_TILE)


def _cast_job(src, lo, hi, steps):
    rows = hi - lo
    count = max(n for n in range(1, steps + 1) if rows % n == 0 and (rows // n) % 16 == 0)
    blk = rows // count
    assert lo % blk == 0
    return (src, blk, lo // blk, count)


def _rwkv(g_main, g_lr, v_first, w2s, pvec, has_vres, casts=()):
    t = g_main.shape[0]
    c = pvec.shape[1]
    tt = RWKV_TILE
    ntile = t // tt
    n_work = _rwkv_work_steps(t, c)
    nsteps = n_work + RWKV_STAGES - 1
    col0 = c // LANES
    nch = tt // CHUNK

    def prep_of(s):
        ss = jnp.minimum(s, n_work - 1)
        return ss // ntile, ss % ntile

    def chain_of(s):
        cs = jnp.maximum(s - (RWKV_STAGES - 1), 0)
        return cs // ntile, cs % ntile

    def in_map(col_mult):
        return lambda s: (prep_of(s)[1], col_mult * col0 + prep_of(s)[0])

    in_specs = [pl.BlockSpec((tt, LANES), in_map(1)),
                pl.BlockSpec((tt, LANES), in_map(2)),
                pl.BlockSpec((tt, LANES), in_map(3)),
                pl.BlockSpec((tt, 2 * LR_PAD), lambda s: (prep_of(s)[1], 0))]
    args = [g_main, g_main, g_main, g_lr]
    if has_vres:
        in_specs.append(pl.BlockSpec((tt, LANES), in_map(0)))
        args.append(v_first)
    in_specs += [pl.BlockSpec((768, LANES), lambda s: (0, prep_of(s)[0])),
                 pl.BlockSpec((PV_ROWS, LANES), lambda s: (0, prep_of(s)[0]))]
    args += [w2s, pvec]
    yr_spec = pl.BlockSpec((tt, LANES), lambda s: (chain_of(s)[1], chain_of(s)[0]))
    out_shape = jax.ShapeDtypeStruct((t, c), F32)
    if has_vres:
        out_specs, out_shapes = [yr_spec], [out_shape]
    else:
        out_specs, out_shapes = [yr_spec, pl.BlockSpec((tt, LANES), in_map(0))], [out_shape, out_shape]
    for src, rows, first, count in casts:
        blk = (rows, src.shape[1])
        in_specs.append(pl.BlockSpec(blk, lambda s, first=first, count=count: (first + jnp.minimum(s, count - 1), 0)))
        args.append(src)
        out_specs.append(pl.BlockSpec(blk, lambda s, count=count: (jnp.minimum(s, count - 1), 0)))
        out_shapes.append(jax.ShapeDtypeStruct((rows * count, src.shape[1]), BF16))
    return pl.pallas_call(
        functools.partial(_rwkv_kernel, tt=tt, ntile=ntile, nsteps=nsteps, has_vres=has_vres, n_cast=len(casts)),
        grid=(nsteps,),
        in_specs=in_specs,
        out_specs=out_specs,
        out_shape=out_shapes,
        scratch_shapes=[pltpu.VMEM((LANES, LANES), F32),
                        pltpu.VMEM((SUBLANES, LANES), F32),
                        pltpu.VMEM((SUBLANES, LR_PAD), F32),
                        pltpu.VMEM((nch, 2 * LANES, LANES), BF16),
                        pltpu.VMEM((nch, LANES, LANES), F32),
                        pltpu.VMEM((nch, LANES, LANES), F32),
                        pltpu.VMEM((2, tt, LANES), F32),
                        pltpu.VMEM((SUBLANES, LANES), F32),
                        pltpu.VMEM((2, 2, nch, LANES, LANES), F32),
                        pltpu.VMEM((2, 5, nch, LANES, LANES), BF16),
                        pltpu.VMEM((2, nch, SUBLANES, LANES), F32),
                        pltpu.VMEM((2, tt, LANES), F32),
                        pltpu.VMEM((SUBLANES, LANES), F32)],
        compiler_params=_cparams(("arbitrary",)),
        name="rwkv7_vres" if has_vres else "rwkv7",
    )(*args)


def _mix_kernel(*refs, tm, n_exp):
    has_router = n_exp is not None
    if has_router:
        (x_ref, pp_ref, halo_ref, glp_ref, glr_ref, yr_ref, pw_ref, ps_ref, po_ref, ro_ref, mo_ref,
         g1_ref, sc2_ref, sh2_ref, rt_ref, xo_ref, h2_ref, idx_ref, gt_ref, ext) = refs
    else:
        (x_ref, pp_ref, halo_ref, glp_ref, glr_ref, yr_ref, pw_ref, ps_ref, po_ref, ro_ref, mo_ref,
         g1_ref, sc2_ref, sh2_ref, xo_ref, h2_ref, ext) = refs
    i = pl.program_id(0)
    ext[0:POOL_HALO, :] = jnp.where(i == 0, 0.0, halo_ref[...].astype(F32))
    ext[POOL_HALO:POOL_HALO + tm, :] = pp_ref[...].astype(F32)
    pos = i * tm + lax.broadcasted_iota(jnp.int32, (tm, 1), 0)
    gw = ext.shape[1] // len(POOL_WINDOWS)
    parts = []
    for gi, win in enumerate(POOL_WINDOWS):
        cs = slice(gi * gw, (gi + 1) * gw)
        cur = ext[POOL_HALO:POOL_HALO + tm, cs]
        s = cur
        for j in range(1, win):
            s = s + ext[POOL_HALO - j:POOL_HALO - j + tm, cs]
        cnt = jnp.minimum(pos + 1, win).astype(F32)
        pooled = s / cnt - cur
        parts.append(_mm(pooled, pw_ref[gi]))
    mixed = jnp.concatenate(parts, axis=-1) * ps_ref[...]
    y_pool = _mm(mixed, po_ref[...])
    y_rwkv = _mm(yr_ref[...], ro_ref[...])
    merged = (jax.nn.sigmoid(glp_ref[...].astype(F32)) * y_pool
              + jax.nn.sigmoid(glr_ref[...].astype(F32)) * y_rwkv)
    xn = x_ref[...] + g1_ref[...] * _mm(merged, mo_ref[...])
    xo_ref[...] = xn
    ms = jnp.mean(xn * xn, axis=-1, keepdims=True)
    h2 = xn * lax.rsqrt(ms + NORM_EPS) * (1.0 + sc2_ref[...]) + sh2_ref[...]
    if has_router:
        slab = h2.shape[1] // LANES
        for a in range(slab):
            h2_ref[pl.ds(a, tm, stride=slab), :] = h2[:, a * LANES:(a + 1) * LANES]
        hi, lo = _split(h2)
        rhi, rlo = _split(rt_ref[...])
        hi_r = jnp.dot(hi, jnp.concatenate([rhi, rlo], axis=1), preferred_element_type=F32)
        logits = hi_r[:, 0:LANES] + hi_r[:, LANES:] + jnp.dot(lo, rhi, preferred_element_type=F32)
        lane = lax.broadcasted_iota(jnp.int32, logits.shape, 1)
        lane_f = lane.astype(F32)
        big = float(LANES)
        lg = jnp.where(lane < n_exp, logits, -jnp.inf)
        m1 = jnp.max(lg, axis=-1, keepdims=True)
        i1 = jnp.min(jnp.where(lg == m1, lane_f, big), axis=-1, keepdims=True)
        lg2 = jnp.where(lane_f == i1, -jnp.inf, lg)
        m2 = jnp.max(lg2, axis=-1, keepdims=True)
        i2 = jnp.min(jnp.where(lg2 == m2, lane_f, big), axis=-1, keepdims=True)
        e2 = jnp.exp(m2 - m1)
        p1 = 1.0 / (1.0 + e2)
        p2 = e2 / (1.0 + e2)
        idx_ref[...] = jnp.where(lane == 0, i1, jnp.where(lane == 1, i2, 0.0)).astype(jnp.int32)
        gt_ref[...] = jnp.where(lane == 0, p1, jnp.where(lane == 1, p2, 0.0))
    else:
        h2_ref[...] = h2.astype(h2_ref.dtype)


def _mix(x, g_main, yr, pool_w, pool_scale, pool_out, rk_out, mix_out, gate1, scale2, shift2, router_pad, n_exp):
    t, d = x.shape
    c = yr.shape[1]
    has_router = router_pad is not None
    tm = 256 if has_router else 512
    once = dict(pipeline_mode=pl.Buffered(1))
    halo_blocks = tm // POOL_HALO
    cb_glp = (4 * c) // d
    in_specs = [pl.BlockSpec((tm, d), lambda i: (i, 0)),
                pl.BlockSpec((tm, c), lambda i: (i, 0)),
                pl.BlockSpec((POOL_HALO, c), lambda i: (jnp.maximum(i * halo_blocks - 1, 0), 0)),
                pl.BlockSpec((tm, d), lambda i: (i, cb_glp)),
                pl.BlockSpec((tm, d), lambda i: (i, cb_glp + 1)),
                pl.BlockSpec((tm, c), lambda i: (i, 0)),
                pl.BlockSpec(pool_w.shape, lambda i: (0, 0, 0), **once),
                pl.BlockSpec((1, c), lambda i: (0, 0)),
                pl.BlockSpec((c, d), lambda i: (0, 0), **once),
                pl.BlockSpec((c, d), lambda i: (0, 0), **once),
                pl.BlockSpec((d, d), lambda i: (0, 0), **once),
                pl.BlockSpec((1, d), lambda i: (0, 0)),
                pl.BlockSpec((1, d), lambda i: (0, 0)),
                pl.BlockSpec((1, d), lambda i: (0, 0))]
    args = [x, g_main, g_main, g_main, g_main, yr, pool_w, pool_scale, pool_out, rk_out, mix_out,
            gate1, scale2, shift2]
    out_specs = [pl.BlockSpec((tm, d), lambda i: (i, 0))]
    out_shape = [jax.ShapeDtypeStruct((t, d), F32)]
    if has_router:
        slab = d // LANES
        out_specs.append(pl.BlockSpec((tm * slab, LANES), lambda i: (i, 0)))
        out_shape.append(jax.ShapeDtypeStruct((t * slab, LANES), F32))
    else:
        out_specs.append(pl.BlockSpec((tm, d), lambda i: (i, 0)))
        out_shape.append(jax.ShapeDtypeStruct((t, d), BF16))
    if has_router:
        in_specs.append(pl.BlockSpec((d, LANES), lambda i: (0, 0)))
        args.append(router_pad)
        out_specs += [pl.BlockSpec((tm, LANES), lambda i: (i, 0)), pl.BlockSpec((tm, LANES), lambda i: (i, 0))]
        out_shape += [jax.ShapeDtypeStruct((t, LANES), jnp.int32), jax.ShapeDtypeStruct((t, LANES), F32)]
    return pl.pallas_call(
        functools.partial(_mix_kernel, tm=tm, n_exp=n_exp if has_router else None),
        grid=(t // tm,),
        in_specs=in_specs,
        out_specs=out_specs,
        out_shape=out_shape,
        scratch_shapes=[pltpu.VMEM((POOL_HALO + tm, c), F32)],
        compiler_params=_cparams(("arbitrary",), VMEM_LIMIT_HIGH),
        name="mix_router" if has_router else "mix",
    )(*args)


def _ffn_kernel(h_ref, x_ref, g_ref, w1_ref, w3_ref, w2_ref, o_ref, acc):
    f = pl.program_id(1)

    @pl.when(f == 0)
    def _():
        acc[...] = jnp.zeros_like(acc)

    h = h_ref[...]
    a = jnp.dot(h, w1_ref[...].astype(BF16), preferred_element_type=F32)
    b = jnp.dot(h, w3_ref[...].astype(BF16), preferred_element_type=F32)
    u = (a * jax.nn.sigmoid(a) * b).astype(BF16)
    acc[...] += jnp.dot(u, w2_ref[...].astype(BF16), preferred_element_type=F32)

    @pl.when(f == pl.num_programs(1) - 1)
    def _():
        o_ref[...] = x_ref[...] + g_ref[...] * acc[...]


def _ffn(h2, x, gate2, w1, w3, w2):
    t, d = x.shape
    ff = w1.shape[1]
    tm, tf = 512, 512
    return pl.pallas_call(
        _ffn_kernel,
        grid=(t // tm, ff // tf),
        in_specs=[pl.BlockSpec((tm, d), lambda i, f: (i, 0)),
                  pl.BlockSpec((tm, d), lambda i, f: (i, 0)),
                  pl.BlockSpec((1, d), lambda i, f: (0, 0)),
                  pl.BlockSpec((d, tf), lambda i, f: (0, f)),
                  pl.BlockSpec((d, tf), lambda i, f: (0, f)),
                  pl.BlockSpec((tf, d), lambda i, f: (f, 0))],
        out_specs=pl.BlockSpec((tm, d), lambda i, f: (i, 0)),
        out_shape=jax.ShapeDtypeStruct((t, d), F32),
        scratch_shapes=[pltpu.VMEM((tm, d), F32)],
        compiler_params=_cparams(("arbitrary", "arbitrary")),
        name="ffn_dense",
    )(h2, x, gate2, w1, w3, w2)


def _slab_copy(src_hbm, src_row, dst, slot, dst_row, slab, sem):
    src = src_hbm.at[pl.ds(pl.multiple_of(src_row * slab, slab), slab), :]
    return pltpu.make_async_copy(src, dst.at[slot, pl.ds(pl.multiple_of(dst_row * slab, slab), slab), :],
                                 sem.at[slot])


def _moe_kernel(te_ref, tr_ref, na_ref, tok_ref, h_hbm, w1_ref, w3_ref, w2l_ref, w2h_ref,
                o_ref, xg, xb, acc, sem, *, tm, n_lo):
    b = pl.program_id(0)
    f = pl.program_id(1)
    nf = pl.num_programs(1)
    n_act = na_ref[0]
    slab = xb.shape[1] // LANES

    def start_gather(tile):
        def body(s, carry):
            _slab_copy(h_hbm, tok_ref[tile * tm + s], xg, 0, s, slab, sem).start()
            return carry
        lax.fori_loop(0, tm, body, 0, unroll=8)

    @pl.when(jnp.logical_and(f == 0, b < n_act))
    def _():
        @pl.when(b == 0)
        def _():
            start_gather(0)

        pltpu.make_async_copy(h_hbm.at[pl.ds(0, tm * slab), :], xg.at[0], sem.at[0]).wait()
        for a in range(slab):
            xb[:, a * LANES:(a + 1) * LANES] = xg[0, pl.ds(a, tm, stride=slab), :].astype(BF16)
        acc[...] = jnp.zeros_like(acc)

        @pl.when(b + 1 < n_act)
        def _():
            start_gather(b + 1)

    def swiglu_step(w2_ref, rows):
        x = xb[0:rows, :]
        a_ = jnp.dot(x, w1_ref[0].astype(BF16), preferred_element_type=F32)
        b_ = jnp.dot(x, w3_ref[0].astype(BF16), preferred_element_type=F32)
        u = (a_ * jax.nn.sigmoid(a_) * b_).astype(BF16)
        acc[0:rows, :] += jnp.dot(u, w2_ref[0].astype(BF16), preferred_element_type=F32)

    in_lo = te_ref[b] < n_lo
    for rows in range(MOE_ROW_STEP, tm + 1, MOE_ROW_STEP):
        has_rows = jnp.logical_and(b < n_act, tr_ref[b] == rows)

        @pl.when(jnp.logical_and(has_rows, in_lo))
        def _():
            swiglu_step(w2l_ref, rows)

        @pl.when(jnp.logical_and(has_rows, jnp.logical_not(in_lo)))
        def _():
            swiglu_step(w2h_ref, rows)

    @pl.when(f == nf - 1)
    def _():
        res = jnp.where(b < n_act, acc[...], 0.0)
        for a in range(slab):
            o_ref[pl.ds(a, tm, stride=slab), :] = res[:, a * LANES:(a + 1) * LANES]


MOE_ROW_STEP = 128


def _moe(h2_slab, tile_e, tile_rows, n_active, slot_tok, w1, w3, w2_lo, w2_hi, tm, n_tiles):
    n_exp, d, ff = w1.shape
    n_lo = w2_lo.shape[0]
    tf = 1024
    nf = ff // tf

    def w13_map(b, f, te, tr, na, tok):
        return (te[b], 0, jnp.where(b < na[0], f, nf - 1))

    def w2_map(lo):
        def index(b, f, te, tr, na, tok):
            e = te[b]
            mine = jnp.logical_and(b < na[0], (e < n_lo) if lo else (e >= n_lo))
            e_loc = jnp.minimum(e, n_lo - 1) if lo else jnp.maximum(e - n_lo, 0)
            return (e_loc, jnp.where(mine, f, (nf - 1) if lo else 0), 0)
        return index

    grid_spec = pltpu.PrefetchScalarGridSpec(
        num_scalar_prefetch=4,
        grid=(n_tiles, nf),
        in_specs=[pl.BlockSpec(memory_space=pl.ANY),
                  pl.BlockSpec((1, d, tf), w13_map), pl.BlockSpec((1, d, tf), w13_map),
                  pl.BlockSpec((1, tf, d), w2_map(True)), pl.BlockSpec((1, tf, d), w2_map(False))],
        out_specs=pl.BlockSpec((tm * (d // LANES), LANES), lambda b, f, te, tr, na, tok: (b, 0)),
        scratch_shapes=[pltpu.VMEM((1, tm * (d // LANES), LANES), F32),
                        pltpu.VMEM((tm, d), BF16),
                        pltpu.VMEM((tm, d), F32),
                        pltpu.SemaphoreType.DMA((1,))],
    )
    return pl.pallas_call(
        functools.partial(_moe_kernel, tm=tm, n_lo=n_lo),
        grid_spec=grid_spec,
        out_shape=jax.ShapeDtypeStruct((n_tiles * tm * (d // LANES), LANES), F32),
        compiler_params=_cparams(("arbitrary", "arbitrary")),
        name="moe_experts",
    )(tile_e, tile_rows, n_active, slot_tok, h2_slab, w1, w3, w2_lo, w2_hi)


def _comb_kernel(dest_ref, yb_hbm, x_ref, gt_ref, g2_ref, fg_ref, o_ref, buf, sem, *, tm):
    i = pl.program_id(0)
    n = pl.num_programs(0)
    slot = i % 2
    nrow = TOP_K * tm
    slab = o_ref.shape[1] // LANES

    def start_gather(tile, slt):
        def body(s, carry):
            _slab_copy(yb_hbm, dest_ref[tile * nrow + s], buf, slt, s, slab, sem).start()
            return carry
        lax.fori_loop(0, nrow, body, 0, unroll=8)

    @pl.when(i == 0)
    def _():
        start_gather(0, 0)

    @pl.when(i + 1 < n)
    def _():
        start_gather(i + 1, 1 - slot)

    pltpu.make_async_copy(yb_hbm.at[pl.ds(0, nrow * slab), :], buf.at[slot], sem.at[slot]).wait()
    gt = gt_ref[...]
    p0 = gt[:, 0:1]
    p1 = gt[:, 1:2]
    cols = []
    for a in range(slab):
        y0 = buf[slot, pl.ds(a, tm, stride=TOP_K * slab), :]
        y1 = buf[slot, pl.ds(slab + a, tm, stride=TOP_K * slab), :]
        cols.append(y0 * p0 + y1 * p1)
    fmoe = jnp.concatenate(cols, axis=-1)
    xn = x_ref[...] + g2_ref[...] * fmoe
    ms = jnp.mean(xn * xn, axis=-1, keepdims=True)
    o_ref[...] = xn * lax.rsqrt(ms + NORM_EPS) * fg_ref[...]


def _combine(dest, yb_slab, x, gates, gate2, final_gain):
    t, d = x.shape
    tm = 512
    grid_spec = pltpu.PrefetchScalarGridSpec(
        num_scalar_prefetch=1,
        grid=(t // tm,),
        in_specs=[pl.BlockSpec(memory_space=pl.ANY),
                  pl.BlockSpec((tm, d), lambda i, dr: (i, 0)),
                  pl.BlockSpec((tm, LANES), lambda i, dr: (i, 0)),
                  pl.BlockSpec((1, d), lambda i, dr: (0, 0)),
                  pl.BlockSpec((1, d), lambda i, dr: (0, 0))],
        out_specs=pl.BlockSpec((tm, d), lambda i, dr: (i, 0)),
        scratch_shapes=[pltpu.VMEM((2, TOP_K * tm * (d // LANES), LANES), F32),
                        pltpu.SemaphoreType.DMA((2,))],
    )
    return pl.pallas_call(
        functools.partial(_comb_kernel, tm=tm),
        grid_spec=grid_spec,
        out_shape=jax.ShapeDtypeStruct((t, d), F32),
        compiler_params=_cparams(("arbitrary",)),
        name="moe_combine",
    )(dest, yb_slab, x, gates, gate2, final_gain)


def _pad_rows(w, rows):
    return jnp.pad(w, ((0, rows - w.shape[0]), (0, 0)))


def _lowrank_weights(l, rk_lerp, rk_w1, rk_a1, rk_g1, rk_vmu, rk_v1):
    mus = [rk_lerp[l, 0], rk_lerp[l, 1], rk_lerp[l, 2]]
    ws = [rk_w1[l], rk_a1[l], rk_g1[l]]
    if l > 0:
        mus.append(rk_vmu[l - 1])
        ws.append(rk_v1[l - 1])
    wa = jnp.concatenate([(1.0 - m)[:, None] * w for m, w in zip(mus, ws)], axis=1)
    wb = jnp.concatenate([m[:, None] * w for m, w in zip(mus, ws)], axis=1)
    wa = jnp.pad(wa, ((0, 0), (0, LR_PAD - wa.shape[1])))
    wb = jnp.pad(wb, ((0, 0), (0, LR_PAD - wb.shape[1])))
    return jnp.concatenate([wa, wb], axis=1).astype(BF16)


def _second_stage_weights(l, rk_w2, rk_a2, rk_g2, rk_v2):
    c = rk_w2.shape[2]
    w2 = _pad_rows(rk_w2[l], 128)
    a2 = jnp.concatenate([jnp.zeros((64, c), F32), rk_a2[l]], axis=0)
    g2 = _pad_rows(rk_g2[l], 256)
    if l > 0:
        v2 = jnp.concatenate([jnp.zeros((160, c), F32), rk_v2[l - 1], jnp.zeros((64, c), F32)], axis=0)
    else:
        v2 = jnp.zeros((256, c), F32)
    return jnp.concatenate([w2, a2, g2, v2], axis=0).astype(BF16)


def _route(idx, tm, n_exp, n_tiles):
    n_assign = idx.size
    flat_e = idx.reshape(-1)
    onehot = (flat_e[:, None] == jnp.arange(n_exp, dtype=jnp.int32)[None, :]).astype(jnp.int32)
    csum = jnp.cumsum(onehot, axis=0)
    rank = jnp.take_along_axis(csum, flat_e[:, None], axis=1)[:, 0] - 1
    counts = csum[-1]
    padded = (counts + tm - 1) // tm * tm
    pad_end = jnp.cumsum(padded)
    pad_start = pad_end - padded
    dest = (pad_start[flat_e] + rank).astype(jnp.int32)
    slot_tok = jnp.zeros((n_tiles * tm,), jnp.int32).at[dest].set(
        jnp.arange(n_assign, dtype=jnp.int32) // TOP_K)
    n_active = (pad_end[-1] // tm).astype(jnp.int32)
    tile_ids = jnp.minimum(jnp.arange(n_tiles, dtype=jnp.int32), jnp.maximum(n_active - 1, 0))
    tile_e = jnp.minimum(jnp.searchsorted(pad_end, tile_ids * tm, side='right'), n_exp - 1).astype(jnp.int32)
    filled = counts[tile_e] - (tile_ids * tm - pad_start[tile_e])
    tile_rows = jnp.clip((filled + MOE_ROW_STEP - 1) // MOE_ROW_STEP * MOE_ROW_STEP, MOE_ROW_STEP, tm)
    return dest, slot_tok, tile_e, tile_rows.astype(jnp.int32), n_active.reshape(1)


def kernel(x, c, ada_w, ada_b, w_in, pool_w, pool_scale, pool_out, rk_shift, rk_lerp, rk_w0, rk_w1, rk_w2, rk_a0, rk_a1, rk_a2, rk_g1, rk_g2, rk_kk, rk_ka, rk_rk, rk_gn_w, rk_gn_b, rk_vmu, rk_v0, rk_v1, rk_v2, rk_out, mix_out, ffn_w1, ffn_w3, ffn_w2, router, moe_w1, moe_w3, moe_w2, final_gain):
    bsz, t, d = x.shape
    depth = ada_w.shape[0]
    cw = rk_w0.shape[1]
    n_exp = router.shape[2]
    assert bsz == 1 and depth == 2 and cw % LANES == 0
    xs = x.reshape(t, d)
    mod = _ada(c.reshape(d, 1), ada_w, ada_b)

    def mod_row(l, i):
        return mod[l, :, i * d:(i + 1) * d]

    v_first = None
    out = None
    for l in range(depth):
        w_lr = _lowrank_weights(l, rk_lerp, rk_w1, rk_a1, rk_g1, rk_vmu, rk_v1)
        if l == 0:
            g_main, g_lr = _in_proj(xs, mod_row(l, 1), mod_row(l, 0), w_in, l, w_lr)
        else:
            g_main, g_lr = _in_proj(xs, mod_row(l, 1), mod_row(l, 0), w_in_next.reshape(1, d, -1), 0, w_lr)
        w2s = _second_stage_weights(l, rk_w2, rk_a2, rk_g2, rk_v2)
        zeros_c = jnp.zeros((cw,), F32)
        rows = [rk_w0[l], rk_a0[l], rk_v0[l - 1] if l > 0 else zeros_c, rk_kk[l], rk_ka[l],
                rk_gn_w[l], rk_gn_b[l], rk_shift[l, 0], rk_shift[l, 1], rk_shift[l, 2],
                rk_rk[l].reshape(cw)]
        pvec = _pad_rows(jnp.stack(rows, axis=0), PV_ROWS)
        steps = _rwkv_work_steps(t, cw)
        mixer = [w.reshape(-1, w.shape[-1]) for w in (pool_w, pool_out, rk_out, mix_out)]
        mixer_casts = [_cast_job(w, l * (w.shape[0] // depth), (l + 1) * (w.shape[0] // depth), steps)
                       for w in mixer]
        if l == 0:
            dense = [ffn_w1[0], ffn_w3[0], ffn_w2[0]]
            experts = [moe_w1[0].reshape(-1, moe_w1.shape[-1]), moe_w3[0].reshape(-1, moe_w3.shape[-1]),
                       moe_w2[0].reshape(-1, moe_w2.shape[-1])]
            e1, e3, e2 = experts
            casts = ([_cast_job(w, 0, w.shape[0], steps) for w in dense]
                     + [_cast_job(e1, 0, e1.shape[0], steps), _cast_job(e2, 0, e2.shape[0] // 2, steps)]
                     + [_cast_job(w_in.reshape(-1, w_in.shape[-1]), d, 2 * d, steps)] + mixer_casts)
            yr, v_first, *cast_out = _rwkv(g_main, g_lr, None, w2s, pvec, has_vres=False, casts=casts)
            dense_bf, (e1_bf, e2_lo), w_in_next, mixer_bf = cast_out[:3], cast_out[3:5], cast_out[5], cast_out[6:]
        else:
            casts = [_cast_job(e3, 0, e3.shape[0], steps),
                     _cast_job(e2, e2.shape[0] // 2, e2.shape[0], steps)] + mixer_casts
            yr, e3_bf, e2_hi, *mixer_bf = _rwkv(g_main, g_lr, v_first, w2s, pvec, has_vres=True, casts=casts)
        is_moe = (l % 2 == 1)
        router_pad = None
        if is_moe:
            router_pad = jnp.pad(router[l // 2], ((0, 0), (0, LANES - n_exp)))
        res = _mix(xs, g_main, yr, mixer_bf[0].reshape(pool_w.shape[1:]), pool_scale[l].reshape(1, -1),
                   mixer_bf[1], mixer_bf[2], mixer_bf[3], mod_row(l, 2), mod_row(l, 4), mod_row(l, 3), router_pad,
                   n_exp)
        if not is_moe:
            x_new, h2 = res
            xs = _ffn(h2, x_new, mod_row(l, 5), *dense_bf)
        else:
            x_new, h2, idx_pad, gate_pad = res
            tm_moe = 512
            n_tiles = (t * TOP_K) // tm_moe + n_exp
            dest, slot_tok, tile_e, tile_rows, n_active = _route(idx_pad[:, :TOP_K], tm_moe, n_exp, n_tiles)
            n_lo = n_exp // 2
            yb_slab = _moe(h2, tile_e, tile_rows, n_active, slot_tok,
                           e1_bf.reshape(moe_w1.shape[1:]), e3_bf.reshape(moe_w3.shape[1:]),
                           e2_lo.reshape((n_lo,) + moe_w2.shape[2:]),
                           e2_hi.reshape((n_exp - n_lo,) + moe_w2.shape[2:]), tm_moe, n_tiles)
            out = _combine(dest, yb_slab, x_new, gate_pad, mod_row(l, 5), final_gain.reshape(1, d))
    return out.reshape(bsz, t, d)
```

```python
import functools

import jax
import jax.numpy as jnp
from jax import lax
from jax.experimental import pallas as pl
from jax.experimental.pallas import tpu as pltpu

F32 = jnp.float32
BF16 = jnp.bfloat16

NORM_EPS = 1e-6
GROUPNORM_EPS = 64e-5
L2_EPS = 1e-12
POOL_WINDOWS = (2, 4, 8, 16)
HEAD_DIM = 64
TOP_K = 2
LANES = 128
SUBLANES = 8
CHUNK = 64
LR_PAD = 384
POOL_HALO = 16
VMEM_LIMIT = 56 * 1024 * 1024
VMEM_LIMIT_HIGH = 60 * 1024 * 1024


def _cparams(sem, vmem_limit=VMEM_LIMIT):
    return pltpu.CompilerParams(dimension_semantics=sem, vmem_limit_bytes=vmem_limit)


def _mm(a, b):
    return jnp.dot(a.astype(BF16), b.astype(BF16), preferred_element_type=F32)


def _mm_nt(a, b):
    return lax.dot_general(a.astype(BF16), b.astype(BF16), (((1,), (1,)), ((), ())),
                           preferred_element_type=F32)


def _split(x):
    hi = x.astype(BF16)
    lo = (x - hi.astype(F32)).astype(BF16)
    return hi, lo


def _ada_kernel(c_ref, w_ref, b_ref, o_ref):
    c = c_ref[...]
    act = c * jax.nn.sigmoid(c)
    o_ref[0] = jnp.sum(act * w_ref[0], axis=0, keepdims=True) + b_ref[0]


def _ada(c_col, ada_w, ada_b):
    depth, d, n = ada_w.shape
    tn = 1024
    return pl.pallas_call(
        _ada_kernel,
        grid=(depth, n // tn),
        in_specs=[pl.BlockSpec((d, 1), lambda l, j: (0, 0)),
                  pl.BlockSpec((1, d, tn), lambda l, j: (l, 0, j)),
                  pl.BlockSpec((1, 1, tn), lambda l, j: (l, 0, j))],
        out_specs=pl.BlockSpec((1, 1, tn), lambda l, j: (l, 0, j)),
        out_shape=jax.ShapeDtypeStruct((depth, 1, n), F32),
        compiler_params=_cparams(("arbitrary", "arbitrary")),
        name="ada_gemv",
    )(c_col, ada_w, ada_b.reshape(depth, 1, n))


def _in_kernel(x_ref, sc_ref, sh_ref, w_ref, wlr_ref, o_ref, olr_ref, h_sc):
    @pl.when(pl.program_id(1) == 0)
    def _():
        x = x_ref[...]
        ms = jnp.mean(x * x, axis=-1, keepdims=True)
        h = x * lax.rsqrt(ms + NORM_EPS) * (1.0 + sc_ref[...]) + sh_ref[...]
        hb = h.astype(BF16)
        h_sc[...] = hb
        olr_ref[...] = jnp.dot(hb, wlr_ref[...], preferred_element_type=F32)

    o_ref[...] = jnp.dot(h_sc[...], w_ref[0].astype(BF16), preferred_element_type=F32).astype(o_ref.dtype)


def _in_proj(x, scale, shift, w_in, layer, w_lr):
    t, d = x.shape
    n = w_in.shape[2]
    nlr = w_lr.shape[1]
    tm, tn = 1024, 1024
    return pl.pallas_call(
        _in_kernel,
        grid=(t // tm, n // tn),
        in_specs=[pl.BlockSpec((tm, d), lambda i, j: (i, 0)),
                  pl.BlockSpec((1, d), lambda i, j: (0, 0)),
                  pl.BlockSpec((1, d), lambda i, j: (0, 0)),
                  pl.BlockSpec((1, d, tn), lambda i, j: (layer, 0, j)),
                  pl.BlockSpec((d, nlr), lambda i, j: (0, 0))],
        out_specs=[pl.BlockSpec((tm, tn), lambda i, j: (i, j)),
                   pl.BlockSpec((tm, nlr), lambda i, j: (i, 0))],
        out_shape=[jax.ShapeDtypeStruct((t, n), BF16), jax.ShapeDtypeStruct((t, nlr), F32)],
        scratch_shapes=[pltpu.VMEM((tm, d), BF16)],
        compiler_params=_cparams(("arbitrary", "arbitrary"), VMEM_LIMIT_HIGH),
        name="in_proj",
    )(x, scale, shift, w_in, w_lr)


PV_W0, PV_A0, PV_V0, PV_KK, PV_KA, PV_GNW, PV_GNB, PV_MUR, PV_MUK, PV_MUV, PV_RK = range(11)
PV_ROWS = 16


def _rwkv_kernel(*refs, tt, ntile, nsteps, has_vres, n_cast):
    n_in = 7 if has_vres else 6
    n_out = 1 if has_vres else 2
    ins = refs[:n_in]
    cast_src = refs[n_in:n_in + n_cast]
    outs = refs[n_in + n_cast:n_in + n_cast + n_out]
    cast_dst = refs[n_in + n_cast + n_out:n_in + 2 * n_cast + n_out]
    (s_sc, c_rkv, c_lr, qm_sc, yin_sc, cc_sc, bg_sc, gn_sc,
     opf_sc, opb_sc, el_sc, bgp_sc, gnp_sc) = refs[n_in + 2 * n_cast + n_out:]
    if has_vres:
        (r_ref, k_ref, v_ref, lr_ref, vf_ref, w2_ref, pv_ref), (yr_ref,) = ins, outs
    else:
        (r_ref, k_ref, v_ref, lr_ref, w2_ref, pv_ref), (yr_ref, vfo_ref) = ins, outs
    step = pl.program_id(0)
    n_work = nsteps - 2
    is_drain = step >= n_work
    prep_tile = jnp.minimum(step, n_work - 1) % ntile
    chain_tile = jnp.maximum(step - 2, 0) % ntile
    wr = step % 2
    rd = 1 - wr
    chunks = range(tt // CHUNK)

    @pl.when(step == 0)
    def _():
        for ref in (s_sc, qm_sc, yin_sc, cc_sc, bg_sc, gn_sc, opf_sc, opb_sc, el_sc, bgp_sc, gnp_sc):
            ref[...] = jnp.zeros_like(ref)

    @pl.when(prep_tile == 0)
    def _():
        c_rkv[...] = jnp.zeros_like(c_rkv)
        c_lr[...] = jnp.zeros_like(c_lr)

    li = lax.broadcasted_iota(jnp.int32, (LANES, LANES), 0)
    lj = lax.broadcasted_iota(jnp.int32, (LANES, LANES), 1)
    same_head = ((li >> 6) == (lj >> 6))
    head_ones = jnp.where(same_head, 1.0, 0.0).astype(BF16)
    strict = (li & (CHUNK - 1)) > (lj & (CHUNK - 1))
    incl = (li & (CHUNK - 1)) >= (lj & (CHUNK - 1))
    eye = li == lj
    eye_f = jnp.where(eye, 1.0, 0.0)
    off_masks = [((li >> (s + 1)) == (lj >> (s + 1))) & (((li >> s) & 1) == 1) & (((lj >> s) & 1) == 0)
                 for s in range(CHUNK.bit_length() - 1)]

    def head_sum(x):
        return _mm(x, head_ones)

    OP_RH, OP_AH = 0, 1
    OP_BT, OP_KT, OP_VV, OP_BBT, OP_KBT = range(5)
    tile = {}

    def prep_tile_wide():
        row = lax.broadcasted_iota(jnp.int32, (tt, 1), 0)

        def shift(x, carry_row):
            return jnp.where(row == 0, carry_row, pltpu.roll(x, 1, axis=0))

        pv = pv_ref[...]

        def prow(i):
            return pv[i:i + 1, :]

        r_raw, k_raw, v_raw = (ref[...].astype(F32) for ref in (r_ref, k_ref, v_ref))
        zb = lr_ref[:, LR_PAD:2 * LR_PAD]
        c_rkv_used = jnp.where(is_drain, c_rkv[4:7, :], c_rkv[0:3, :])
        c_z = jnp.where(is_drain, c_lr[1:2, :], c_lr[0:1, :])
        r_prev = shift(r_raw, c_rkv_used[0:1, :])
        k_prev = shift(k_raw, c_rkv_used[1:2, :])
        v_prev = shift(v_raw, c_rkv_used[2:3, :])
        z = lr_ref[:, 0:LR_PAD] + shift(zb, c_z)
        c_rkv[4:7, :] = c_rkv_used
        c_lr[1:2, :] = c_z
        c_rkv[0:1, :] = r_raw[tt - 1:tt, :]
        c_rkv[1:2, :] = k_raw[tt - 1:tt, :]
        c_rkv[2:3, :] = v_raw[tt - 1:tt, :]
        c_lr[0:1, :] = zb[tt - 1:tt, :]

        r = r_raw + (r_prev - r_raw) * prow(PV_MUR)
        k = k_raw + (k_prev - k_raw) * prow(PV_MUK)
        v = v_raw + (v_prev - v_raw) * prow(PV_MUV)

        z_wa = z[:, 0:LANES]
        z_gv = z[:, LANES:LR_PAD]
        w_pre = prow(PV_W0) + _mm(jnp.tanh(z_wa), w2_ref[0:128, :])
        neg = -w_pre
        softplus = jnp.maximum(neg, 0.0) + jnp.log(1.0 + jnp.exp(-jnp.abs(neg)))
        logd = -jnp.exp(-softplus - 0.5)
        a_i = jax.nn.sigmoid(prow(PV_A0) + _mm(z_wa, w2_ref[128:256, :]))
        g = _mm(jax.nn.sigmoid(z_gv), w2_ref[256:512, :])
        if has_vres:
            v = v + (vf_ref[...] - v) * jax.nn.sigmoid(prow(PV_V0) + _mm(z_gv, w2_ref[512:768, :]))
        else:
            vfo_ref[...] = v

        kk = k * prow(PV_KK)
        kk = kk / jnp.maximum(jnp.sqrt(head_sum(kk * kk)), L2_EPS)
        k2 = k * (1.0 + (a_i - 1.0) * prow(PV_KA))
        tile.update(r=r, v=v, k2=k2, a_vec=-kk, b_vec=kk * a_i, logd=logd, g=g,
                    bonus=head_sum(r * k2 * prow(PV_RK)) * v, gnw=prow(PV_GNW), gnb=prow(PV_GNB))

    ci = lax.broadcasted_iota(jnp.int32, (CHUNK, CHUNK), 0)
    cj = lax.broadcasted_iota(jnp.int32, (CHUNK, CHUNK), 1)
    tri_incl = jnp.where(ci >= cj, 1.0, 0.0).astype(BF16)
    head0 = lax.broadcasted_iota(jnp.int32, (CHUNK, LANES), 1) < HEAD_DIM

    def stack(x):
        return jnp.concatenate([jnp.where(head0, x, 0.0), jnp.where(head0, 0.0, x)], axis=0)

    def prep(c):
        def run():
            sl = slice(c * CHUNK, (c + 1) * CHUNK)
            ld = tile['logd'][sl]
            ld_hi, ld_lo = _split(ld)
            lp = (jnp.dot(tri_incl, ld_hi, preferred_element_type=F32)
                  + jnp.dot(tri_incl, ld_lo, preferred_element_type=F32))
            le = lp[CHUNK - 1:CHUNK, :]
            e_p = jnp.exp(lp)
            e_n = jnp.exp(-lp)
            e_e = jnp.exp(le - lp)
            b_c, k_c = tile['b_vec'][sl], tile['k2'][sl]
            opf_sc[wr, OP_RH, c] = stack(tile['r'][sl] * e_p)
            opf_sc[wr, OP_AH, c] = stack(tile['a_vec'][sl] * jnp.exp(lp - ld))
            opb_sc[wr, OP_BT, c] = stack(b_c * e_n).astype(BF16)
            opb_sc[wr, OP_KT, c] = stack(k_c * e_n).astype(BF16)
            opb_sc[wr, OP_VV, c] = stack(tile['v'][sl]).astype(BF16)
            opb_sc[wr, OP_BBT, c] = stack(b_c * e_e).T.astype(BF16)
            opb_sc[wr, OP_KBT, c] = stack(k_c * e_e).T.astype(BF16)
            el_sc[wr, c] = jnp.broadcast_to(jnp.exp(le), (SUBLANES, LANES))
        return run

    def substages():
        ops = [dict(c=c) for c in chunks]

        def gram():
            for o in ops:
                c = o['c']
                ar = jnp.concatenate([opf_sc[rd, OP_AH, c], opf_sc[rd, OP_RH, c]], axis=0).astype(BF16)
                gb = _mm_nt(ar, opb_sc[rd, OP_BT, c])
                gk = _mm_nt(ar, opb_sc[rd, OP_KT, c])
                o['a_ab'] = jnp.where(strict, gb[0:LANES], 0.0)
                o['a_rb'] = jnp.where(incl, gb[LANES:], 0.0).astype(BF16)
                o['a_ak'] = jnp.where(strict, gk[0:LANES], 0.0).astype(BF16)
                o['a_rk'] = jnp.where(incl, gk[LANES:], 0.0).astype(BF16)

        def values():
            for o in ops:
                vv = opb_sc[rd, OP_VV, o['c']]
                o['x'] = _mm(o['a_ak'], vv)
                o['yv'] = _mm(o['a_rk'], vv)
                o['cv'] = _mm(opb_sc[rd, OP_KBT, o['c']], vv)
                o['t'] = eye_f + jnp.where(off_masks[0], o['a_ab'], 0.0)

        def level_a(off):
            def run():
                for o in ops:
                    o['ta'] = _mm(o['t'], jnp.where(off, o['a_ab'], 0.0))
            return run

        def level_b():
            for o in ops:
                o['t'] = o['t'] + _mm(o['ta'], o['t'])

        def apply_inverse():
            for o in ops:
                ah_x = jnp.concatenate([opf_sc[rd, OP_AH, o['c']], o['x']], axis=1)
                o['w'] = _mm(o['t'], ah_x).astype(BF16)

        def emit():
            for o in ops:
                c = o['c']
                qy = _mm(o['a_rb'], o['w'])
                mc = _mm(opb_sc[rd, OP_BBT, c], o['w'])
                diag = jnp.where(eye, el_sc[rd, c][0:1, :], 0.0)
                qm_sc[c] = jnp.concatenate([opf_sc[rd, OP_RH, c] + qy[:, 0:LANES], diag + mc[:, 0:LANES]],
                                           axis=0).astype(BF16)
                yin_sc[c] = qy[:, LANES:] + o['yv']
                cc_sc[c] = mc[:, LANES:] + o['cv']

        steps = [gram, values]
        for off in off_masks[1:]:
            steps += [level_a(off), level_b]
        return steps + [apply_inverse, emit]

    chain = dict(s=jnp.where(chain_tile == 0, 0.0, s_sc[...]), ys=[])

    def link(c):
        def run():
            res = _mm(qm_sc[c], chain['s'])
            y2 = res[0:LANES] + yin_sc[c]
            chain['ys'].append(y2[0:CHUNK] + y2[CHUNK:2 * CHUNK])
            chain['s'] = res[LANES:] + cc_sc[c]
        return run

    def finish_chain():
        s_sc[...] = chain['s']
        y = jnp.concatenate(chain['ys'], axis=0)
        mean = head_sum(y) * (1.0 / HEAD_DIM)
        dlt = y - mean
        var = head_sum(dlt * dlt) * (1.0 / HEAD_DIM)
        yn = dlt * lax.rsqrt(var + GROUPNORM_EPS) * gn_sc[0:1, :] + gn_sc[1:2, :]
        yr_ref[...] = (yn + bg_sc[0]) * bg_sc[1]

    def cast(j):
        def run():
            cast_dst[j][...] = cast_src[j][...].astype(BF16)
        return run

    steps = substages()
    fill = [prep_tile_wide]
    for c in chunks:
        fill += [link(c), prep(c)] + [cast(j) for j in range(n_cast)[c::len(chunks)]]
    fill += [finish_chain]
    gaps = len(steps) - 1
    for i, run in enumerate(steps):
        run()
        for item in fill[i * len(fill) // gaps:(i + 1) * len(fill) // gaps] if i < gaps else ():
            item()
    bg_sc[...] = bgp_sc[...]
    gn_sc[...] = gnp_sc[...]
    bgp_sc[0] = tile['bonus']
    bgp_sc[1] = tile['g']
    gnp_sc[0:1, :] = tile['gnw']
    gnp_sc[1:2, :] = tile['gnb']


RWKV_TILE = 512
RWKV_STAGES = 3


def _rwkv_work_steps(t, c):
    return (c // LANES) * (t // RWKV_TILE)


def _cast_job(src, lo, hi, steps):
    rows = hi - lo
    count = max(n for n in range(1, steps + 1) if rows % n == 0 and (rows // n) % 16 == 0)
    blk = rows // count
    assert lo % blk == 0
    return (src, blk, lo // blk, count)


def _rwkv(g_main, g_lr, v_first, w2s, pvec, has_vres, casts=()):
    t = g_main.shape[0]
    c = pvec.shape[1]
    tt = RWKV_TILE
    ntile = t // tt
    n_work = _rwkv_work_steps(t, c)
    nsteps = n_work + RWKV_STAGES - 1
    col0 = c // LANES
    nch = tt // CHUNK

    def prep_of(s):
        ss = jnp.minimum(s, n_work - 1)
        return ss // ntile, ss % ntile

    def chain_of(s):
        cs = jnp.maximum(s - (RWKV_STAGES - 1), 0)
        return cs // ntile, cs % ntile

    def in_map(col_mult):
        return lambda s: (prep_of(s)[1], col_mult * col0 + prep_of(s)[0])

    in_specs = [pl.BlockSpec((tt, LANES), in_map(1)),
                pl.BlockSpec((tt, LANES), in_map(2)),
                pl.BlockSpec((tt, LANES), in_map(3)),
                pl.BlockSpec((tt, 2 * LR_PAD), lambda s: (prep_of(s)[1], 0))]
    args = [g_main, g_main, g_main, g_lr]
    if has_vres:
        in_specs.append(pl.BlockSpec((tt, LANES), in_map(0)))
        args.append(v_first)
    in_specs += [pl.BlockSpec((768, LANES), lambda s: (0, prep_of(s)[0])),
                 pl.BlockSpec((PV_ROWS, LANES), lambda s: (0, prep_of(s)[0]))]
    args += [w2s, pvec]
    yr_spec = pl.BlockSpec((tt, LANES), lambda s: (chain_of(s)[1], chain_of(s)[0]))
    out_shape = jax.ShapeDtypeStruct((t, c), F32)
    if has_vres:
        out_specs, out_shapes = [yr_spec], [out_shape]
    else:
        out_specs, out_shapes = [yr_spec, pl.BlockSpec((tt, LANES), in_map(0))], [out_shape, out_shape]
    for src, rows, first, count in casts:
        blk = (rows, src.shape[1])
        in_specs.append(pl.BlockSpec(blk, lambda s, first=first, count=count: (first + jnp.minimum(s, count - 1), 0)))
        args.append(src)
        out_specs.append(pl.BlockSpec(blk, lambda s, count=count: (jnp.minimum(s, count - 1), 0)))
        out_shapes.append(jax.ShapeDtypeStruct((rows * count, src.shape[1]), BF16))
    return pl.pallas_call(
        functools.partial(_rwkv_kernel, tt=tt, ntile=ntile, nsteps=nsteps, has_vres=has_vres, n_cast=len(casts)),
        grid=(nsteps,),
        in_specs=in_specs,
        out_specs=out_specs,
        out_shape=out_shapes,
        scratch_shapes=[pltpu.VMEM((LANES, LANES), F32),
                        pltpu.VMEM((SUBLANES, LANES), F32),
                        pltpu.VMEM((SUBLANES, LR_PAD), F32),
                        pltpu.VMEM((nch, 2 * LANES, LANES), BF16),
                        pltpu.VMEM((nch, LANES, LANES), F32),
                        pltpu.VMEM((nch, LANES, LANES), F32),
                        pltpu.VMEM((2, tt, LANES), F32),
                        pltpu.VMEM((SUBLANES, LANES), F32),
                        pltpu.VMEM((2, 2, nch, LANES, LANES), F32),
                        pltpu.VMEM((2, 5, nch, LANES, LANES), BF16),
                        pltpu.VMEM((2, nch, SUBLANES, LANES), F32),
                        pltpu.VMEM((2, tt, LANES), F32),
                        pltpu.VMEM((SUBLANES, LANES), F32)],
        compiler_params=_cparams(("arbitrary",)),
        name="rwkv7_vres" if has_vres else "rwkv7",
    )(*args)


def _mix_kernel(*refs, tm, n_exp):
    has_router = n_exp is not None
    if has_router:
        (x_ref, pp_ref, halo_ref, glp_ref, glr_ref, yr_ref, pw_ref, ps_ref, po_ref, ro_ref, mo_ref,
         g1_ref, sc2_ref, sh2_ref, rt_ref, xo_ref, h2_ref, idx_ref, gt_ref, ext) = refs
    else:
        (x_ref, pp_ref, halo_ref, glp_ref, glr_ref, yr_ref, pw_ref, ps_ref, po_ref, ro_ref, mo_ref,
         g1_ref, sc2_ref, sh2_ref, xo_ref, h2_ref, ext) = refs
    i = pl.program_id(0)
    ext[0:POOL_HALO, :] = jnp.where(i == 0, 0.0, halo_ref[...].astype(F32))
    ext[POOL_HALO:POOL_HALO + tm, :] = pp_ref[...].astype(F32)
    pos = i * tm + lax.broadcasted_iota(jnp.int32, (tm, 1), 0)
    gw = ext.shape[1] // len(POOL_WINDOWS)
    parts = []
    for gi, win in enumerate(POOL_WINDOWS):
        cs = slice(gi * gw, (gi + 1) * gw)
        cur = ext[POOL_HALO:POOL_HALO + tm, cs]
        s = cur
        for j in range(1, win):
            s = s + ext[POOL_HALO - j:POOL_HALO - j + tm, cs]
        cnt = jnp.minimum(pos + 1, win).astype(F32)
        pooled = s / cnt - cur
        parts.append(_mm(pooled, pw_ref[gi]))
    mixed = jnp.concatenate(parts, axis=-1) * ps_ref[...]
    y_pool = _mm(mixed, po_ref[...])
    y_rwkv = _mm(yr_ref[...], ro_ref[...])
    merged = (jax.nn.sigmoid(glp_ref[...].astype(F32)) * y_pool
              + jax.nn.sigmoid(glr_ref[...].astype(F32)) * y_rwkv)
    xn = x_ref[...] + g1_ref[...] * _mm(merged, mo_ref[...])
    xo_ref[...] = xn
    ms = jnp.mean(xn * xn, axis=-1, keepdims=True)
    h2 = xn * lax.rsqrt(ms + NORM_EPS) * (1.0 + sc2_ref[...]) + sh2_ref[...]
    if has_router:
        slab = h2.shape[1] // LANES
        for a in range(slab):
            h2_ref[pl.ds(a, tm, stride=slab), :] = h2[:, a * LANES:(a + 1) * LANES]
        hi, lo = _split(h2)
        rhi, rlo = _split(rt_ref[...])
        hi_r = jnp.dot(hi, jnp.concatenate([rhi, rlo], axis=1), preferred_element_type=F32)
        logits = hi_r[:, 0:LANES] + hi_r[:, LANES:] + jnp.dot(lo, rhi, preferred_element_type=F32)
        lane = lax.broadcasted_iota(jnp.int32, logits.shape, 1)
        lane_f = lane.astype(F32)
        big = float(LANES)
        lg = jnp.where(lane < n_exp, logits, -jnp.inf)
        m1 = jnp.max(lg, axis=-1, keepdims=True)
        i1 = jnp.min(jnp.where(lg == m1, lane_f, big), axis=-1, keepdims=True)
        lg2 = jnp.where(lane_f == i1, -jnp.inf, lg)
        m2 = jnp.max(lg2, axis=-1, keepdims=True)
        i2 = jnp.min(jnp.where(lg2 == m2, lane_f, big), axis=-1, keepdims=True)
        e2 = jnp.exp(m2 - m1)
        p1 = 1.0 / (1.0 + e2)
        p2 = e2 / (1.0 + e2)
        idx_ref[...] = jnp.where(lane == 0, i1, jnp.where(lane == 1, i2, 0.0)).astype(jnp.int32)
        gt_ref[...] = jnp.where(lane == 0, p1, jnp.where(lane == 1, p2, 0.0))
    else:
        h2_ref[...] = h2.astype(h2_ref.dtype)


def _mix(x, g_main, yr, pool_w, pool_scale, pool_out, rk_out, mix_out, gate1, scale2, shift2, router_pad, n_exp):
    t, d = x.shape
    c = yr.shape[1]
    has_router = router_pad is not None
    tm = 256 if has_router else 512
    once = dict(pipeline_mode=pl.Buffered(1))
    halo_blocks = tm // POOL_HALO
    cb_glp = (4 * c) // d
    in_specs = [pl.BlockSpec((tm, d), lambda i: (i, 0)),
                pl.BlockSpec((tm, c), lambda i: (i, 0)),
                pl.BlockSpec((POOL_HALO, c), lambda i: (jnp.maximum(i * halo_blocks - 1, 0), 0)),
                pl.BlockSpec((tm, d), lambda i: (i, cb_glp)),
                pl.BlockSpec((tm, d), lambda i: (i, cb_glp + 1)),
                pl.BlockSpec((tm, c), lambda i: (i, 0)),
                pl.BlockSpec(pool_w.shape, lambda i: (0, 0, 0), **once),
                pl.BlockSpec((1, c), lambda i: (0, 0)),
                pl.BlockSpec((c, d), lambda i: (0, 0), **once),
                pl.BlockSpec((c, d), lambda i: (0, 0), **once),
                pl.BlockSpec((d, d), lambda i: (0, 0), **once),
                pl.BlockSpec((1, d), lambda i: (0, 0)),
                pl.BlockSpec((1, d), lambda i: (0, 0)),
                pl.BlockSpec((1, d), lambda i: (0, 0))]
    args = [x, g_main, g_main, g_main, g_main, yr, pool_w, pool_scale, pool_out, rk_out, mix_out,
            gate1, scale2, shift2]
    out_specs = [pl.BlockSpec((tm, d), lambda i: (i, 0))]
    out_shape = [jax.ShapeDtypeStruct((t, d), F32)]
    if has_router:
        slab = d // LANES
        out_specs.append(pl.BlockSpec((tm * slab, LANES), lambda i: (i, 0)))
        out_shape.append(jax.ShapeDtypeStruct((t * slab, LANES), F32))
    else:
        out_specs.append(pl.BlockSpec((tm, d), lambda i: (i, 0)))
        out_shape.append(jax.ShapeDtypeStruct((t, d), BF16))
    if has_router:
        in_specs.append(pl.BlockSpec((d, LANES), lambda i: (0, 0)))
        args.append(router_pad)
        out_specs += [pl.BlockSpec((tm, LANES), lambda i: (i, 0)), pl.BlockSpec((tm, LANES), lambda i: (i, 0))]
        out_shape += [jax.ShapeDtypeStruct((t, LANES), jnp.int32), jax.ShapeDtypeStruct((t, LANES), F32)]
    return pl.pallas_call(
        functools.partial(_mix_kernel, tm=tm, n_exp=n_exp if has_router else None),
        grid=(t // tm,),
        in_specs=in_specs,
        out_specs=out_specs,
        out_shape=out_shape,
        scratch_shapes=[pltpu.VMEM((POOL_HALO + tm, c), F32)],
        compiler_params=_cparams(("arbitrary",), VMEM_LIMIT_HIGH),
        name="mix_router" if has_router else "mix",
    )(*args)


def _ffn_kernel(h_ref, x_ref, g_ref, w1_ref, w3_ref, w2_ref, o_ref, acc):
    f = pl.program_id(1)

    @pl.when(f == 0)
    def _():
        acc[...] = jnp.zeros_like(acc)

    h = h_ref[...]
    a = jnp.dot(h, w1_ref[...].astype(BF16), preferred_element_type=F32)
    b = jnp.dot(h, w3_ref[...].astype(BF16), preferred_element_type=F32)
    u = (a * jax.nn.sigmoid(a) * b).astype(BF16)
    acc[...] += jnp.dot(u, w2_ref[...].astype(BF16), preferred_element_type=F32)

    @pl.when(f == pl.num_programs(1) - 1)
    def _():
        o_ref[...] = x_ref[...] + g_ref[...] * acc[...]


def _ffn(h2, x, gate2, w1, w3, w2):
    t, d = x.shape
    ff = w1.shape[1]
    tm, tf = 512, 512
    return pl.pallas_call(
        _ffn_kernel,
        grid=(t // tm, ff // tf),
        in_specs=[pl.BlockSpec((tm, d), lambda i, f: (i, 0)),
                  pl.BlockSpec((tm, d), lambda i, f: (i, 0)),
                  pl.BlockSpec((1, d), lambda i, f: (0, 0)),
                  pl.BlockSpec((d, tf), lambda i, f: (0, f)),
                  pl.BlockSpec((d, tf), lambda i, f: (0, f)),
                  pl.BlockSpec((tf, d), lambda i, f: (f, 0))],
        out_specs=pl.BlockSpec((tm, d), lambda i, f: (i, 0)),
        out_shape=jax.ShapeDtypeStruct((t, d), F32),
        scratch_shapes=[pltpu.VMEM((tm, d), F32)],
        compiler_params=_cparams(("arbitrary", "arbitrary")),
        name="ffn_dense",
    )(h2, x, gate2, w1, w3, w2)


GATHER_UNROLL = 8


def _slab_copy(src_hbm, src_row, dst, slot, dst_row, slab, sem):
    src = src_hbm.at[pl.ds(pl.multiple_of(src_row * slab, slab), slab), :]
    return pltpu.make_async_copy(src, dst.at[slot, pl.ds(pl.multiple_of(dst_row * slab, slab), slab), :],
                                 sem.at[slot])


def _moe_kernel(te_ref, tr_ref, na_ref, tok_ref, h_hbm, w1_ref, w3_ref, w2l_ref, w2h_ref,
                o_ref, xg, xb, acc, sem, *, tm, n_lo):
    b = pl.program_id(0)
    f = pl.program_id(1)
    nf = pl.num_programs(1)
    n_act = na_ref[0]
    slab = xb.shape[1] // LANES

    def start_gather(tile):
        def body(g, carry):
            for u in range(GATHER_UNROLL):
                s = g * GATHER_UNROLL + u
                _slab_copy(h_hbm, tok_ref[tile * tm + s], xg, 0, s, slab, sem).start(priority=u % 2)
            return carry
        lax.fori_loop(0, tm // GATHER_UNROLL, body, 0)

    @pl.when(jnp.logical_and(f == 0, b < n_act))
    def _():
        @pl.when(b == 0)
        def _():
            start_gather(0)

        pltpu.make_async_copy(h_hbm.at[pl.ds(0, tm * slab), :], xg.at[0], sem.at[0]).wait()
        for a in range(slab):
            xb[:, a * LANES:(a + 1) * LANES] = xg[0, pl.ds(a, tm, stride=slab), :].astype(BF16)
        acc[...] = jnp.zeros_like(acc)

        @pl.when(b + 1 < n_act)
        def _():
            start_gather(b + 1)

    def swiglu_step(w2_ref, rows):
        x = xb[0:rows, :]
        a_ = jnp.dot(x, w1_ref[0].astype(BF16), preferred_element_type=F32)
        b_ = jnp.dot(x, w3_ref[0].astype(BF16), preferred_element_type=F32)
        u = (a_ * jax.nn.sigmoid(a_) * b_).astype(BF16)
        acc[0:rows, :] += jnp.dot(u, w2_ref[0].astype(BF16), preferred_element_type=F32)

    in_lo = te_ref[b] < n_lo
    for rows in range(MOE_ROW_STEP, tm + 1, MOE_ROW_STEP):
        has_rows = jnp.logical_and(b < n_act, tr_ref[b] == rows)

        @pl.when(jnp.logical_and(has_rows, in_lo))
        def _():
            swiglu_step(w2l_ref, rows)

        @pl.when(jnp.logical_and(has_rows, jnp.logical_not(in_lo)))
        def _():
            swiglu_step(w2h_ref, rows)

    @pl.when(f == nf - 1)
    def _():
        res = jnp.where(b < n_act, acc[...], 0.0)
        for a in range(slab):
            o_ref[pl.ds(a, tm, stride=slab), :] = res[:, a * LANES:(a + 1) * LANES]


MOE_ROW_STEP = 128


def _moe(h2_slab, tile_e, tile_rows, n_active, slot_tok, w1, w3, w2_lo, w2_hi, tm, n_tiles):
    n_exp, d, ff = w1.shape
    n_lo = w2_lo.shape[0]
    tf = 1024
    nf = ff // tf

    def w13_map(b, f, te, tr, na, tok):
        return (te[b], 0, jnp.where(b < na[0], f, nf - 1))

    def w2_map(lo):
        def index(b, f, te, tr, na, tok):
            e = te[b]
            mine = jnp.logical_and(b < na[0], (e < n_lo) if lo else (e >= n_lo))
            e_loc = jnp.minimum(e, n_lo - 1) if lo else jnp.maximum(e - n_lo, 0)
            return (e_loc, jnp.where(mine, f, (nf - 1) if lo else 0), 0)
        return index

    grid_spec = pltpu.PrefetchScalarGridSpec(
        num_scalar_prefetch=4,
        grid=(n_tiles, nf),
        in_specs=[pl.BlockSpec(memory_space=pl.ANY),
                  pl.BlockSpec((1, d, tf), w13_map), pl.BlockSpec((1, d, tf), w13_map),
                  pl.BlockSpec((1, tf, d), w2_map(True)), pl.BlockSpec((1, tf, d), w2_map(False))],
        out_specs=pl.BlockSpec((tm * (d // LANES), LANES), lambda b, f, te, tr, na, tok: (b, 0)),
        scratch_shapes=[pltpu.VMEM((1, tm * (d // LANES), LANES), F32),
                        pltpu.VMEM((tm, d), BF16),
                        pltpu.VMEM((tm, d), F32),
                        pltpu.SemaphoreType.DMA((1,))],
    )
    return pl.pallas_call(
        functools.partial(_moe_kernel, tm=tm, n_lo=n_lo),
        grid_spec=grid_spec,
        out_shape=jax.ShapeDtypeStruct((n_tiles * tm * (d // LANES), LANES), F32),
        compiler_params=_cparams(("arbitrary", "arbitrary")),
        name="moe_experts",
    )(tile_e, tile_rows, n_active, slot_tok, h2_slab, w1, w3, w2_lo, w2_hi)


def _comb_kernel(dest_ref, yb_hbm, x_ref, gt_ref, g2_ref, fg_ref, o_ref, buf, sem, *, tm):
    i = pl.program_id(0)
    n = pl.num_programs(0)
    slot = i % 2
    nrow = TOP_K * tm
    slab = o_ref.shape[1] // LANES

    def start_gather(tile, slt):
        def body(g, carry):
            for u in range(GATHER_UNROLL):
                s = g * GATHER_UNROLL + u
                _slab_copy(yb_hbm, dest_ref[tile * nrow + s], buf, slt, s, slab, sem).start(priority=u % 2)
            return carry
        lax.fori_loop(0, nrow // GATHER_UNROLL, body, 0)

    @pl.when(i == 0)
    def _():
        start_gather(0, 0)

    @pl.when(i + 1 < n)
    def _():
        start_gather(i + 1, 1 - slot)

    pltpu.make_async_copy(yb_hbm.at[pl.ds(0, nrow * slab), :], buf.at[slot], sem.at[slot]).wait()
    gt = gt_ref[...]
    p0 = gt[:, 0:1]
    p1 = gt[:, 1:2]
    cols = []
    for a in range(slab):
        y0 = buf[slot, pl.ds(a, tm, stride=TOP_K * slab), :]
        y1 = buf[slot, pl.ds(slab + a, tm, stride=TOP_K * slab), :]
        cols.append(y0 * p0 + y1 * p1)
    fmoe = jnp.concatenate(cols, axis=-1)
    xn = x_ref[...] + g2_ref[...] * fmoe
    ms = jnp.mean(xn * xn, axis=-1, keepdims=True)
    o_ref[...] = xn * lax.rsqrt(ms + NORM_EPS) * fg_ref[...]


def _combine(dest, yb_slab, x, gates, gate2, final_gain):
    t, d = x.shape
    tm = 256
    grid_spec = pltpu.PrefetchScalarGridSpec(
        num_scalar_prefetch=1,
        grid=(t // tm,),
        in_specs=[pl.BlockSpec(memory_space=pl.ANY),
                  pl.BlockSpec((tm, d), lambda i, dr: (i, 0)),
                  pl.BlockSpec((tm, LANES), lambda i, dr: (i, 0)),
                  pl.BlockSpec((1, d), lambda i, dr: (0, 0)),
                  pl.BlockSpec((1, d), lambda i, dr: (0, 0))],
        out_specs=pl.BlockSpec((tm, d), lambda i, dr: (i, 0)),
        scratch_shapes=[pltpu.VMEM((2, TOP_K * tm * (d // LANES), LANES), F32),
                        pltpu.SemaphoreType.DMA((2,))],
    )
    return pl.pallas_call(
        functools.partial(_comb_kernel, tm=tm),
        grid_spec=grid_spec,
        out_shape=jax.ShapeDtypeStruct((t, d), F32),
        compiler_params=_cparams(("arbitrary",)),
        name="moe_combine",
    )(dest, yb_slab, x, gates, gate2, final_gain)


def _pad_rows(w, rows):
    return jnp.pad(w, ((0, rows - w.shape[0]), (0, 0)))


def _lowrank_weights(l, rk_lerp, rk_w1, rk_a1, rk_g1, rk_vmu, rk_v1):
    mus = [rk_lerp[l, 0], rk_lerp[l, 1], rk_lerp[l, 2]]
    ws = [rk_w1[l], rk_a1[l], rk_g1[l]]
    if l > 0:
        mus.append(rk_vmu[l - 1])
        ws.append(rk_v1[l - 1])
    wa = jnp.concatenate([(1.0 - m)[:, None] * w for m, w in zip(mus, ws)], axis=1)
    wb = jnp.concatenate([m[:, None] * w for m, w in zip(mus, ws)], axis=1)
    wa = jnp.pad(wa, ((0, 0), (0, LR_PAD - wa.shape[1])))
    wb = jnp.pad(wb, ((0, 0), (0, LR_PAD - wb.shape[1])))
    return jnp.concatenate([wa, wb], axis=1).astype(BF16)


def _second_stage_weights(l, rk_w2, rk_a2, rk_g2, rk_v2):
    c = rk_w2.shape[2]
    w2 = _pad_rows(rk_w2[l], 128)
    a2 = jnp.concatenate([jnp.zeros((64, c), F32), rk_a2[l]], axis=0)
    g2 = _pad_rows(rk_g2[l], 256)
    if l > 0:
        v2 = jnp.concatenate([jnp.zeros((160, c), F32), rk_v2[l - 1], jnp.zeros((64, c), F32)], axis=0)
    else:
        v2 = jnp.zeros((256, c), F32)
    return jnp.concatenate([w2, a2, g2, v2], axis=0).astype(BF16)


def _route(idx, tm, n_exp, n_tiles):
    n_assign = idx.size
    flat_e = idx.reshape(-1)
    onehot = (flat_e[:, None] == jnp.arange(n_exp, dtype=jnp.int32)[None, :]).astype(jnp.int32)
    csum = jnp.cumsum(onehot, axis=0)
    rank = jnp.take_along_axis(csum, flat_e[:, None], axis=1)[:, 0] - 1
    counts = csum[-1]
    padded = (counts + tm - 1) // tm * tm
    pad_end = jnp.cumsum(padded)
    pad_start = pad_end - padded
    dest = (pad_start[flat_e] + rank).astype(jnp.int32)
    slot_tok = jnp.zeros((n_tiles * tm,), jnp.int32).at[dest].set(
        jnp.arange(n_assign, dtype=jnp.int32) // TOP_K)
    n_active = (pad_end[-1] // tm).astype(jnp.int32)
    tile_ids = jnp.minimum(jnp.arange(n_tiles, dtype=jnp.int32), jnp.maximum(n_active - 1, 0))
    tile_e = jnp.minimum(jnp.searchsorted(pad_end, tile_ids * tm, side='right'), n_exp - 1).astype(jnp.int32)
    filled = counts[tile_e] - (tile_ids * tm - pad_start[tile_e])
    tile_rows = jnp.clip((filled + MOE_ROW_STEP - 1) // MOE_ROW_STEP * MOE_ROW_STEP, MOE_ROW_STEP, tm)
    return dest, slot_tok, tile_e, tile_rows.astype(jnp.int32), n_active.reshape(1)


def kernel(x, c, ada_w, ada_b, w_in, pool_w, pool_scale, pool_out, rk_shift, rk_lerp, rk_w0, rk_w1, rk_w2, rk_a0, rk_a1, rk_a2, rk_g1, rk_g2, rk_kk, rk_ka, rk_rk, rk_gn_w, rk_gn_b, rk_vmu, rk_v0, rk_v1, rk_v2, rk_out, mix_out, ffn_w1, ffn_w3, ffn_w2, router, moe_w1, moe_w3, moe_w2, final_gain):
    bsz, t, d = x.shape
    depth = ada_w.shape[0]
    cw = rk_w0.shape[1]
    n_exp = router.shape[2]
    assert bsz == 1 and depth == 2 and cw % LANES == 0
    xs = x.reshape(t, d)
    mod = _ada(c.reshape(d, 1), ada_w, ada_b)

    def mod_row(l, i):
        return mod[l, :, i * d:(i + 1) * d]

    v_first = None
    out = None
    for l in range(depth):
        w_lr = _lowrank_weights(l, rk_lerp, rk_w1, rk_a1, rk_g1, rk_vmu, rk_v1)
        if l == 0:
            g_main, g_lr = _in_proj(xs, mod_row(l, 1), mod_row(l, 0), w_in, l, w_lr)
        else:
            g_main, g_lr = _in_proj(xs, mod_row(l, 1), mod_row(l, 0), w_in_next.reshape(1, d, -1), 0, w_lr)
        w2s = _second_stage_weights(l, rk_w2, rk_a2, rk_g2, rk_v2)
        zeros_c = jnp.zeros((cw,), F32)
        rows = [rk_w0[l], rk_a0[l], rk_v0[l - 1] if l > 0 else zeros_c, rk_kk[l], rk_ka[l],
                rk_gn_w[l], rk_gn_b[l], rk_shift[l, 0], rk_shift[l, 1], rk_shift[l, 2],
                rk_rk[l].reshape(cw)]
        pvec = _pad_rows(jnp.stack(rows, axis=0), PV_ROWS)
        steps = _rwkv_work_steps(t, cw)
        mixer = [w.reshape(-1, w.shape[-1]) for w in (pool_w, pool_out, rk_out, mix_out)]
        mixer_casts = [_cast_job(w, l * (w.shape[0] // depth), (l + 1) * (w.shape[0] // depth), steps)
                       for w in mixer]
        if l == 0:
            dense = [ffn_w1[0], ffn_w3[0], ffn_w2[0]]
            experts = [moe_w1[0].reshape(-1, moe_w1.shape[-1]), moe_w3[0].reshape(-1, moe_w3.shape[-1]),
                       moe_w2[0].reshape(-1, moe_w2.shape[-1])]
            e1, e3, e2 = experts
            casts = ([_cast_job(w, 0, w.shape[0], steps) for w in dense]
                     + [_cast_job(e1, 0, e1.shape[0], steps), _cast_job(e2, 0, e2.shape[0] // 2, steps)]
                     + [_cast_job(w_in.reshape(-1, w_in.shape[-1]), d, 2 * d, steps)] + mixer_casts)
            yr, v_first, *cast_out = _rwkv(g_main, g_lr, None, w2s, pvec, has_vres=False, casts=casts)
            dense_bf, (e1_bf, e2_lo), w_in_next, mixer_bf = cast_out[:3], cast_out[3:5], cast_out[5], cast_out[6:]
        else:
            casts = [_cast_job(e3, 0, e3.shape[0], steps),
                     _cast_job(e2, e2.shape[0] // 2, e2.shape[0], steps)] + mixer_casts
            yr, e3_bf, e2_hi, *mixer_bf = _rwkv(g_main, g_lr, v_first, w2s, pvec, has_vres=True, casts=casts)
        is_moe = (l % 2 == 1)
        router_pad = None
        if is_moe:
            router_pad = jnp.pad(router[l // 2], ((0, 0), (0, LANES - n_exp)))
        res = _mix(xs, g_main, yr, mixer_bf[0].reshape(pool_w.shape[1:]), pool_scale[l].reshape(1, -1),
                   mixer_bf[1], mixer_bf[2], mixer_bf[3], mod_row(l, 2), mod_row(l, 4), mod_row(l, 3), router_pad,
                   n_exp)
        if not is_moe:
            x_new, h2 = res
            xs = _ffn(h2, x_new, mod_row(l, 5), *dense_bf)
        else:
            x_new, h2, idx_pad, gate_pad = res
            tm_moe = 512
            n_tiles = (t * TOP_K) // tm_moe + n_exp
            dest, slot_tok, tile_e, tile_rows, n_active = _route(idx_pad[:, :TOP_K], tm_moe, n_exp, n_tiles)
            n_lo = n_exp // 2
            yb_slab = _moe(h2, tile_e, tile_rows, n_active, slot_tok,
                           e1_bf.reshape(moe_w1.shape[1:]), e3_bf.reshape(moe_w3.shape[1:]),
                           e2_lo.reshape((n_lo,) + moe_w2.shape[2:]),
                           e2_hi.reshape((n_exp - n_lo,) + moe_w2.shape[2:]), tm_moe, n_tiles)
            out = _combine(dest, yb_slab, x_new, gate_pad, mod_row(l, 5), final_gain.reshape(1, d))
    return out.reshape(bsz, t, d)
```
